```python
import math
import jax
import jax.numpy as jnp
from jax import lax
import numpy as np


D_MODEL = 1024
BATCH = 1
SEQ = 16384
DEPTH = 4

GRID_W = 64
CTX_LEN = 256
EPS = 1e-6
ROPE_THETA = 10000.0
Q_BLOCK = 128

MLA_HEADS = 8
MLA_NOPE = 64
MLA_ROPE = 32
MLA_V = 64
MLA_Q_RANK = 384
MLA_KV_RANK = 256
MLA_QK = MLA_NOPE + MLA_ROPE
MLA_WIDTH = MLA_HEADS * MLA_V

NA_HEADS = 8
NA_DIM = 64
NA_KH = 8
NA_KW = 16
NA_WIDTH = NA_HEADS * NA_DIM

S5_GROUPS = 32
S5_GCH = 16
S5_STATE = 64
S5_WIDTH = S5_GROUPS * S5_GCH

N_BRANCH = 3
D_FF = 2816
N_EXPERTS = 8
TOP_K = 2

IN_SPLITS = (MLA_Q_RANK, MLA_KV_RANK, MLA_ROPE, NA_WIDTH, NA_WIDTH, NA_WIDTH, S5_WIDTH, N_BRANCH * D_MODEL)
D_IN = sum(IN_SPLITS)

kernel_name = 'hybrid_mla_natten_s5_moe_dit'

F32 = jnp.float32


def rms_norm(x, g):
    xf = x.astype(F32)
    y = xf * lax.rsqrt(jnp.mean(xf * xf, axis=-1, keepdims=True) + EPS)
    return (y * g.astype(F32)).astype(x.dtype)


def modulate(h, shift, scale):
    return h * (1.0 + scale) + shift


def axial_rope_tables(n_tokens, dim):
    t = jnp.arange(n_tokens, dtype=jnp.int32)
    row = (t // GRID_W).astype(F32)
    col = (t % GRID_W).astype(F32)
    n_freq = dim // 4
    inv = ROPE_THETA ** (-jnp.arange(n_freq, dtype=F32) / n_freq)
    ang = jnp.concatenate([row[:, None] * inv[None], col[:, None] * inv[None]], axis=-1)
    return jnp.cos(ang), jnp.sin(ang)


def apply_rope(x, cos, sin):
    half = x.shape[-1] // 2
    xf = x.astype(F32)
    x1, x2 = xf[..., :half], xf[..., half:]
    cs, sn = cos[None, :, None, :], sin[None, :, None, :]
    return jnp.concatenate([x1 * cs - x2 * sn, x1 * sn + x2 * cs], axis=-1).astype(x.dtype)


def rope_tail(x, cos, sin):
    return jnp.concatenate([x[..., :MLA_NOPE], apply_rope(x[..., MLA_NOPE:], cos, sin)], axis=-1)


def softmax_attend(q, k, v, scale):
    s = jnp.einsum('bqhd,bkhd->bhqk', q, k).astype(F32) * scale
    w = jax.nn.softmax(s, axis=-1).astype(v.dtype)
    return jnp.einsum('bhqk,bkhd->bqhd', w, v)


def blocked_attend(q, k, v, scale):
    b, n, h, d = q.shape
    nb = n // Q_BLOCK
    qb = jnp.moveaxis(q.reshape(b, nb, Q_BLOCK, h, d), 1, 0)
    out = lax.map(lambda qi: softmax_attend(qi, k, v, scale), qb)
    return jnp.moveaxis(out, 0, 1).reshape(b, n, h, v.shape[-1])


def mla_project(p_q, p_kv, p_kr, g_q, g_kv, w_uq, w_ukv, g_qn, g_kn):
    b, l = p_q.shape[:2]
    q = (rms_norm(p_q, g_q) @ w_uq).reshape(b, l, MLA_HEADS, MLA_QK)
    kv = (rms_norm(p_kv, g_kv) @ w_ukv).reshape(b, l, MLA_HEADS, MLA_NOPE + MLA_V)
    k_nope, v = kv[..., :MLA_NOPE], kv[..., MLA_NOPE:]
    k_rope = jnp.broadcast_to(p_kr[:, :, None, :], (b, l, MLA_HEADS, MLA_ROPE))
    k = jnp.concatenate([k_nope, k_rope], axis=-1)
    return rms_norm(q, g_qn), rms_norm(k, g_kn), v


def neighbourhood_attend(q, k, v, k_ctx, v_ctx, rpb, scale):
    b, n, h, d = q.shape
    n_ctx = k_ctx.shape[1]
    rows = n // GRID_W
    kh = min(NA_KH, rows)
    grid = lambda t: t.reshape(b, rows, GRID_W, h, t.shape[-1])
    kg, vg = grid(k), grid(v)
    qrows = jnp.moveaxis(grid(q), 1, 0)
    cols = np.arange(GRID_W)
    c0 = np.clip(cols - NA_KW // 2, 0, GRID_W - NA_KW)
    cidx = c0[:, None] + np.arange(NA_KW)[None, :]
    dc = cidx - cols[:, None] + (NA_KW - 1)
    bias_c = rpb.astype(F32)[:, :, dc]

    def one_row(args):
        r, qr = args
        r0 = jnp.clip(r - kh // 2, 0, rows - kh)
        kb = lax.dynamic_slice_in_dim(kg, r0, kh, axis=1)
        vb = lax.dynamic_slice_in_dim(vg, r0, kh, axis=1)
        kn = jnp.take(kb, cidx, axis=2)
        vn = jnp.take(vb, cidx, axis=2)
        dr = r0 + jnp.arange(kh) - r + (NA_KH - 1)
        bias = jnp.take(bias_c, dr, axis=1).transpose(0, 2, 1, 3)
        s_win = jnp.einsum('bqhd,brqjhd->bhqrj', qr, kn).astype(F32) * scale + bias[None]
        s_ctx = jnp.einsum('bqhd,bkhd->bhqk', qr, k_ctx).astype(F32) * scale
        s = jnp.concatenate([s_ctx, s_win.reshape(b, h, GRID_W, kh * NA_KW)], axis=-1)
        w = jax.nn.softmax(s, axis=-1).astype(v.dtype)
        w_ctx = w[..., :n_ctx]
        w_win = w[..., n_ctx:].reshape(b, h, GRID_W, kh, NA_KW)
        return (jnp.einsum('bhqk,bkhd->bqhd', w_ctx, v_ctx)
                + jnp.einsum('bhqrj,brqjhd->bqhd', w_win, vn))

    out = lax.map(one_row, (jnp.arange(rows), qrows))
    return jnp.moveaxis(out, 0, 1).reshape(b, n, h, d)


def s5_discretise(a_re, a_im, log_dt, b_re, b_im):
    a = lax.complex(a_re.astype(F32), a_im.astype(F32))
    a_dt = a * jnp.exp(log_dt.astype(F32))[:, None]
    a_bar = jnp.exp(a_dt)
    b_bar = ((a_bar - 1.0) / a)[..., None] * lax.complex(b_re.astype(F32), b_im.astype(F32))
    return a_dt, a_bar, b_bar


def s5_states(u, a_dt, a_bar, b_bar, s0, reverse):
    bu = jnp.einsum('gpc,blgc->blgp', b_bar, u.astype(jnp.complex64))
    a_seq = jnp.broadcast_to(a_bar, bu.shape)

    def combine(e1, e2):
        a1, b1 = e1
        a2, b2 = e2
        return a1 * a2, a2 * b1 + b2

    _, s = lax.associative_scan(combine, (a_seq, bu), reverse=reverse, axis=1)
    if s0 is not None:
        length = u.shape[1]
        steps = (jnp.arange(length, 0, -1) if reverse else jnp.arange(1, length + 1)).astype(F32)
        decay = jnp.exp(a_dt[None] * steps[:, None, None])
        s = s + decay[None] * s0[:, None]
    return s


def s5_readout(s, c_re, c_im):
    cm = lax.complex(c_re.astype(F32), c_im.astype(F32))
    return jnp.real(jnp.einsum('gcp,blgp->blgc', cm, s))


def s5_mixer(u_ctx, u_lat, a_re, a_im, log_dt, b_re, b_im, c_re, c_im, d_skip, w_glu, b_glu, need_ctx):
    b = u_lat.shape[0]
    groups = lambda u: u.astype(F32).reshape(b, u.shape[1], S5_GROUPS, S5_GCH)
    uc, ul = groups(u_ctx), groups(u_lat)
    d_g = d_skip.astype(F32).reshape(S5_GROUPS, S5_GCH)
    y_l = d_g * ul
    y_c = d_g * uc if need_ctx else None
    for dirn in range(2):
        rev = dirn == 1
        a_dt, a_bar, b_bar = s5_discretise(a_re[dirn], a_im[dirn], log_dt[dirn], b_re[dirn], b_im[dirn])
        s_c = s5_states(uc, a_dt, a_bar, b_bar, None, rev)
        s_c_final = s_c[:, 0] if rev else s_c[:, -1]
        s_l = s5_states(ul, a_dt, a_bar, b_bar, s_c_final, rev)
        y_l = y_l + s5_readout(s_l, c_re[dirn], c_im[dirn])
        if need_ctx:
            y_c = y_c + s5_readout(s_c, c_re[dirn], c_im[dirn])

    def glu(y):
        y = jax.nn.gelu(y.reshape(b, y.shape[1], S5_WIDTH))
        return (y * jax.nn.sigmoid(y @ w_glu.astype(F32) + b_glu.astype(F32))).astype(u_lat.dtype)

    return glu(y_l), (glu(y_c) if need_ctx else None)


def branch_merge(o_a, o_b, o_c, gate_logits, w_a, w_b, w_c, w_o):
    g_a, g_b, g_c = jnp.split(jax.nn.sigmoid(gate_logits), N_BRANCH, axis=-1)
    return (g_a * (o_a @ w_a) + g_b * (o_b @ w_b) + g_c * (o_c @ w_c)) @ w_o


def swiglu(h, w_g, w_u, w_d):
    return (jax.nn.silu(h @ w_g) * (h @ w_u)) @ w_d


def moe_swiglu(h, w_router, w_g, w_u, w_d):
    logits = (h @ w_router).astype(F32)
    top_v, top_i = lax.top_k(logits, TOP_K)
    top_w = jax.nn.softmax(top_v, axis=-1)
    comb = jnp.sum(jax.nn.one_hot(top_i, N_EXPERTS, dtype=F32) * top_w[..., None], axis=-2)
    out = jnp.zeros_like(h)
    for e in range(N_EXPERTS):
        out = out + comb[..., e:e + 1].astype(h.dtype) * swiglu(h, w_g[e], w_u[e], w_d[e])
    return out


def setup_inputs(seed: int = 0) -> dict:
    key = jax.random.key(seed)
    ks = iter(jax.random.split(key, 64))
    L = DEPTH
    n_dense = (DEPTH + 1) // 2
    n_moe = DEPTH // 2
    G, P, Cg, D = S5_GROUPS, S5_STATE, S5_GCH, D_MODEL

    def nrm(shape, fan_in):
        return jax.random.normal(next(ks), shape, F32) * (fan_in ** -0.5)

    def small(shape, s):
        return s * jax.random.normal(next(ks), shape, F32)

    def gain(shape):
        return 1.0 + 0.02 * jax.random.normal(next(ks), shape, F32)

    n_idx = jnp.arange(P, dtype=F32)
    return {
        'x': jax.random.normal(next(ks), (BATCH, SEQ, D), F32),
        'c': jax.random.normal(next(ks), (BATCH, D), F32),
        'ctx': jax.random.normal(next(ks), (BATCH, CTX_LEN, D), F32),
        'c_ctx': jax.random.normal(next(ks), (D,), F32),
        'w_mod': nrm((L, D, 6 * D), D),
        'b_mod': small((L, 6 * D), 0.02),
        'g_norm1': gain((L, D)),
        'g_norm2': gain((L, D)),
        'w_in': nrm((L, D, D_IN), D),
        'g_mla_q': gain((L, MLA_Q_RANK)),
        'g_mla_kv': gain((L, MLA_KV_RANK)),
        'w_mla_uq': nrm((L, MLA_Q_RANK, MLA_HEADS * MLA_QK), MLA_Q_RANK),
        'w_mla_ukv': nrm((L, MLA_KV_RANK, MLA_HEADS * (MLA_NOPE + MLA_V)), MLA_KV_RANK),
        'g_mla_qn': gain((L, MLA_QK)),
        'g_mla_kn': gain((L, MLA_QK)),
        'g_na_qn': gain((L, NA_DIM)),
        'g_na_kn': gain((L, NA_DIM)),
        'na_rpb': small((L, NA_HEADS, 2 * NA_KH - 1, 2 * NA_KW - 1), 0.1),
        's5_a_re': -0.5 + small((L, 2, G, P), 0.01),
        's5_a_im': math.pi * n_idx + small((L, 2, G, P), 0.01),
        's5_log_dt': jax.random.uniform(next(ks), (L, 2, G), F32, math.log(1e-3), math.log(1e-1)),
        's5_b_re': small((L, 2, G, P, Cg), (2.0 * Cg) ** -0.5),
        's5_b_im': small((L, 2, G, P, Cg), (2.0 * Cg) ** -0.5),
        's5_c_re': small((L, 2, G, Cg, P), (2.0 * P) ** -0.5),
        's5_c_im': small((L, 2, G, Cg, P), (2.0 * P) ** -0.5),
        's5_d': jax.random.normal(next(ks), (L, S5_WIDTH), F32),
        'w_glu': nrm((L, S5_WIDTH, S5_WIDTH), S5_WIDTH),
        'b_glu': small((L, S5_WIDTH), 0.02),
        'w_br_mla': nrm((L, MLA_WIDTH, D), MLA_WIDTH),
        'w_br_na': nrm((L, NA_WIDTH, D), NA_WIDTH),
        'w_br_s5': nrm((L, S5_WIDTH, D), S5_WIDTH),
        'w_out': nrm((L, D, D), D),
        'w_ffn_gate': nrm((n_dense, D, D_FF), D),
        'w_ffn_up': nrm((n_dense, D, D_FF), D),
        'w_ffn_down': nrm((n_dense, D_FF, D), D_FF),
        'w_router': nrm((n_moe, D, N_EXPERTS), D),
        'w_exp_gate': nrm((n_moe, N_EXPERTS, D, D_FF), D),
        'w_exp_up': nrm((n_moe, N_EXPERTS, D, D_FF), D),
        'w_exp_down': nrm((n_moe, N_EXPERTS, D_FF, D), D_FF),
    }


def reference(x, c, ctx, c_ctx, w_mod, b_mod, g_norm1, g_norm2, w_in, g_mla_q, g_mla_kv, w_mla_uq, w_mla_ukv,
              g_mla_qn, g_mla_kn, g_na_qn, g_na_kn, na_rpb, s5_a_re, s5_a_im, s5_log_dt, s5_b_re, s5_b_im,
              s5_c_re, s5_c_im, s5_d, w_glu, b_glu, w_br_mla, w_br_na, w_br_s5, w_out, w_ffn_gate, w_ffn_up,
              w_ffn_down, w_router, w_exp_gate, w_exp_up, w_exp_down):
    b, n, _ = x.shape
    n_ctx = ctx.shape[1]
    cos, sin = axial_rope_tables(n, MLA_ROPE)
    split_idx = np.cumsum(IN_SPLITS)[:-1].tolist()
    mla_scale = MLA_QK ** -0.5
    na_scale = NA_DIM ** -0.5
    xl, xc = x, ctx
    for i in range(DEPTH):
        need_ctx = i < DEPTH - 1
        mod_l = (jax.nn.silu(c) @ w_mod[i] + b_mod[i])[:, None, :]
        mod_c = jax.nn.silu(c_ctx) @ w_mod[i] + b_mod[i]
        sh1_l, sc1_l, ga1_l, sh2_l, sc2_l, ga2_l = jnp.split(mod_l, 6, axis=-1)
        sh1_c, sc1_c, ga1_c, sh2_c, sc2_c, ga2_c = jnp.split(mod_c, 6, axis=-1)

        h = jnp.concatenate([modulate(rms_norm(xc, g_norm1[i]), sh1_c, sc1_c),
                             modulate(rms_norm(xl, g_norm1[i]), sh1_l, sc1_l)], axis=1)
        p_q, p_kv, p_kr, p_nq, p_nk, p_nv, p_u, p_gate = jnp.split(h @ w_in[i], split_idx, axis=-1)

        q, k, v = mla_project(p_q, p_kv, p_kr, g_mla_q[i], g_mla_kv[i], w_mla_uq[i], w_mla_ukv[i],
                              g_mla_qn[i], g_mla_kn[i])
        q_lat = rope_tail(q[:, n_ctx:], cos, sin)
        k_all = jnp.concatenate([k[:, :n_ctx], rope_tail(k[:, n_ctx:], cos, sin)], axis=1)
        o_a_l = blocked_attend(q_lat, k_all, v, mla_scale).reshape(b, n, MLA_WIDTH)

        heads = lambda t: t.reshape(b, n_ctx + n, NA_HEADS, NA_DIM)
        nq = rms_norm(heads(p_nq), g_na_qn[i])
        nk = rms_norm(heads(p_nk), g_na_kn[i])
        nv = heads(p_nv)
        o_b_l = neighbourhood_attend(nq[:, n_ctx:], nk[:, n_ctx:], nv[:, n_ctx:], nk[:, :n_ctx], nv[:, :n_ctx],
                                     na_rpb[i], na_scale).reshape(b, n, NA_WIDTH)

        o_c_l, o_c_c = s5_mixer(p_u[:, :n_ctx], p_u[:, n_ctx:], s5_a_re[i], s5_a_im[i], s5_log_dt[i],
                                s5_b_re[i], s5_b_im[i], s5_c_re[i], s5_c_im[i], s5_d[i], w_glu[i], b_glu[i],
                                need_ctx)

        mix_l = branch_merge(o_a_l, o_b_l, o_c_l, p_gate[:, n_ctx:], w_br_mla[i], w_br_na[i], w_br_s5[i], w_out[i])
        xl_new = xl + ga1_l * mix_l
        if need_ctx:
            o_a_c = softmax_attend(q[:, :n_ctx], k[:, :n_ctx], v[:, :n_ctx], mla_scale).reshape(b, n_ctx, MLA_WIDTH)
            o_b_c = softmax_attend(nq[:, :n_ctx], nk[:, :n_ctx], nv[:, :n_ctx], na_scale).reshape(b, n_ctx, NA_WIDTH)
            mix_c = branch_merge(o_a_c, o_b_c, o_c_c, p_gate[:, :n_ctx], w_br_mla[i], w_br_na[i], w_br_s5[i],
                                 w_out[i])
            xc = xc + ga1_c * mix_c
        xl = xl_new

        h2 = modulate(rms_norm(xl, g_norm2[i]), sh2_l, sc2_l)
        if need_ctx:
            h2 = jnp.concatenate([modulate(rms_norm(xc, g_norm2[i]), sh2_c, sc2_c), h2], axis=1)
        j = i // 2
        if i % 2 == 0:
            f = swiglu(h2, w_ffn_gate[j], w_ffn_up[j], w_ffn_down[j])
        else:
            f = moe_swiglu(h2, w_router[j], w_exp_gate[j], w_exp_up[j], w_exp_down[j])
        if need_ctx:
            xc = xc + ga2_c * f[:, :n_ctx]
        xl = xl + ga2_l * f[:, f.shape[1] - n:]
    return xl
```

```python
import functools
import math

import jax
import jax.numpy as jnp
import numpy as np
from jax import lax
from jax.experimental import pallas as pl
from jax.experimental.pallas import tpu as pltpu

F32 = jnp.float32
BF16 = jnp.bfloat16

D_MODEL = 1024
DEPTH = 4
GRID_W = 64
EPS = 1e-6
ROPE_THETA = 10000.0

MLA_HEADS = 8
MLA_NOPE = 64
MLA_ROPE = 32
MLA_V = 64
MLA_Q_RANK = 384
MLA_KV_RANK = 256
MLA_QK = MLA_NOPE + MLA_ROPE

NA_HEADS = 8
NA_DIM = 64
NA_KH = 8
NA_KW = 16
NA_WIDTH = NA_HEADS * NA_DIM

S5_GROUPS = 32
S5_GCH = 16
S5_STATE = 64
S5_WIDTH = S5_GROUPS * S5_GCH

N_BRANCH = 3
D_FF = 2816
N_EXPERTS = 8
TOP_K = 2

IN_SPLITS = (MLA_Q_RANK, MLA_KV_RANK, MLA_ROPE, NA_WIDTH, NA_WIDTH, NA_WIDTH, S5_WIDTH, N_BRANCH * D_MODEL)

LANES = 128
HEAD_PAD = MLA_HEADS * LANES
S5_T = 16
NA_QROWS = 4
NA_WROWS = NA_QROWS + NA_KH - 1
VMEM_LIMIT = 56 * 1024 * 1024
NEG = -1e30


def _pick(n, candidates):
    for c in candidates:
        if n % c == 0:
            return c
    raise ValueError(f"no tile in {candidates} divides {n}")


def _const_spec(shape):
    nd = len(shape)
    return pl.BlockSpec(shape, lambda *_: (0,) * nd, pipeline_mode=pl.Buffered(1))


def _params(sem):
    return pltpu.CompilerParams(dimension_semantics=sem, vmem_limit_bytes=VMEM_LIMIT)


def _dot(a, b):
    return jnp.dot(a, b, preferred_element_type=F32)


def _dot_t(a, b):
    return lax.dot_general(a, b, (((1,), (1,)), ((), ())), preferred_element_type=F32)


def _split(x):
    hi = x.astype(BF16)
    lo = (x - hi.astype(F32)).astype(BF16)
    return hi, lo


def _dot3(a, b):
    a_hi, a_lo = _split(a)
    b_hi, b_lo = _split(b)
    return _dot(a_hi, b_hi) + _dot(a_hi, b_lo) + _dot(a_lo, b_hi)


def _dot3_t(a, b):
    a_hi, a_lo = _split(a)
    b_hi, b_lo = _split(b)
    return _dot_t(a_hi, b_hi) + _dot_t(a_hi, b_lo) + _dot_t(a_lo, b_hi)


def _rms(x, g):
    ms = jnp.mean(x * x, axis=-1, keepdims=True)
    return x * lax.rsqrt(ms + EPS) * g


def _row_select(tile, tm, n_lat, mod_ref, idx):
    rows = tile * tm + lax.broadcasted_iota(jnp.int32, (tm, 1), 0)
    lat = mod_ref[0, :, idx * D_MODEL:(idx + 1) * D_MODEL]
    ctx = mod_ref[1, :, idx * D_MODEL:(idx + 1) * D_MODEL]
    return jnp.where(rows < n_lat, lat, ctx)


def _mod_kernel(c_ref, w_ref, b_ref, o_ref):
    c = c_ref[...]
    a = c * jax.nn.sigmoid(c)
    o_ref[...] = _dot3(a, w_ref[...]) + b_ref[...]


def _mod_call(cvec, w_mod, b_mod):
    depth, d, n6 = w_mod.shape
    tn = _pick(n6, (1536, 1024, 512, 128))
    return pl.pallas_call(
        _mod_kernel,
        grid=(depth, n6 // tn),
        in_specs=[
            pl.BlockSpec((8, d), lambda l, j: (0, 0)),
            pl.BlockSpec((None, d, tn), lambda l, j: (l, 0, j)),
            pl.BlockSpec((None, 1, tn), lambda l, j: (l, 0, j)),
        ],
        out_specs=pl.BlockSpec((None, 8, tn), lambda l, j: (l, 0, j)),
        out_shape=jax.ShapeDtypeStruct((depth, 8, n6), F32),
        compiler_params=_params(("arbitrary", "arbitrary")),
        name="mod",
    )(cvec, w_mod, b_mod.reshape(depth, 1, n6))


def _head_norm_rope(xp, gain, c_t, s1_t, s2_t, scale, o_ref):
    for h in range(MLA_HEADS):
        sl = slice(h * LANES, (h + 1) * LANES)
        xh = xp[:, sl]
        ms = jnp.sum(xh * xh, axis=-1, keepdims=True) * (1.0 / MLA_QK)
        y = xh * lax.rsqrt(ms + EPS) * gain[:, sl]
        y = y * c_t + pltpu.roll(y, LANES - MLA_ROPE // 2, 1) * s1_t + pltpu.roll(y, MLA_ROPE // 2, 1) * s2_t
        if scale != 1.0:
            y = y * scale
        o_ref[:, sl] = y.astype(o_ref.dtype)


def _pre_kernel(n_lat, tm, mla_scale, na_scale,
                x_ref, mod_ref, g1_ref, wq_ref, wkv_ref, wkr_ref, wnq_ref, wnk_ref, wnv_ref, wu_ref, wg_ref,
                gq_ref, gkv_ref, wuq_ref, wk_ref, wv_ref, vone_ref, gqn_ref, gkn_ref,
                rc_ref, rs1_ref, rs2_ref, e64_ref, gnq_ref, gnk_ref,
                q_ref, k_ref, v_ref, nq_ref, nk_ref, nv_ref, u_ref, gate_ref):
    i = pl.program_id(0)
    sh = _row_select(i, tm, n_lat, mod_ref, 0)
    sc = _row_select(i, tm, n_lat, mod_ref, 1)
    h = (_rms(x_ref[...], g1_ref[...]) * (1.0 + sc) + sh).astype(BF16)

    c_t, s1_t, s2_t = rc_ref[...], rs1_ref[...], rs2_ref[...]

    rq = _rms(_dot(h, wq_ref[...]), gq_ref[...]).astype(BF16)
    _head_norm_rope(_dot(rq, wuq_ref[...]), gqn_ref[...], c_t, s1_t, s2_t, mla_scale, q_ref)
    rkv = _rms(_dot(h, wkv_ref[...]), gkv_ref[...]).astype(BF16)
    pkr = _dot(h, wkr_ref[...]).astype(BF16)
    kin = jnp.concatenate([rkv, pkr], axis=-1)
    _head_norm_rope(_dot(kin, wk_ref[...]), gkn_ref[...], c_t, s1_t, s2_t, 1.0, k_ref)
    v_ref[...] = (_dot(rkv, wv_ref[...]) + vone_ref[...]).astype(BF16)

    def na_norm(w_ref, g_ref, scale):
        p = _dot(h, w_ref[...])
        sq_hi, sq_lo = _split(p * p)
        ss = (_dot(sq_hi, e64_ref[...]) + _dot(sq_lo, e64_ref[...])) * (1.0 / NA_DIM)
        return (p * lax.rsqrt(ss + EPS) * (g_ref[...] * scale)).astype(BF16)

    nq_ref[...] = na_norm(wnq_ref, gnq_ref, na_scale)
    nk_ref[...] = na_norm(wnk_ref, gnk_ref, 1.0)
    nv_ref[...] = _dot(h, wnv_ref[...]).astype(BF16)

    u_ref[...] = _dot(h, wu_ref[...])
    gate_ref[...] = jax.nn.sigmoid(_dot(h, wg_ref[...])).astype(BF16)


def _pre_call(x, mods, n_lat, lw, rope):
    L, d = x.shape
    tm = _pick(L, (320, 256, 128))
    row = lambda w: pl.BlockSpec((tm, w), lambda i: (i, 0))
    weights = [lw[k] for k in ("g1", "wq", "wkv", "wkr", "wnq", "wnk", "wnv", "wu", "wg", "gq", "gkv", "wuq",
                               "wk", "wv", "vone", "gqn", "gkn")]
    tail = [lw["e64"], lw["gnq"], lw["gnk"]]
    in_specs = ([row(d), _const_spec(mods.shape)] + [_const_spec(w.shape) for w in weights]
                + [row(LANES)] * 3 + [_const_spec(w.shape) for w in tail])
    outs = [(HEAD_PAD, BF16)] * 3 + [(NA_WIDTH, BF16)] * 3 + [(S5_WIDTH, F32), (N_BRANCH * d, BF16)]
    return pl.pallas_call(
        functools.partial(_pre_kernel, n_lat, tm, MLA_QK ** -0.5, NA_DIM ** -0.5),
        grid=(L // tm,),
        in_specs=in_specs,
        out_specs=[row(w) for w, _ in outs],
        out_shape=[jax.ShapeDtypeStruct((L, w), t) for w, t in outs],
        compiler_params=_params(("arbitrary",)),
        name="pre",
    )(x, mods, *weights, *rope, *tail)


def _mla_kernel(tq, tk, n_chunks, q_ref, k_ref, v_ref, o_ref, m_ref, acc_ref):
    m_ref[...] = jnp.full_like(m_ref, NEG)
    acc_ref[...] = jnp.zeros_like(acc_ref)
    q = q_ref[...]

    def body(c, carry):
        off = pl.multiple_of(c * tk, tk)
        s = _dot_t(q, k_ref[pl.ds(off, tk), :])
        m_prev = m_ref[...]
        m_new = jnp.maximum(m_prev, jnp.max(s, axis=-1, keepdims=True))
        p = jnp.exp(s - pltpu.repeat(m_new, tk // LANES, axis=1))
        acc_ref[...] = jnp.exp(m_prev - m_new) * acc_ref[...] + _dot(p.astype(BF16), v_ref[pl.ds(off, tk), :])
        m_ref[...] = m_new
        return carry

    lax.fori_loop(0, n_chunks, body, 0)
    acc = acc_ref[...]
    o_ref[...] = (acc * pl.reciprocal(acc[:, MLA_V:MLA_V + 1], approx=False)).astype(o_ref.dtype)


def _mla_call(q, k, v, n_lat):
    L = q.shape[0]
    tq = _pick(n_lat, (512, 256))
    tk = _pick(L, (640, 768, 512, 256))
    kv_spec = pl.BlockSpec((L, LANES), lambda h, i: (0, h))
    return pl.pallas_call(
        functools.partial(_mla_kernel, tq, tk, L // tk),
        grid=(MLA_HEADS, n_lat // tq),
        in_specs=[pl.BlockSpec((tq, LANES), lambda h, i: (i, h)), kv_spec, kv_spec],
        out_specs=pl.BlockSpec((tq, LANES), lambda h, i: (i, h)),
        out_shape=jax.ShapeDtypeStruct((L, HEAD_PAD), BF16),
        scratch_shapes=[pltpu.VMEM((tq, LANES), F32), pltpu.VMEM((tq, LANES), F32)],
        compiler_params=_params(("arbitrary", "arbitrary")),
        name="mla",
    )(q, k, v)


def _softmax_pv(parts):
    m = parts[0][0].max(axis=-1, keepdims=True)
    for s, _ in parts[1:]:
        m = jnp.maximum(m, s.max(axis=-1, keepdims=True))
    num, den = 0.0, 0.0
    for s, v in parts:
        p = jnp.exp(s - m)
        den = den + p.sum(axis=-1, keepdims=True)
        num = num + _dot(p.astype(BF16), v)
    return num * pl.reciprocal(den, approx=False)


def _na_kernel(n_lat, n_ctx, n_blocks, q_ref, k_ref, v_ref, bias_ref, o_ref):
    b = pl.program_id(1)
    wtok = NA_WROWS * GRID_W
    row0 = jnp.clip(b * NA_QROWS - NA_KH // 2, 0, n_lat // GRID_W - NA_WROWS)
    off = pl.multiple_of(row0 * GRID_W, GRID_W)
    q = q_ref[...]
    kw, vw = k_ref[pl.ds(off, wtok), :], v_ref[pl.ds(off, wtok), :]
    kc, vc = k_ref[n_lat:n_lat + n_ctx, :], v_ref[n_lat:n_lat + n_ctx, :]
    outs = []
    for j in range(LANES // NA_DIM):
        sl = slice(j * NA_DIM, (j + 1) * NA_DIM)
        qh = q[:, sl]
        s_win = _dot_t(qh, kw[:, sl]) + bias_ref[j]
        s_ctx = _dot_t(qh, kc[:, sl])
        outs.append(_softmax_pv([(s_ctx, vc[:, sl]), (s_win, vw[:, sl])]))
    o_ref[...] = jnp.concatenate(outs, axis=-1).astype(o_ref.dtype)


def _na_bias(rpb, n_rows):
    n_blocks = n_rows // NA_QROWS
    out = []
    for b in (0, 1, n_blocks - 1):
        row0 = min(max(b * NA_QROWS - NA_KH // 2, 0), n_rows - NA_WROWS)
        qr = b * NA_QROWS + np.arange(NA_QROWS)
        kr = row0 + np.arange(NA_WROWS)
        r0 = np.clip(qr - NA_KH // 2, 0, n_rows - NA_KH)
        row_ok = (kr[None, :] >= r0[:, None]) & (kr[None, :] < r0[:, None] + NA_KH)
        dr = np.clip(kr[None, :] - qr[:, None] + NA_KH - 1, 0, 2 * NA_KH - 2)
        cols = np.arange(GRID_W)
        c0 = np.clip(cols - NA_KW // 2, 0, GRID_W - NA_KW)
        col_ok = (cols[None, :] >= c0[:, None]) & (cols[None, :] < c0[:, None] + NA_KW)
        dc = np.clip(cols[None, :] - cols[:, None] + NA_KW - 1, 0, 2 * NA_KW - 2)
        ok = row_ok[:, None, :, None] & col_ok[None, :, None, :]
        vals = rpb.astype(F32)[:, dr[:, None, :, None], dc[None, :, None, :]]
        vals = jnp.where(ok[None], vals, NEG)
        out.append(vals.reshape(rpb.shape[0], NA_QROWS * GRID_W, NA_WROWS * GRID_W))
    return jnp.stack(out)


def _na_call(nq, nk, nv, bias, n_lat, n_ctx):
    L = nq.shape[0]
    tq = NA_QROWS * GRID_W
    n_blocks = n_lat // tq
    hp = LANES // NA_DIM
    kv_spec = pl.BlockSpec((L, LANES), lambda h, b: (0, h))

    def bias_map(h, b):
        return (jnp.where(b == 0, 0, jnp.where(b == n_blocks - 1, 2, 1)), h, 0, 0)

    return pl.pallas_call(
        functools.partial(_na_kernel, n_lat, n_ctx, n_blocks),
        grid=(NA_HEADS // hp, n_blocks),
        in_specs=[pl.BlockSpec((tq, LANES), lambda h, b: (b, h)), kv_spec, kv_spec,
                  pl.BlockSpec((None, hp, tq, NA_WROWS * GRID_W), bias_map)],
        out_specs=pl.BlockSpec((tq, LANES), lambda h, b: (b, h)),
        out_shape=jax.ShapeDtypeStruct((L, NA_WIDTH), BF16),
        compiler_params=_params(("arbitrary", "arbitrary")),
        name="na",
    )(nq, nk, nv, bias)


def _ctx_kernel(q_ref, k_ref, v_ref, nq_ref, nk_ref, nv_ref, oa_ref, ob_ref):
    for h in range(MLA_HEADS):
        sl = slice(h * LANES, (h + 1) * LANES)
        s = _dot_t(q_ref[:, sl], k_ref[:, sl])
        p = jnp.exp(s - s.max(axis=-1, keepdims=True))
        acc = _dot(p.astype(BF16), v_ref[:, sl])
        oa_ref[:, sl] = (acc * pl.reciprocal(acc[:, MLA_V:MLA_V + 1], approx=False)).astype(oa_ref.dtype)
    nq, nk, nv = nq_ref[...], nk_ref[...], nv_ref[...]
    outs = []
    for h in range(NA_HEADS):
        sl = slice(h * NA_DIM, (h + 1) * NA_DIM)
        outs.append(_softmax_pv([(_dot_t(nq[:, sl], nk[:, sl]), nv[:, sl])]))
    ob_ref[...] = jnp.concatenate(outs, axis=-1).astype(ob_ref.dtype)


def _ctx_call(q, k, v, nq, nk, nv, n_lat, n_ctx):
    blk = n_lat // n_ctx
    a_spec = pl.BlockSpec((n_ctx, HEAD_PAD), lambda i: (blk, 0))
    b_spec = pl.BlockSpec((n_ctx, NA_WIDTH), lambda i: (blk, 0))
    return pl.pallas_call(
        _ctx_kernel,
        grid=(1,),
        in_specs=[a_spec] * 3 + [b_spec] * 3,
        out_specs=[pl.BlockSpec((n_ctx, HEAD_PAD), lambda i: (0, 0)),
                   pl.BlockSpec((n_ctx, NA_WIDTH), lambda i: (0, 0))],
        out_shape=[jax.ShapeDtypeStruct((n_ctx, HEAD_PAD), BF16), jax.ShapeDtypeStruct((n_ctx, NA_WIDTH), BF16)],
        compiler_params=_params(("arbitrary",)),
        name="ctx_attn",
    )(q, k, v, nq, nk, nv)


def _s5prep_kernel(are_ref, aim_ref, ldt_ref, bre_ref, bim_ref, cre_ref, cim_ref,
                   kall_ref, wr_ref, wi_ref, br_ref, bi_ref, atr_ref, ati_ref):
    a_re, a_im = are_ref[...], aim_ref[...]
    dt = jnp.exp(ldt_ref[...])
    adt_re, adt_im = a_re * dt, a_im * dt

    def power(steps):
        mag = jnp.exp(adt_re * steps)
        return mag * jnp.cos(adt_im * steps), mag * jnp.sin(adt_im * steps)

    steps = lax.broadcasted_iota(jnp.int32, (S5_T, S5_GCH, S5_STATE), 0).astype(F32)
    p0_re, p0_im = power(steps)
    p1_re, p1_im = power(steps + 1.0)
    ab_re, ab_im = power(1.0)
    nr, ni = ab_re - 1.0, ab_im
    den = 1.0 / (a_re * a_re + a_im * a_im)
    f_re, f_im = (nr * a_re + ni * a_im) * den, (ni * a_re - nr * a_im) * den
    b_re, b_im = bre_ref[...], bim_ref[...]
    bb_re, bb_im = f_re * b_re - f_im * b_im, f_re * b_im + f_im * b_re
    c_re, c_im = cre_ref[...], cim_ref[...]
    n = S5_T * S5_GCH
    flat = lambda a: a.reshape(n, S5_STATE)
    w0_re, w0_im = flat(c_re * p0_re - c_im * p0_im), flat(c_re * p0_im + c_im * p0_re)
    kall_ref[...] = _dot3_t(w0_re, bb_re) - _dot3_t(w0_im, bb_im)
    wr_ref[...] = flat(c_re * p1_re - c_im * p1_im)
    wi_ref[...] = flat(c_re * p1_im + c_im * p1_re)
    br_ref[...] = flat(p0_re * bb_re - p0_im * bb_im)
    bi_ref[...] = flat(p0_re * bb_im + p0_im * bb_re)
    atr_ref[...], ati_ref[...] = power(float(S5_T))


def _s5prep_call(a_re, a_im, log_dt, b_re, b_im, c_re, c_im):
    nd, g, p = a_re.shape
    cg = b_re.shape[-1]
    n = S5_T * cg
    vec = pl.BlockSpec((None, None, 1, p), lambda d, j: (d, j, 0, 0))
    mat = pl.BlockSpec((None, None, cg, p), lambda d, j: (d, j, 0, 0))
    big = pl.BlockSpec((None, None, n, p), lambda d, j: (d, j, 0, 0))
    sds = lambda *s: jax.ShapeDtypeStruct((nd, g) + s, F32)
    tr = lambda t: jnp.swapaxes(t, -1, -2)
    return pl.pallas_call(
        _s5prep_kernel,
        grid=(nd, g),
        in_specs=[vec, vec, vec, mat, mat, mat, mat],
        out_specs=[pl.BlockSpec((None, None, n, cg), lambda d, j: (d, j, 0, 0)), big, big, big, big, vec, vec],
        out_shape=[sds(n, cg), sds(n, p), sds(n, p), sds(n, p), sds(n, p), sds(1, p), sds(1, p)],
        compiler_params=_params(("arbitrary", "arbitrary")),
        name="s5_prep",
    )(a_re[:, :, None, :], a_im[:, :, None, :], jnp.broadcast_to(log_dt[:, :, None, None], (nd, g, 1, p)),
      tr(b_re), tr(b_im), c_re, c_im)


def _pair_blockdiag(m):
    nd, g, r, c = m.shape
    m = m.reshape(nd, g // 2, 2, r, c)
    z = jnp.zeros_like(m[:, :, 0])
    top = jnp.concatenate([m[:, :, 0], z], axis=-1)
    bot = jnp.concatenate([z, m[:, :, 1]], axis=-1)
    return jnp.concatenate([top, bot], axis=-2)


def _s5_operands(kall, wr, wi, br, bi, atr, ati):
    nd, g = kall.shape[:2]
    t = np.arange(S5_T)
    k5 = kall.reshape(nd, g, S5_T, S5_GCH, S5_GCH)
    mts = []
    for d, delta in enumerate((t[:, None] - t[None, :], t[None, :] - t[:, None])):
        blk = k5[d][:, np.clip(delta, 0, S5_T - 1)]
        blk = jnp.where((delta >= 0)[None, :, :, None, None], blk, 0.0)
        mts.append(blk.transpose(0, 1, 3, 2, 4).reshape(g, S5_T * S5_GCH, S5_T * S5_GCH))
    mt = jnp.stack(mts)

    def flip(m, which):
        m5 = m.reshape(nd, g, S5_T, S5_GCH, -1)
        return jnp.stack([m5[d][:, ::-1] if d == which else m5[d] for d in range(nd)]).reshape(m.shape)

    wr, wi = flip(wr, 1), flip(-wi, 1)
    br, bi = flip(br, 0), flip(bi, 0)
    pair = lambda m: _pair_blockdiag(m).astype(BF16)
    a_t = lambda a: a.reshape(nd, g * S5_STATE)
    return pair(mt), pair(wr), pair(wi), pair(br), pair(bi), a_t(atr), a_t(ati)


def _s5a_kernel(u_ref, br_ref, bi_ref, ore_ref, oim_ref):
    u = u_ref[...]
    ore_ref[...] = _dot(u, br_ref[...])
    oim_ref[...] = _dot(u, bi_ref[...])


def _s5a_call(u_pair, brp, bip):
    gp, nch, w = u_pair.shape
    nd = brp.shape[0]
    ws = brp.shape[-1]
    wspec = pl.BlockSpec((None, None, w, ws), lambda d, j: (d, j, 0, 0))
    ospec = pl.BlockSpec((None, nch, ws), lambda d, j: (d, 0, j))
    return pl.pallas_call(
        _s5a_kernel,
        grid=(nd, gp),
        in_specs=[pl.BlockSpec((None, nch, w), lambda d, j: (j, 0, 0)), wspec, wspec],
        out_specs=[ospec, ospec],
        out_shape=[jax.ShapeDtypeStruct((nd, nch, gp * ws), F32)] * 2,
        compiler_params=_params(("arbitrary", "arbitrary")),
        name="s5_chunk_inputs",
    )(u_pair, brp, bip)


def _s5scan_kernel(n_lat_ch, n_ctx_ch, bre_ref, bim_ref, atr_ref, ati_ref, sre_ref, sim_ref):
    d = pl.program_id(0)
    a_re, a_im = atr_ref[...], ati_ref[...]

    def segment(base, count, carry):
        def step(i, st):
            s_re, s_im = st
            k = base + jnp.where(d == 0, i, count - 1 - i)
            sre_ref[k] = s_re
            sim_ref[k] = s_im
            return (a_re * s_re - a_im * s_im + bre_ref[k], a_re * s_im + a_im * s_re + bim_ref[k])
        return lax.fori_loop(0, count, step, carry)

    zero = jnp.zeros(a_re.shape, F32)
    carry = segment(n_lat_ch, n_ctx_ch, (zero, zero))
    segment(0, n_lat_ch, carry)


def _s5scan_call(b_re, b_im, atr, ati, n_lat_ch, n_ctx_ch):
    nd, nch, w = b_re.shape
    shp = (nd, nch, w // LANES, LANES)
    spec = pl.BlockSpec((None, nch, 8, LANES), lambda d, j: (d, 0, j, 0))
    aspec = pl.BlockSpec((None, 8, LANES), lambda d, j: (d, j, 0))
    return pl.pallas_call(
        functools.partial(_s5scan_kernel, n_lat_ch, n_ctx_ch),
        grid=(nd, w // LANES // 8),
        in_specs=[spec, spec, aspec, aspec],
        out_specs=[spec, spec],
        out_shape=[jax.ShapeDtypeStruct(shp, F32)] * 2,
        compiler_params=_params(("arbitrary", "arbitrary")),
        name="s5_scan",
    )(b_re.reshape(shp), b_im.reshape(shp), atr.reshape(nd, w // LANES, LANES), ati.reshape(nd, w // LANES, LANES))


def _s5c_kernel(u_ref, sre_ref, sim_ref, mt_ref, wr_ref, wi_ref, y_ref):
    d = pl.program_id(1)
    y = (_dot_t(u_ref[...], mt_ref[...]) + _dot_t(sre_ref[...].astype(BF16), wr_ref[...])
         + _dot_t(sim_ref[...].astype(BF16), wi_ref[...]))

    @pl.when(d == 0)
    def _():
        y_ref[...] = y

    @pl.when(d != 0)
    def _():
        y_ref[...] += y


def _s5c_call(u_pair, s_re, s_im, mtp, wrp, wip):
    gp, nch, w = u_pair.shape
    nd = mtp.shape[0]
    ws = wrp.shape[-1]
    sspec = pl.BlockSpec((None, nch, ws), lambda j, d: (d, 0, j))
    wspec = pl.BlockSpec((None, None, w, ws), lambda j, d: (d, j, 0, 0))
    return pl.pallas_call(
        _s5c_kernel,
        grid=(gp, nd),
        in_specs=[pl.BlockSpec((None, nch, w), lambda j, d: (j, 0, 0)), sspec, sspec,
                  pl.BlockSpec((None, None, w, w), lambda j, d: (d, j, 0, 0)), wspec, wspec],
        out_specs=pl.BlockSpec((None, nch, w), lambda j, d: (j, 0, 0)),
        out_shape=jax.ShapeDtypeStruct((gp, nch, w), F32),
        compiler_params=_params(("arbitrary", "arbitrary")),
        name="s5_outputs",
    )(u_pair, s_re, s_im, mtp, wrp, wip)


def _s5_mix(u, ops, n_lat, n_ctx):
    mtp, wrp, wip, brp, bip, atr, ati = ops
    L = u.shape[0]
    nch, gp = L // S5_T, S5_GROUPS // 2
    u_pair = (u.astype(BF16).reshape(nch, S5_T, gp, 2, S5_GCH).transpose(2, 0, 3, 1, 4)
              .reshape(gp, nch, 2 * S5_T * S5_GCH))
    b_re, b_im = _s5a_call(u_pair, brp, bip)
    s_re, s_im = _s5scan_call(b_re, b_im, atr, ati, n_lat // S5_T, n_ctx // S5_T)
    s_re, s_im = s_re.reshape(b_re.shape), s_im.reshape(b_im.shape)
    y_pair = _s5c_call(u_pair, s_re, s_im, mtp, wrp, wip)
    return y_pair.reshape(gp, nch, 2, S5_T, S5_GCH).transpose(1, 3, 0, 2, 4).reshape(L, S5_WIDTH)


def _gelu_tanh(x):
    return 0.5 * x * (1.0 + jnp.tanh(math.sqrt(2.0 / math.pi) * (x + 0.044715 * (x * x * x))))


def _post_kernel(n_lat, tm, moe,
                 x_ref, mod_ref, oa_ref, ob_ref, ys_ref, u_ref, gate_ref,
                 wa_ref, wb_ref, wc_ref, wglu_ref, bglu_ref, d_ref, wo_ref, g2_ref, *rest):
    i = pl.program_id(0)
    d = x_ref.shape[-1]
    g = _gelu_tanh(d_ref[...] * u_ref[...] + ys_ref[...])
    oc = (g * jax.nn.sigmoid(_dot(g.astype(BF16), wglu_ref[...]) + bglu_ref[...])).astype(BF16)
    gate = gate_ref[...].astype(F32)
    mix = (gate[:, :d] * _dot(oa_ref[...], wa_ref[...]) + gate[:, d:2 * d] * _dot(ob_ref[...], wb_ref[...])
           + gate[:, 2 * d:] * _dot(oc, wc_ref[...]))
    x_new = x_ref[...] + _row_select(i, tm, n_lat, mod_ref, 2) * _dot(mix.astype(BF16), wo_ref[...])
    h2 = (_rms(x_new, g2_ref[...]) * (1.0 + _row_select(i, tm, n_lat, mod_ref, 4))
          + _row_select(i, tm, n_lat, mod_ref, 3))
    if not moe:
        xo_ref, h2_ref = rest
    else:
        wr_ref, xo_ref, h2_ref, comb_ref = rest
        logits = _dot3(h2, wr_ref[...])
        lane = lax.broadcasted_iota(jnp.int32, logits.shape, 1)
        logits = jnp.where(lane < N_EXPERTS, logits, NEG)
        m1 = logits.max(axis=-1, keepdims=True)
        i1 = jnp.where(logits == m1, lane, LANES).min(axis=-1, keepdims=True)
        rest_l = jnp.where(lane == i1, NEG, logits)
        m2 = rest_l.max(axis=-1, keepdims=True)
        i2 = jnp.where(rest_l == m2, lane, LANES).min(axis=-1, keepdims=True)
        e2 = jnp.exp(m2 - m1)
        w1 = 1.0 / (1.0 + e2)
        comb_ref[...] = jnp.where(lane == i1, w1, 0.0) + jnp.where(lane == i2, e2 * w1, 0.0)
    xo_ref[...] = x_new
    h2_ref[...] = h2.astype(BF16)


def _post_call(x, mods, o_a, o_b, ys, u, gate, lw, n_lat, moe):
    L, d = x.shape
    tm = _pick(L, (320, 256, 128))
    row = lambda w: pl.BlockSpec((tm, w), lambda i: (i, 0))
    weights = [lw[k] for k in ("wa", "wb", "wc", "wglu", "bglu", "d", "wo", "g2")]
    if moe:
        weights.append(lw["wrouter"])
    outs = [(d, F32), (d, BF16)] + ([(LANES, F32)] if moe else [])
    return pl.pallas_call(
        functools.partial(_post_kernel, n_lat, tm, moe),
        grid=(L // tm,),
        in_specs=([row(d), _const_spec(mods.shape), row(HEAD_PAD), row(NA_WIDTH), row(S5_WIDTH), row(S5_WIDTH),
                   row(N_BRANCH * d)] + [_const_spec(w.shape) for w in weights]),
        out_specs=[row(w) for w, _ in outs],
        out_shape=[jax.ShapeDtypeStruct((L, w), t) for w, t in outs],
        compiler_params=_params(("arbitrary",)),
        name="post_moe" if moe else "post",
    )(x, mods, o_a, o_b, ys, u, gate, *weights)


def _swiglu(h, wg_ref, wu_ref, wd_ref):
    a = _dot(h, wg_ref[...])
    act = (a * jax.nn.sigmoid(a) * _dot(h, wu_ref[...])).astype(BF16)
    return _dot(act, wd_ref[...])


def _ffn_kernel(n_lat, tm, x_ref, h_ref, mod_ref, wg_ref, wu_ref, wd_ref, o_ref):
    f = _swiglu(h_ref[...], wg_ref, wu_ref, wd_ref)
    o_ref[...] = x_ref[...] + _row_select(pl.program_id(0), tm, n_lat, mod_ref, 5) * f


def _ffn_call(x, h2, mods, wg, wu, wd, n_lat):
    L, d = x.shape
    tm = _pick(L, (640, 256, 128))
    row = lambda w: pl.BlockSpec((tm, w), lambda i: (i, 0))
    return pl.pallas_call(
        functools.partial(_ffn_kernel, n_lat, tm),
        grid=(L // tm,),
        in_specs=[row(d), row(d), _const_spec(mods.shape), _const_spec(wg.shape), _const_spec(wu.shape),
                  _const_spec(wd.shape)],
        out_specs=row(d),
        out_shape=jax.ShapeDtypeStruct((L, d), F32),
        compiler_params=_params(("arbitrary",)),
        name="ffn",
    )(x, h2, mods, wg, wu, wd)


def _moe_kernel(n_lat, tm, x_ref, h_ref, comb_ref, mod_ref, wg_ref, wu_ref, wd_ref, o_ref, acc_ref):
    e, f_blk = pl.program_id(1), pl.program_id(2)
    comb = comb_ref[...]
    lane = lax.broadcasted_iota(jnp.int32, comb.shape, 1)
    w_e = jnp.sum(jnp.where(lane == e, comb, 0.0), axis=-1, keepdims=True)
    f = w_e * _swiglu(h_ref[...], wg_ref, wu_ref, wd_ref)
    first = (e == 0) & (f_blk == 0)

    @pl.when(first)
    def _():
        acc_ref[...] = f

    @pl.when(jnp.logical_not(first))
    def _():
        acc_ref[...] += f

    @pl.when((e == pl.num_programs(1) - 1) & (f_blk == pl.num_programs(2) - 1))
    def _():
        o_ref[...] = x_ref[...] + _row_select(pl.program_id(0), tm, n_lat, mod_ref, 5) * acc_ref[...]


def _moe_call(x, h2, comb, mods, wg, wu, wd, n_lat):
    L, d = x.shape
    ne, _, dff = wg.shape
    tm = _pick(L, (640, 256, 128))
    tf = dff // 2
    row = lambda w: pl.BlockSpec((tm, w), lambda i, e, f: (i, 0))
    return pl.pallas_call(
        functools.partial(_moe_kernel, n_lat, tm),
        grid=(L // tm, ne, dff // tf),
        in_specs=[row(d), row(d), row(LANES), _const_spec(mods.shape),
                  pl.BlockSpec((None, d, tf), lambda i, e, f: (e, 0, f)),
                  pl.BlockSpec((None, d, tf), lambda i, e, f: (e, 0, f)),
                  pl.BlockSpec((None, tf, d), lambda i, e, f: (e, f, 0))],
        out_specs=row(d),
        out_shape=jax.ShapeDtypeStruct((L, d), F32),
        scratch_shapes=[pltpu.VMEM((tm, d), F32)],
        compiler_params=_params(("arbitrary", "arbitrary", "arbitrary")),
        name="moe",
    )(x, h2, comb, mods, wg, wu, wd)


def _rope_tables(n_lat, n_ctx):
    t = jnp.arange(n_lat, dtype=jnp.int32)
    n_freq = MLA_ROPE // 4
    inv = ROPE_THETA ** (-jnp.arange(n_freq, dtype=F32) / n_freq)
    ang = jnp.concatenate([(t // GRID_W).astype(F32)[:, None] * inv[None],
                           (t % GRID_W).astype(F32)[:, None] * inv[None]], axis=-1)
    cos, sin = jnp.cos(ang), jnp.sin(ang)
    ones = jnp.ones((n_lat, MLA_NOPE), F32)
    zn = jnp.zeros((n_lat, MLA_NOPE), F32)
    zh = jnp.zeros_like(sin)
    zp = jnp.zeros((n_lat, LANES - MLA_QK), F32)
    c = jnp.concatenate([ones, cos, cos, zp], axis=-1)
    s1 = jnp.concatenate([zn, -sin, zh, zp], axis=-1)
    s2 = jnp.concatenate([zn, zh, sin, zp], axis=-1)
    ctx = lambda a, fill: jnp.concatenate([a, jnp.full((n_ctx, LANES), fill, F32)], axis=0)
    return ctx(c, 1.0), ctx(s1, 0.0), ctx(s2, 0.0)


def _head_pad_cols(w, width):
    r = w.shape[0]
    w = w.reshape(r, MLA_HEADS, width)
    return jnp.pad(w, ((0, 0), (0, 0), (0, LANES - width))).reshape(r, HEAD_PAD)


def _layer_weights(i, p):
    d = D_MODEL
    w_in = p["w_in"][i]
    cuts = np.cumsum((0,) + IN_SPLITS)
    piece = lambda j: w_in[:, cuts[j]:cuts[j + 1]]
    bf = lambda w: w.astype(BF16)
    row = lambda v: v.reshape(1, -1).astype(F32)
    ukv = p["w_mla_ukv"][i].reshape(MLA_KV_RANK, MLA_HEADS, MLA_NOPE + MLA_V)
    wk_nope = _head_pad_cols(ukv[:, :, :MLA_NOPE].reshape(MLA_KV_RANK, -1), MLA_NOPE)
    kr_place = jnp.zeros((LANES, MLA_HEADS, LANES), F32)
    eye = jnp.eye(MLA_ROPE, dtype=F32)
    kr_place = kr_place.at[:MLA_ROPE, :, MLA_NOPE:MLA_QK].set(jnp.broadcast_to(eye[:, None, :], (MLA_ROPE, MLA_HEADS, MLA_ROPE)))
    vone = jnp.zeros((MLA_HEADS, LANES), F32).at[:, MLA_V].set(1.0).reshape(1, HEAD_PAD)
    head_gain = lambda g: jnp.tile(jnp.pad(g, (0, LANES - MLA_QK)), MLA_HEADS).reshape(1, HEAD_PAD)
    e64 = jnp.kron(jnp.eye(NA_HEADS, dtype=F32), jnp.ones((NA_DIM, NA_DIM), F32))
    wa = jnp.pad(p["w_br_mla"][i].reshape(MLA_HEADS, MLA_V, d), ((0, 0), (0, LANES - MLA_V), (0, 0)))
    return {
        "g1": row(p["g_norm1"][i]), "g2": row(p["g_norm2"][i]),
        "wq": bf(piece(0)), "wkv": bf(piece(1)),
        "wkr": bf(jnp.pad(piece(2), ((0, 0), (0, LANES - MLA_ROPE)))),
        "wnq": bf(piece(3)), "wnk": bf(piece(4)), "wnv": bf(piece(5)), "wu": bf(piece(6)), "wg": bf(piece(7)),
        "gq": row(p["g_mla_q"][i]), "gkv": row(p["g_mla_kv"][i]),
        "wuq": bf(_head_pad_cols(p["w_mla_uq"][i], MLA_QK)),
        "wk": bf(jnp.concatenate([wk_nope, kr_place.reshape(LANES, HEAD_PAD)], axis=0)),
        "wv": bf(_head_pad_cols(ukv[:, :, MLA_NOPE:].reshape(MLA_KV_RANK, -1), MLA_V)),
        "vone": vone,
        "gqn": head_gain(p["g_mla_qn"][i]), "gkn": head_gain(p["g_mla_kn"][i]),
        "e64": bf(e64),
        "gnq": jnp.tile(p["g_na_qn"][i], NA_HEADS).reshape(1, -1), "gnk": jnp.tile(p["g_na_kn"][i], NA_HEADS).reshape(1, -1),
        "wa": bf(wa.reshape(HEAD_PAD, d)), "wb": bf(p["w_br_na"][i]), "wc": bf(p["w_br_s5"][i]),
        "wglu": bf(p["w_glu"][i]), "bglu": row(p["b_glu"][i]), "d": row(p["s5_d"][i]), "wo": bf(p["w_out"][i]),
    }


def kernel(x, c, ctx, c_ctx, w_mod, b_mod, g_norm1, g_norm2, w_in, g_mla_q, g_mla_kv, w_mla_uq, w_mla_ukv,
           g_mla_qn, g_mla_kn, g_na_qn, g_na_kn, na_rpb, s5_a_re, s5_a_im, s5_log_dt, s5_b_re, s5_b_im,
           s5_c_re, s5_c_im, s5_d, w_glu, b_glu, w_br_mla, w_br_na, w_br_s5, w_out, w_ffn_gate, w_ffn_up,
           w_ffn_down, w_router, w_exp_gate, w_exp_up, w_exp_down):
    p = dict(w_in=w_in, g_norm1=g_norm1, g_norm2=g_norm2, g_mla_q=g_mla_q, g_mla_kv=g_mla_kv, w_mla_uq=w_mla_uq,
             w_mla_ukv=w_mla_ukv, g_mla_qn=g_mla_qn, g_mla_kn=g_mla_kn, g_na_qn=g_na_qn, g_na_kn=g_na_kn,
             s5_d=s5_d, w_glu=w_glu, b_glu=b_glu, w_br_mla=w_br_mla, w_br_na=w_br_na, w_br_s5=w_br_s5, w_out=w_out)
    assert x.shape[0] == 1 and x.shape[2] == D_MODEL
    n_lat, n_ctx = x.shape[1], ctx.shape[1]
    depth = w_mod.shape[0]
    xs = jnp.concatenate([x[0], ctx[0]], axis=0)

    cvec = jnp.zeros((8, D_MODEL), F32).at[0].set(c[0]).at[1].set(c_ctx)
    mods_all = _mod_call(cvec, w_mod, b_mod)[:, :2, None, :]
    rope = _rope_tables(n_lat, n_ctx)

    for i in range(depth):
        lw = _layer_weights(i, p)
        mods = mods_all[i]
        q, k, v, nq, nk, nv, u, gate = _pre_call(xs, mods, n_lat, lw, rope)

        o_a = _mla_call(q, k, v, n_lat)
        bias = _na_bias(na_rpb[i], n_lat // GRID_W)
        o_b = _na_call(nq, nk, nv, bias, n_lat, n_ctx)
        o_a_c, o_b_c = _ctx_call(q, k, v, nq, nk, nv, n_lat, n_ctx)
        o_a = lax.dynamic_update_slice(o_a, o_a_c, (n_lat, 0))
        o_b = lax.dynamic_update_slice(o_b, o_b_c, (n_lat, 0))

        ops = _s5_operands(*_s5prep_call(s5_a_re[i], s5_a_im[i], s5_log_dt[i], s5_b_re[i], s5_b_im[i],
                                         s5_c_re[i], s5_c_im[i]))
        ys = _s5_mix(u, ops, n_lat, n_ctx)

        moe = i % 2 == 1
        j = i // 2
        if moe:
            lw["wrouter"] = jnp.pad(w_router[j], ((0, 0), (0, LANES - N_EXPERTS)))
            xs, h2, comb = _post_call(xs, mods, o_a, o_b, ys, u, gate, lw, n_lat, True)
            xs = _moe_call(xs, h2, comb, mods, w_exp_gate[j].astype(BF16), w_exp_up[j].astype(BF16),
                           w_exp_down[j].astype(BF16), n_lat)
        else:
            xs, h2 = _post_call(xs, mods, o_a, o_b, ys, u, gate, lw, n_lat, False)
            xs = _ffn_call(xs, h2, mods, w_ffn_gate[j].astype(BF16), w_ffn_up[j].astype(BF16),
                           w_ffn_down[j].astype(BF16), n_lat)
    return xs[:n_lat][None]
```

```python
import functools
import math

import jax
import jax.numpy as jnp
import numpy as np
from jax import lax
from jax.experimental import pallas as pl
from jax.experimental.pallas import tpu as pltpu

F32 = jnp.float32
BF16 = jnp.bfloat16

D_MODEL = 1024
DEPTH = 4
GRID_W = 64
EPS = 1e-6
ROPE_THETA = 10000.0

MLA_HEADS = 8
MLA_NOPE = 64
MLA_ROPE = 32
MLA_V = 64
MLA_Q_RANK = 384
MLA_KV_RANK = 256
MLA_QK = MLA_NOPE + MLA_ROPE

NA_HEADS = 8
NA_DIM = 64
NA_KH = 8
NA_KW = 16
NA_WIDTH = NA_HEADS * NA_DIM

S5_GROUPS = 32
S5_GCH = 16
S5_STATE = 64
S5_WIDTH = S5_GROUPS * S5_GCH

N_BRANCH = 3
D_FF = 2816
N_EXPERTS = 8
TOP_K = 2

IN_SPLITS = (MLA_Q_RANK, MLA_KV_RANK, MLA_ROPE, NA_WIDTH, NA_WIDTH, NA_WIDTH, S5_WIDTH, N_BRANCH * D_MODEL)

LANES = 128
HEAD_PAD = MLA_HEADS * LANES
S5_T = 16
NA_QROWS = 4
NA_WROWS = NA_QROWS + NA_KH - 1
VMEM_LIMIT = 56 * 1024 * 1024
NEG = -1e30


def _pick(n, candidates):
    for c in candidates:
        if n % c == 0:
            return c
    raise ValueError(f"no tile in {candidates} divides {n}")


def _const_spec(shape):
    nd = len(shape)
    return pl.BlockSpec(shape, lambda *_: (0,) * nd, pipeline_mode=pl.Buffered(1))


def _params(sem):
    return pltpu.CompilerParams(dimension_semantics=sem, vmem_limit_bytes=VMEM_LIMIT)


def _dot(a, b):
    return jnp.dot(a, b, preferred_element_type=F32)


def _dot_t(a, b):
    return lax.dot_general(a, b, (((1,), (1,)), ((), ())), preferred_element_type=F32)


def _split(x):
    hi = x.astype(BF16)
    lo = (x - hi.astype(F32)).astype(BF16)
    return hi, lo


def _dot3(a, b):
    a_hi, a_lo = _split(a)
    b_hi, b_lo = _split(b)
    return _dot(a_hi, b_hi) + _dot(a_hi, b_lo) + _dot(a_lo, b_hi)


def _dot3_t(a, b):
    a_hi, a_lo = _split(a)
    b_hi, b_lo = _split(b)
    return _dot_t(a_hi, b_hi) + _dot_t(a_hi, b_lo) + _dot_t(a_lo, b_hi)


def _rms(x, g):
    ms = jnp.mean(x * x, axis=-1, keepdims=True)
    return x * lax.rsqrt(ms + EPS) * g


def _row_select(tile, tm, n_lat, mod_ref, idx):
    rows = tile * tm + lax.broadcasted_iota(jnp.int32, (tm, 1), 0)
    lat = mod_ref[0, :, idx * D_MODEL:(idx + 1) * D_MODEL]
    ctx = mod_ref[1, :, idx * D_MODEL:(idx + 1) * D_MODEL]
    return jnp.where(rows < n_lat, lat, ctx)


def _mod_kernel(c_ref, w_ref, b_ref, o_ref):
    c = c_ref[...]
    a = c * jax.nn.sigmoid(c)
    o_ref[...] = _dot3(a, w_ref[...]) + b_ref[...]


def _mod_call(cvec, w_mod, b_mod):
    depth, d, n6 = w_mod.shape
    tn = _pick(n6, (1536, 1024, 512, 128))
    return pl.pallas_call(
        _mod_kernel,
        grid=(depth, n6 // tn),
        in_specs=[
            pl.BlockSpec((8, d), lambda l, j: (0, 0)),
            pl.BlockSpec((None, d, tn), lambda l, j: (l, 0, j)),
            pl.BlockSpec((None, 1, tn), lambda l, j: (l, 0, j)),
        ],
        out_specs=pl.BlockSpec((None, 8, tn), lambda l, j: (l, 0, j)),
        out_shape=jax.ShapeDtypeStruct((depth, 8, n6), F32),
        compiler_params=_params(("arbitrary", "arbitrary")),
        name="mod",
    )(cvec, w_mod, b_mod.reshape(depth, 1, n6))


def _head_norm_rope(xp, gain, c_t, s1_t, s2_t, scale, o_ref):
    for h in range(MLA_HEADS):
        sl = slice(h * LANES, (h + 1) * LANES)
        xh = xp[:, sl]
        ms = jnp.sum(xh * xh, axis=-1, keepdims=True) * (1.0 / MLA_QK)
        y = xh * lax.rsqrt(ms + EPS) * gain[:, sl]
        y = y * c_t + pltpu.roll(y, LANES - MLA_ROPE // 2, 1) * s1_t + pltpu.roll(y, MLA_ROPE // 2, 1) * s2_t
        if scale != 1.0:
            y = y * scale
        o_ref[:, sl] = y.astype(o_ref.dtype)


def _pre_kernel(n_lat, tm, mla_scale, na_scale,
                x_ref, mod_ref, g1_ref, wq_ref, wkv_ref, wkr_ref, wnq_ref, wnk_ref, wnv_ref, wu_ref, wg_ref,
                gq_ref, gkv_ref, wuq_ref, wk_ref, wv_ref, vone_ref, gqn_ref, gkn_ref,
                rc_ref, rs1_ref, rs2_ref, e64_ref, gnq_ref, gnk_ref,
                q_ref, k_ref, v_ref, nq_ref, nk_ref, nv_ref, u_ref, gate_ref):
    i = pl.program_id(0)
    sh = _row_select(i, tm, n_lat, mod_ref, 0)
    sc = _row_select(i, tm, n_lat, mod_ref, 1)
    h = (_rms(x_ref[...], g1_ref[...]) * (1.0 + sc) + sh).astype(BF16)

    c_t, s1_t, s2_t = rc_ref[...], rs1_ref[...], rs2_ref[...]

    rq = _rms(_dot(h, wq_ref[...]), gq_ref[...]).astype(BF16)
    _head_norm_rope(_dot(rq, wuq_ref[...]), gqn_ref[...], c_t, s1_t, s2_t, mla_scale, q_ref)
    rkv = _rms(_dot(h, wkv_ref[...]), gkv_ref[...]).astype(BF16)
    pkr = _dot(h, wkr_ref[...]).astype(BF16)
    kin = jnp.concatenate([rkv, pkr], axis=-1)
    _head_norm_rope(_dot(kin, wk_ref[...]), gkn_ref[...], c_t, s1_t, s2_t, 1.0, k_ref)
    v_ref[...] = (_dot(rkv, wv_ref[...]) + vone_ref[...]).astype(BF16)

    def na_norm(w_ref, g_ref, scale):
        p = _dot(h, w_ref[...])
        sq_hi, sq_lo = _split(p * p)
        ss = (_dot(sq_hi, e64_ref[...]) + _dot(sq_lo, e64_ref[...])) * (1.0 / NA_DIM)
        return (p * lax.rsqrt(ss + EPS) * (g_ref[...] * scale)).astype(BF16)

    nq_ref[...] = na_norm(wnq_ref, gnq_ref, na_scale)
    nk_ref[...] = na_norm(wnk_ref, gnk_ref, 1.0)
    nv_ref[...] = _dot(h, wnv_ref[...]).astype(BF16)

    u_ref[...] = _dot(h, wu_ref[...])
    gate_ref[...] = jax.nn.sigmoid(_dot(h, wg_ref[...])).astype(BF16)


def _pre_call(x, mods, n_lat, lw, rope):
    L, d = x.shape
    tm = _pick(L, (320, 256, 128))
    row = lambda w: pl.BlockSpec((tm, w), lambda i: (i, 0))
    weights = [lw[k] for k in ("g1", "wq", "wkv", "wkr", "wnq", "wnk", "wnv", "wu", "wg", "gq", "gkv", "wuq",
                               "wk", "wv", "vone", "gqn", "gkn")]
    tail = [lw["e64"], lw["gnq"], lw["gnk"]]
    in_specs = ([row(d), _const_spec(mods.shape)] + [_const_spec(w.shape) for w in weights]
                + [row(LANES)] * 3 + [_const_spec(w.shape) for w in tail])
    outs = [(HEAD_PAD, BF16)] * 3 + [(NA_WIDTH, BF16)] * 3 + [(S5_WIDTH, F32), (N_BRANCH * d, BF16)]
    return pl.pallas_call(
        functools.partial(_pre_kernel, n_lat, tm, MLA_QK ** -0.5 * math.log2(math.e), NA_DIM ** -0.5),
        grid=(L // tm,),
        in_specs=in_specs,
        out_specs=[row(w) for w, _ in outs],
        out_shape=[jax.ShapeDtypeStruct((L, w), t) for w, t in outs],
        compiler_params=_params(("arbitrary",)),
        name="pre",
    )(x, mods, *weights, *rope, *tail)


def _mla_kernel(tq, tk, n_pairs, q_ref, k_ref, v_ref, o_ref, m_ref, acc_ref, sa_ref, sb_ref):
    m_ref[...] = jnp.full_like(m_ref, NEG)
    acc_ref[...] = jnp.zeros_like(acc_ref)
    q = q_ref[...]
    last = 2 * n_pairs - 1

    def scores(c):
        off = pl.multiple_of(c * tk, tk)
        return _dot_t(q, k_ref[pl.ds(off, tk), :])

    def absorb(s, c):
        off = pl.multiple_of(c * tk, tk)
        m_prev = m_ref[...]
        m_new = jnp.maximum(m_prev, jnp.max(s, axis=-1, keepdims=True))
        p = jnp.exp2(s - pltpu.repeat(m_new, tk // LANES, axis=1))
        acc_ref[...] = jnp.exp2(m_prev - m_new) * acc_ref[...] + _dot(p.astype(BF16), v_ref[pl.ds(off, tk), :])
        m_ref[...] = m_new

    sa_ref[...] = scores(0)

    def body(i, carry):
        c = 2 * i
        sb_ref[...] = scores(c + 1)
        absorb(sa_ref[...], c)
        sa_ref[...] = scores(jnp.minimum(c + 2, last))
        absorb(sb_ref[...], c + 1)
        return carry

    lax.fori_loop(0, n_pairs, body, 0)
    acc = acc_ref[...]
    o_ref[...] = (acc * pl.reciprocal(acc[:, MLA_V:MLA_V + 1], approx=False)).astype(o_ref.dtype)


def _mla_call(q, k, v, n_lat):
    L = q.shape[0]
    tq = _pick(n_lat, (512, 256))
    tk = _pick(L, (1280, 768, 512, 256)) // 2
    kv_spec = pl.BlockSpec((L, LANES), lambda h, i: (0, h))
    return pl.pallas_call(
        functools.partial(_mla_kernel, tq, tk, L // (2 * tk)),
        grid=(MLA_HEADS, n_lat // tq),
        in_specs=[pl.BlockSpec((tq, LANES), lambda h, i: (i, h)), kv_spec, kv_spec],
        out_specs=pl.BlockSpec((tq, LANES), lambda h, i: (i, h)),
        out_shape=jax.ShapeDtypeStruct((L, HEAD_PAD), BF16),
        scratch_shapes=[pltpu.VMEM((tq, LANES), F32), pltpu.VMEM((tq, LANES), F32),
                        pltpu.VMEM((tq, tk), F32), pltpu.VMEM((tq, tk), F32)],
        compiler_params=_params(("arbitrary", "arbitrary")),
        name="mla",
    )(q, k, v)


def _softmax_pv(parts):
    m = parts[0][0].max(axis=-1, keepdims=True)
    for s, _ in parts[1:]:
        m = jnp.maximum(m, s.max(axis=-1, keepdims=True))
    num, den = 0.0, 0.0
    for s, v in parts:
        p = jnp.exp(s - m)
        den = den + p.sum(axis=-1, keepdims=True)
        num = num + _dot(p.astype(BF16), v)
    return num * pl.reciprocal(den, approx=False)


def _na_kernel(n_lat, n_ctx, n_blocks, q_ref, k_ref, v_ref, bias_ref, o_ref):
    b = pl.program_id(1)
    wtok = NA_WROWS * GRID_W
    row0 = jnp.clip(b * NA_QROWS - NA_KH // 2, 0, n_lat // GRID_W - NA_WROWS)
    off = pl.multiple_of(row0 * GRID_W, GRID_W)
    q = q_ref[...]
    kw, vw = k_ref[pl.ds(off, wtok), :], v_ref[pl.ds(off, wtok), :]
    kc, vc = k_ref[n_lat:n_lat + n_ctx, :], v_ref[n_lat:n_lat + n_ctx, :]
    outs = []
    for j in range(LANES // NA_DIM):
        sl = slice(j * NA_DIM, (j + 1) * NA_DIM)
        qh = q[:, sl]
        s_win = _dot_t(qh, kw[:, sl]) + bias_ref[j]
        s_ctx = _dot_t(qh, kc[:, sl])
        outs.append(_softmax_pv([(s_ctx, vc[:, sl]), (s_win, vw[:, sl])]))
    o_ref[...] = jnp.concatenate(outs, axis=-1).astype(o_ref.dtype)


def _na_bias(rpb, n_rows):
    n_blocks = n_rows // NA_QROWS
    cols = np.arange(GRID_W)
    c0 = np.clip(cols - NA_KW // 2, 0, GRID_W - NA_KW)
    col_ok = (cols[None, :] >= c0[:, None]) & (cols[None, :] < c0[:, None] + NA_KW)
    dc = cols[None, :] - cols[:, None] + NA_KW - 1
    pick_c = (col_ok[:, :, None] & (dc[:, :, None] == np.arange(2 * NA_KW - 1))).astype(np.float32)
    pick_r = np.zeros((3, NA_QROWS, NA_WROWS, 2 * NA_KH - 1), np.float32)
    row_oks = []
    for v, b in enumerate((0, 1, n_blocks - 1)):
        row0 = min(max(b * NA_QROWS - NA_KH // 2, 0), n_rows - NA_WROWS)
        qr = b * NA_QROWS + np.arange(NA_QROWS)
        kr = row0 + np.arange(NA_WROWS)
        r0 = np.clip(qr - NA_KH // 2, 0, n_rows - NA_KH)
        row_ok = (kr[None, :] >= r0[:, None]) & (kr[None, :] < r0[:, None] + NA_KH)
        dr = kr[None, :] - qr[:, None] + NA_KH - 1
        pick_r[v] = row_ok[:, :, None] & (dr[:, :, None] == np.arange(2 * NA_KH - 1))
        row_oks.append(row_ok)
    hi = lax.Precision.HIGHEST
    by_col = jnp.einsum("hrd,qkd->hrqk", rpb.astype(F32), pick_c, precision=hi)
    vals = jnp.einsum("vabr,hrqk->vhaqbk", pick_r, by_col, precision=hi)
    ok = np.stack(row_oks)[:, None, :, None, :, None] & col_ok[None, None, None, :, None, :]
    vals = jnp.where(ok, vals, NEG)
    return vals.reshape(3, rpb.shape[0], NA_QROWS * GRID_W, NA_WROWS * GRID_W)


def _na_call(nq, nk, nv, bias, n_lat, n_ctx):
    L = nq.shape[0]
    tq = NA_QROWS * GRID_W
    n_blocks = n_lat // tq
    hp = LANES // NA_DIM
    kv_spec = pl.BlockSpec((L, LANES), lambda h, b: (0, h))

    def bias_map(h, b):
        return (jnp.where(b == 0, 0, jnp.where(b == n_blocks - 1, 2, 1)), h, 0, 0)

    return pl.pallas_call(
        functools.partial(_na_kernel, n_lat, n_ctx, n_blocks),
        grid=(NA_HEADS // hp, n_blocks),
        in_specs=[pl.BlockSpec((tq, LANES), lambda h, b: (b, h)), kv_spec, kv_spec,
                  pl.BlockSpec((None, hp, tq, NA_WROWS * GRID_W), bias_map)],
        out_specs=pl.BlockSpec((tq, LANES), lambda h, b: (b, h)),
        out_shape=jax.ShapeDtypeStruct((L, NA_WIDTH), BF16),
        compiler_params=_params(("arbitrary", "arbitrary")),
        name="na",
    )(nq, nk, nv, bias)


def _ctx_kernel(q_ref, k_ref, v_ref, nq_ref, nk_ref, nv_ref, oa_ref, ob_ref):
    for h in range(MLA_HEADS):
        sl = slice(h * LANES, (h + 1) * LANES)
        s = _dot_t(q_ref[:, sl], k_ref[:, sl])
        p = jnp.exp2(s - s.max(axis=-1, keepdims=True))
        acc = _dot(p.astype(BF16), v_ref[:, sl])
        oa_ref[:, sl] = (acc * pl.reciprocal(acc[:, MLA_V:MLA_V + 1], approx=False)).astype(oa_ref.dtype)
    nq, nk, nv = nq_ref[...], nk_ref[...], nv_ref[...]
    outs = []
    for h in range(NA_HEADS):
        sl = slice(h * NA_DIM, (h + 1) * NA_DIM)
        outs.append(_softmax_pv([(_dot_t(nq[:, sl], nk[:, sl]), nv[:, sl])]))
    ob_ref[...] = jnp.concatenate(outs, axis=-1).astype(ob_ref.dtype)


def _ctx_call(q, k, v, nq, nk, nv, n_lat, n_ctx):
    blk = n_lat // n_ctx
    a_spec = pl.BlockSpec((n_ctx, HEAD_PAD), lambda i: (blk, 0))
    b_spec = pl.BlockSpec((n_ctx, NA_WIDTH), lambda i: (blk, 0))
    return pl.pallas_call(
        _ctx_kernel,
        grid=(1,),
        in_specs=[a_spec] * 3 + [b_spec] * 3,
        out_specs=[pl.BlockSpec((n_ctx, HEAD_PAD), lambda i: (0, 0)),
                   pl.BlockSpec((n_ctx, NA_WIDTH), lambda i: (0, 0))],
        out_shape=[jax.ShapeDtypeStruct((n_ctx, HEAD_PAD), BF16), jax.ShapeDtypeStruct((n_ctx, NA_WIDTH), BF16)],
        compiler_params=_params(("arbitrary",)),
        name="ctx_attn",
    )(q, k, v, nq, nk, nv)


def _s5prep_kernel(are_ref, aim_ref, ldt_ref, bre_ref, bim_ref, cre_ref, cim_ref,
                   kall_ref, wr_ref, wi_ref, br_ref, bi_ref, atr_ref, ati_ref):
    a_re, a_im = are_ref[...], aim_ref[...]
    dt = jnp.exp(ldt_ref[...])
    adt_re, adt_im = a_re * dt, a_im * dt

    def power(steps):
        mag = jnp.exp(adt_re * steps)
        return mag * jnp.cos(adt_im * steps), mag * jnp.sin(adt_im * steps)

    steps = lax.broadcasted_iota(jnp.int32, (S5_T, S5_GCH, S5_STATE), 0).astype(F32)
    p0_re, p0_im = power(steps)
    p1_re, p1_im = power(steps + 1.0)
    ab_re, ab_im = power(1.0)
    nr, ni = ab_re - 1.0, ab_im
    den = 1.0 / (a_re * a_re + a_im * a_im)
    f_re, f_im = (nr * a_re + ni * a_im) * den, (ni * a_re - nr * a_im) * den
    b_re, b_im = bre_ref[...], bim_ref[...]
    bb_re, bb_im = f_re * b_re - f_im * b_im, f_re * b_im + f_im * b_re
    c_re, c_im = cre_ref[...], cim_ref[...]
    n = S5_T * S5_GCH
    flat = lambda a: a.reshape(n, S5_STATE)
    w0_re, w0_im = flat(c_re * p0_re - c_im * p0_im), flat(c_re * p0_im + c_im * p0_re)
    kall_ref[...] = _dot3_t(w0_re, bb_re) - _dot3_t(w0_im, bb_im)
    wr_ref[...] = flat(c_re * p1_re - c_im * p1_im)
    wi_ref[...] = flat(c_re * p1_im + c_im * p1_re)
    br_ref[...] = flat(p0_re * bb_re - p0_im * bb_im)
    bi_ref[...] = flat(p0_re * bb_im + p0_im * bb_re)
    atr_ref[...], ati_ref[...] = power(float(S5_T))


def _s5prep_call(a_re, a_im, log_dt, b_re, b_im, c_re, c_im):
    nd, g, p = a_re.shape
    cg = b_re.shape[-1]
    n = S5_T * cg
    vec = pl.BlockSpec((None, None, 1, p), lambda d, j: (d, j, 0, 0))
    mat = pl.BlockSpec((None, None, cg, p), lambda d, j: (d, j, 0, 0))
    big = pl.BlockSpec((None, None, n, p), lambda d, j: (d, j, 0, 0))
    sds = lambda *s: jax.ShapeDtypeStruct((nd, g) + s, F32)
    tr = lambda t: jnp.swapaxes(t, -1, -2)
    return pl.pallas_call(
        _s5prep_kernel,
        grid=(nd, g),
        in_specs=[vec, vec, vec, mat, mat, mat, mat],
        out_specs=[pl.BlockSpec((None, None, n, cg), lambda d, j: (d, j, 0, 0)), big, big, big, big, vec, vec],
        out_shape=[sds(n, cg), sds(n, p), sds(n, p), sds(n, p), sds(n, p), sds(1, p), sds(1, p)],
        compiler_params=_params(("arbitrary", "arbitrary")),
        name="s5_prep",
    )(a_re[:, :, None, :], a_im[:, :, None, :], jnp.broadcast_to(log_dt[:, :, None, None], (nd, g, 1, p)),
      tr(b_re), tr(b_im), c_re, c_im)


S5_SG = LANES // S5_GCH


def _s5_operands(kall, wr, wi, br, bi, atr, ati):
    nd, g = kall.shape[:2]
    ns = g // S5_SG
    t = np.arange(S5_T)
    delta = np.stack([t[None, :] - t[:, None], t[:, None] - t[None, :]])
    toep = (delta[:, None] == t[None, :, None, None]).astype(np.float32)
    k5 = kall.reshape(nd, g, S5_T, S5_GCH, S5_GCH)
    kt = jnp.einsum("zdst,zgdyx->zgsxty", toep, k5, precision=lax.Precision.HIGHEST)
    eye = jnp.eye(S5_SG, dtype=F32)

    def flip(m, which):
        m5 = m.reshape(nd, g, S5_T, S5_GCH, -1)
        return jnp.stack([m5[d][:, ::-1] if d == which else m5[d] for d in range(nd)])

    wr, wi = flip(wr, 1), flip(-wi, 1)
    br, bi = flip(br, 0), flip(bi, 0)
    n = S5_T * LANES
    kt = kt.reshape(nd, ns, S5_SG, S5_T, S5_GCH, S5_T, S5_GCH)
    m = (kt.transpose(0, 1, 3, 2, 4, 5, 6)[:, :, :, :, :, :, None, :]
         * eye[None, None, None, :, None, None, :, None]).reshape(nd, ns, n, n)

    def state_in(b):
        b = b.reshape(nd, ns, S5_SG, S5_T, S5_GCH, S5_STATE).transpose(0, 1, 3, 2, 4, 5)
        return (b[:, :, :, :, :, None, :] * eye[None, None, None, :, None, :, None]).reshape(
            nd, ns, n, S5_SG * S5_STATE).astype(BF16)

    def state_out(w):
        w = w.reshape(nd, ns, S5_SG, S5_T, S5_GCH, S5_STATE).transpose(0, 1, 2, 5, 3, 4)
        return (w[:, :, :, :, :, None, :] * eye[None, None, :, None, None, :, None]).reshape(
            nd, ns, S5_SG * S5_STATE, n).astype(BF16)

    a_t = lambda a: a.reshape(nd, g * S5_STATE)
    return m.astype(BF16), state_out(wr), state_out(wi), state_in(br), state_in(bi), a_t(atr), a_t(ati)


def _chunk_rows(u_ref, kb):
    return jnp.concatenate([u_ref[pl.ds(t, kb, stride=S5_T), :] for t in range(S5_T)], axis=1).astype(BF16)


def _s5a_kernel(kb, u_ref, br_ref, bi_ref, ore_ref, oim_ref):
    x = _chunk_rows(u_ref, kb)
    for d in range(br_ref.shape[0]):
        ore_ref[d] = _dot(x, br_ref[d])
        oim_ref[d] = _dot(x, bi_ref[d])


def _s5_blocks(L):
    nch = L // S5_T
    return nch, _pick(nch, (208, 144, 80, 72, 40, 16, 8))


def _s5a_call(u, bcr, bci):
    L = u.shape[0]
    nd, ns, n, ws = bcr.shape
    nch, kb = _s5_blocks(L)
    wspec = pl.BlockSpec((nd, None, n, ws), lambda s, r: (0, s, 0, 0))
    ospec = pl.BlockSpec((nd, kb, ws), lambda s, r: (0, r, s))
    return pl.pallas_call(
        functools.partial(_s5a_kernel, kb),
        grid=(ns, nch // kb),
        in_specs=[pl.BlockSpec((kb * S5_T, LANES), lambda s, r: (r, s)), wspec, wspec],
        out_specs=[ospec, ospec],
        out_shape=[jax.ShapeDtypeStruct((nd, nch, ns * ws), F32)] * 2,
        compiler_params=_params(("arbitrary", "arbitrary")),
        name="s5_chunk_inputs",
    )(u, bcr, bci)


def _s5scan_kernel(n_lat_ch, n_ctx_ch, bre_ref, bim_ref, atr_ref, ati_ref, sre_ref, sim_ref):
    d = pl.program_id(0)
    a_re, a_im = atr_ref[...], ati_ref[...]

    def segment(base, count, carry):
        def step(i, st):
            s_re, s_im = st
            k = base + jnp.where(d == 0, i, count - 1 - i)
            sre_ref[k] = s_re
            sim_ref[k] = s_im
            return (a_re * s_re - a_im * s_im + bre_ref[k], a_re * s_im + a_im * s_re + bim_ref[k])
        return lax.fori_loop(0, count, step, carry)

    zero = jnp.zeros(a_re.shape, F32)
    carry = segment(n_lat_ch, n_ctx_ch, (zero, zero))
    segment(0, n_lat_ch, carry)


def _s5scan_call(b_re, b_im, atr, ati, n_lat_ch, n_ctx_ch):
    nd, nch, w = b_re.shape
    shp = (nd, nch, w // LANES, LANES)
    spec = pl.BlockSpec((None, nch, 8, LANES), lambda d, j: (d, 0, j, 0))
    aspec = pl.BlockSpec((None, 8, LANES), lambda d, j: (d, j, 0))
    return pl.pallas_call(
        functools.partial(_s5scan_kernel, n_lat_ch, n_ctx_ch),
        grid=(nd, w // LANES // 8),
        in_specs=[spec, spec, aspec, aspec],
        out_specs=[spec, spec],
        out_shape=[jax.ShapeDtypeStruct(shp, F32)] * 2,
        compiler_params=_params(("arbitrary", "arbitrary")),
        name="s5_scan",
    )(b_re.reshape(shp), b_im.reshape(shp), atr.reshape(nd, w // LANES, LANES), ati.reshape(nd, w // LANES, LANES))


def _s5c_kernel(kb, u_ref, sre_ref, sim_ref, m_ref, wr_ref, wi_ref, y_ref):
    x = _chunk_rows(u_ref, kb)
    y = 0.0
    for d in range(m_ref.shape[0]):
        y = (y + _dot(x, m_ref[d]) + _dot(sre_ref[d].astype(BF16), wr_ref[d])
             + _dot(sim_ref[d].astype(BF16), wi_ref[d]))
    for t in range(S5_T):
        y_ref[pl.ds(t, kb, stride=S5_T), :] = y[:, t * LANES:(t + 1) * LANES]


def _s5c_call(u, s_re, s_im, m, wcr, wci):
    L = u.shape[0]
    nd, ns, n, _ = m.shape
    ws = wcr.shape[2]
    nch, kb = _s5_blocks(L)
    once = pl.Buffered(1)
    uspec = pl.BlockSpec((kb * S5_T, LANES), lambda s, r: (r, s))
    sspec = pl.BlockSpec((nd, kb, ws), lambda s, r: (0, r, s))
    wspec = pl.BlockSpec((nd, None, ws, n), lambda s, r: (0, s, 0, 0), pipeline_mode=once)
    return pl.pallas_call(
        functools.partial(_s5c_kernel, kb),
        grid=(ns, nch // kb),
        in_specs=[uspec, sspec, sspec,
                  pl.BlockSpec((nd, None, n, n), lambda s, r: (0, s, 0, 0), pipeline_mode=once), wspec, wspec],
        out_specs=uspec,
        out_shape=jax.ShapeDtypeStruct(u.shape, F32),
        compiler_params=_params(("arbitrary", "arbitrary")),
        name="s5_outputs",
    )(u, s_re, s_im, m, wcr, wci)


def _s5_mix(u, ops, n_lat, n_ctx):
    m, wcr, wci, bcr, bci, atr, ati = ops
    b_re, b_im = _s5a_call(u, bcr, bci)
    s_re, s_im = _s5scan_call(b_re, b_im, atr, ati, n_lat // S5_T, n_ctx // S5_T)
    return _s5c_call(u, s_re.reshape(b_re.shape), s_im.reshape(b_im.shape), m, wcr, wci)


def _gelu_tanh(x):
    return 0.5 * x * (1.0 + jnp.tanh(math.sqrt(2.0 / math.pi) * (x + 0.044715 * (x * x * x))))


def _post_kernel(n_lat, tm, moe,
                 x_ref, mod_ref, oa_ref, ob_ref, ys_ref, u_ref, gate_ref,
                 wa_ref, wb_ref, wc_ref, wglu_ref, bglu_ref, d_ref, wo_ref, g2_ref, *rest):
    i = pl.program_id(0)
    d = x_ref.shape[-1]
    g = _gelu_tanh(d_ref[...] * u_ref[...] + ys_ref[...])
    oc = (g * jax.nn.sigmoid(_dot(g.astype(BF16), wglu_ref[...]) + bglu_ref[...])).astype(BF16)
    gate = gate_ref[...].astype(F32)
    mix = (gate[:, :d] * _dot(oa_ref[...], wa_ref[...]) + gate[:, d:2 * d] * _dot(ob_ref[...], wb_ref[...])
           + gate[:, 2 * d:] * _dot(oc, wc_ref[...]))
    x_new = x_ref[...] + _row_select(i, tm, n_lat, mod_ref, 2) * _dot(mix.astype(BF16), wo_ref[...])
    h2 = (_rms(x_new, g2_ref[...]) * (1.0 + _row_select(i, tm, n_lat, mod_ref, 4))
          + _row_select(i, tm, n_lat, mod_ref, 3))
    if not moe:
        xo_ref, h2_ref = rest
    else:
        wr_ref, xo_ref, h2_ref, comb_ref = rest
        logits = _dot3(h2, wr_ref[...])
        lane = lax.broadcasted_iota(jnp.int32, logits.shape, 1)
        logits = jnp.where(lane < N_EXPERTS, logits, NEG)
        m1 = logits.max(axis=-1, keepdims=True)
        i1 = jnp.where(logits == m1, lane, LANES).min(axis=-1, keepdims=True)
        rest_l = jnp.where(lane == i1, NEG, logits)
        m2 = rest_l.max(axis=-1, keepdims=True)
        i2 = jnp.where(rest_l == m2, lane, LANES).min(axis=-1, keepdims=True)
        e2 = jnp.exp(m2 - m1)
        w1 = 1.0 / (1.0 + e2)
        comb_ref[...] = jnp.where(lane == i1, w1, 0.0) + jnp.where(lane == i2, e2 * w1, 0.0)
    xo_ref[...] = x_new
    h2_ref[...] = h2.astype(BF16)


def _post_call(x, mods, o_a, o_b, ys, u, gate, lw, n_lat, moe):
    L, d = x.shape
    tm = _pick(L, (320, 256, 128))
    row = lambda w: pl.BlockSpec((tm, w), lambda i: (i, 0))
    weights = [lw[k] for k in ("wa", "wb", "wc", "wglu", "bglu", "d", "wo", "g2")]
    if moe:
        weights.append(lw["wrouter"])
    outs = [(d, F32), (d, BF16)] + ([(LANES, F32)] if moe else [])
    return pl.pallas_call(
        functools.partial(_post_kernel, n_lat, tm, moe),
        grid=(L // tm,),
        in_specs=([row(d), _const_spec(mods.shape), row(HEAD_PAD), row(NA_WIDTH), row(S5_WIDTH), row(S5_WIDTH),
                   row(N_BRANCH * d)] + [_const_spec(w.shape) for w in weights]),
        out_specs=[row(w) for w, _ in outs],
        out_shape=[jax.ShapeDtypeStruct((L, w), t) for w, t in outs],
        compiler_params=_params(("arbitrary",)),
        name="post_moe" if moe else "post",
    )(x, mods, o_a, o_b, ys, u, gate, *weights)


def _swiglu(h, wg_ref, wu_ref, wd_ref):
    a = _dot(h, wg_ref[...])
    act = (a * jax.nn.sigmoid(a) * _dot(h, wu_ref[...])).astype(BF16)
    return _dot(act, wd_ref[...])


def _ffn_kernel(n_lat, tm, x_ref, h_ref, mod_ref, wg_ref, wu_ref, wd_ref, o_ref):
    f = _swiglu(h_ref[...], wg_ref, wu_ref, wd_ref)
    o_ref[...] = x_ref[...] + _row_select(pl.program_id(0), tm, n_lat, mod_ref, 5) * f


def _ffn_call(x, h2, mods, wg, wu, wd, n_lat):
    L, d = x.shape
    tm = _pick(L, (640, 256, 128))
    row = lambda w: pl.BlockSpec((tm, w), lambda i: (i, 0))
    return pl.pallas_call(
        functools.partial(_ffn_kernel, n_lat, tm),
        grid=(L // tm,),
        in_specs=[row(d), row(d), _const_spec(mods.shape), _const_spec(wg.shape), _const_spec(wu.shape),
                  _const_spec(wd.shape)],
        out_specs=row(d),
        out_shape=jax.ShapeDtypeStruct((L, d), F32),
        compiler_params=_params(("arbitrary",)),
        name="ffn",
    )(x, h2, mods, wg, wu, wd)


def _moe_kernel(n_lat, tm, x_ref, h_ref, comb_ref, mod_ref, wg_ref, wu_ref, wd_ref, o_ref, acc_ref):
    e, f_blk = pl.program_id(1), pl.program_id(2)
    comb = comb_ref[...]
    lane = lax.broadcasted_iota(jnp.int32, comb.shape, 1)
    w_e = jnp.sum(jnp.where(lane == e, comb, 0.0), axis=-1, keepdims=True)
    f = w_e * _swiglu(h_ref[...], wg_ref, wu_ref, wd_ref)
    first = (e == 0) & (f_blk == 0)

    @pl.when(first)
    def _():
        acc_ref[...] = f

    @pl.when(jnp.logical_not(first))
    def _():
        acc_ref[...] += f

    @pl.when((e == pl.num_programs(1) - 1) & (f_blk == pl.num_programs(2) - 1))
    def _():
        o_ref[...] = x_ref[...] + _row_select(pl.program_id(0), tm, n_lat, mod_ref, 5) * acc_ref[...]


def _moe_call(x, h2, comb, mods, wg, wu, wd, n_lat):
    L, d = x.shape
    ne, _, dff = wg.shape
    tm = _pick(L, (640, 256, 128))
    tf = dff // 2
    row = lambda w: pl.BlockSpec((tm, w), lambda i, e, f: (i, 0))
    return pl.pallas_call(
        functools.partial(_moe_kernel, n_lat, tm),
        grid=(L // tm, ne, dff // tf),
        in_specs=[row(d), row(d), row(LANES), _const_spec(mods.shape),
                  pl.BlockSpec((None, d, tf), lambda i, e, f: (e, 0, f)),
                  pl.BlockSpec((None, d, tf), lambda i, e, f: (e, 0, f)),
                  pl.BlockSpec((None, tf, d), lambda i, e, f: (e, f, 0))],
        out_specs=row(d),
        out_shape=jax.ShapeDtypeStruct((L, d), F32),
        scratch_shapes=[pltpu.VMEM((tm, d), F32)],
        compiler_params=_params(("arbitrary", "arbitrary", "arbitrary")),
        name="moe",
    )(x, h2, comb, mods, wg, wu, wd)


def _rope_tables(n_lat, n_ctx):
    t = jnp.arange(n_lat, dtype=jnp.int32)
    n_freq = MLA_ROPE // 4
    inv = ROPE_THETA ** (-jnp.arange(n_freq, dtype=F32) / n_freq)
    ang = jnp.concatenate([(t // GRID_W).astype(F32)[:, None] * inv[None],
                           (t % GRID_W).astype(F32)[:, None] * inv[None]], axis=-1)
    cos, sin = jnp.cos(ang), jnp.sin(ang)
    ones = jnp.ones((n_lat, MLA_NOPE), F32)
    zn = jnp.zeros((n_lat, MLA_NOPE), F32)
    zh = jnp.zeros_like(sin)
    zp = jnp.zeros((n_lat, LANES - MLA_QK), F32)
    c = jnp.concatenate([ones, cos, cos, zp], axis=-1)
    s1 = jnp.concatenate([zn, -sin, zh, zp], axis=-1)
    s2 = jnp.concatenate([zn, zh, sin, zp], axis=-1)
    ctx = lambda a, fill: jnp.concatenate([a, jnp.full((n_ctx, LANES), fill, F32)], axis=0)
    return ctx(c, 1.0), ctx(s1, 0.0), ctx(s2, 0.0)


def _head_pad_cols(w, width):
    r = w.shape[0]
    w = w.reshape(r, MLA_HEADS, width)
    return jnp.pad(w, ((0, 0), (0, 0), (0, LANES - width))).reshape(r, HEAD_PAD)


def _layer_weights(i, p):
    d = D_MODEL
    w_in = p["w_in"][i]
    cuts = np.cumsum((0,) + IN_SPLITS)
    piece = lambda j: w_in[:, cuts[j]:cuts[j + 1]]
    bf = lambda w: w.astype(BF16)
    row = lambda v: v.reshape(1, -1).astype(F32)
    ukv = p["w_mla_ukv"][i].reshape(MLA_KV_RANK, MLA_HEADS, MLA_NOPE + MLA_V)
    wk_nope = _head_pad_cols(ukv[:, :, :MLA_NOPE].reshape(MLA_KV_RANK, -1), MLA_NOPE)
    kr_place = jnp.zeros((LANES, MLA_HEADS, LANES), F32)
    eye = jnp.eye(MLA_ROPE, dtype=F32)
    kr_place = kr_place.at[:MLA_ROPE, :, MLA_NOPE:MLA_QK].set(jnp.broadcast_to(eye[:, None, :], (MLA_ROPE, MLA_HEADS, MLA_ROPE)))
    vone = jnp.zeros((MLA_HEADS, LANES), F32).at[:, MLA_V].set(1.0).reshape(1, HEAD_PAD)
    head_gain = lambda g: jnp.tile(jnp.pad(g, (0, LANES - MLA_QK)), MLA_HEADS).reshape(1, HEAD_PAD)
    e64 = jnp.kron(jnp.eye(NA_HEADS, dtype=F32), jnp.ones((NA_DIM, NA_DIM), F32))
    wa = jnp.pad(p["w_br_mla"][i].reshape(MLA_HEADS, MLA_V, d), ((0, 0), (0, LANES - MLA_V), (0, 0)))
    return {
        "g1": row(p["g_norm1"][i]), "g2": row(p["g_norm2"][i]),
        "wq": bf(piece(0)), "wkv": bf(piece(1)),
        "wkr": bf(jnp.pad(piece(2), ((0, 0), (0, LANES - MLA_ROPE)))),
        "wnq": bf(piece(3)), "wnk": bf(piece(4)), "wnv": bf(piece(5)), "wu": bf(piece(6)), "wg": bf(piece(7)),
        "gq": row(p["g_mla_q"][i]), "gkv": row(p["g_mla_kv"][i]),
        "wuq": bf(_head_pad_cols(p["w_mla_uq"][i], MLA_QK)),
        "wk": bf(jnp.concatenate([wk_nope, kr_place.reshape(LANES, HEAD_PAD)], axis=0)),
        "wv": bf(_head_pad_cols(ukv[:, :, MLA_NOPE:].reshape(MLA_KV_RANK, -1), MLA_V)),
        "vone": vone,
        "gqn": head_gain(p["g_mla_qn"][i]), "gkn": head_gain(p["g_mla_kn"][i]),
        "e64": bf(e64),
        "gnq": jnp.tile(p["g_na_qn"][i], NA_HEADS).reshape(1, -1), "gnk": jnp.tile(p["g_na_kn"][i], NA_HEADS).reshape(1, -1),
        "wa": bf(wa.reshape(HEAD_PAD, d)), "wb": bf(p["w_br_na"][i]), "wc": bf(p["w_br_s5"][i]),
        "wglu": bf(p["w_glu"][i]), "bglu": row(p["b_glu"][i]), "d": row(p["s5_d"][i]), "wo": bf(p["w_out"][i]),
    }


def kernel(x, c, ctx, c_ctx, w_mod, b_mod, g_norm1, g_norm2, w_in, g_mla_q, g_mla_kv, w_mla_uq, w_mla_ukv,
           g_mla_qn, g_mla_kn, g_na_qn, g_na_kn, na_rpb, s5_a_re, s5_a_im, s5_log_dt, s5_b_re, s5_b_im,
           s5_c_re, s5_c_im, s5_d, w_glu, b_glu, w_br_mla, w_br_na, w_br_s5, w_out, w_ffn_gate, w_ffn_up,
           w_ffn_down, w_router, w_exp_gate, w_exp_up, w_exp_down):
    p = dict(w_in=w_in, g_norm1=g_norm1, g_norm2=g_norm2, g_mla_q=g_mla_q, g_mla_kv=g_mla_kv, w_mla_uq=w_mla_uq,
             w_mla_ukv=w_mla_ukv, g_mla_qn=g_mla_qn, g_mla_kn=g_mla_kn, g_na_qn=g_na_qn, g_na_kn=g_na_kn,
             s5_d=s5_d, w_glu=w_glu, b_glu=b_glu, w_br_mla=w_br_mla, w_br_na=w_br_na, w_br_s5=w_br_s5, w_out=w_out)
    assert x.shape[0] == 1 and x.shape[2] == D_MODEL
    n_lat, n_ctx = x.shape[1], ctx.shape[1]
    depth = w_mod.shape[0]
    xs = jnp.concatenate([x[0], ctx[0]], axis=0)

    cvec = jnp.zeros((8, D_MODEL), F32).at[0].set(c[0]).at[1].set(c_ctx)
    mods_all = _mod_call(cvec, w_mod, b_mod)[:, :2, None, :]
    rope = _rope_tables(n_lat, n_ctx)

    for i in range(depth):
        lw = _layer_weights(i, p)
        mods = mods_all[i]
        q, k, v, nq, nk, nv, u, gate = _pre_call(xs, mods, n_lat, lw, rope)

        o_a = _mla_call(q, k, v, n_lat)
        bias = _na_bias(na_rpb[i], n_lat // GRID_W)
        o_b = _na_call(nq, nk, nv, bias, n_lat, n_ctx)
        o_a_c, o_b_c = _ctx_call(q, k, v, nq, nk, nv, n_lat, n_ctx)
        o_a = lax.dynamic_update_slice(o_a, o_a_c, (n_lat, 0))
        o_b = lax.dynamic_update_slice(o_b, o_b_c, (n_lat, 0))

        ops = _s5_operands(*_s5prep_call(s5_a_re[i], s5_a_im[i], s5_log_dt[i], s5_b_re[i], s5_b_im[i],
                                         s5_c_re[i], s5_c_im[i]))
        ys = _s5_mix(u, ops, n_lat, n_ctx)

        moe = i % 2 == 1
        j = i // 2
        if moe:
            lw["wrouter"] = jnp.pad(w_router[j], ((0, 0), (0, LANES - N_EXPERTS)))
            xs, h2, comb = _post_call(xs, mods, o_a, o_b, ys, u, gate, lw, n_lat, True)
            xs = _moe_call(xs, h2, comb, mods, w_exp_gate[j].astype(BF16), w_exp_up[j].astype(BF16),
                           w_exp_down[j].astype(BF16), n_lat)
        else:
            xs, h2 = _post_call(xs, mods, o_a, o_b, ys, u, gate, lw, n_lat, False)
            xs = _ffn_call(xs, h2, mods, w_ffn_gate[j].astype(BF16), w_ffn_up[j].astype(BF16),
                           w_ffn_down[j].astype(BF16), n_lat)
    return xs[:n_lat][None]
```

```python
import functools
import math

import jax
import jax.numpy as jnp
import numpy as np
from jax import lax
from jax.experimental import pallas as pl
from jax.experimental.pallas import tpu as pltpu

F32 = jnp.float32
BF16 = jnp.bfloat16

D_MODEL = 1024
DEPTH = 4
GRID_W = 64
EPS = 1e-6
ROPE_THETA = 10000.0

MLA_HEADS = 8
MLA_NOPE = 64
MLA_ROPE = 32
MLA_V = 64
MLA_Q_RANK = 384
MLA_KV_RANK = 256
MLA_QK = MLA_NOPE + MLA_ROPE

NA_HEADS = 8
NA_DIM = 64
NA_KH = 8
NA_KW = 16
NA_WIDTH = NA_HEADS * NA_DIM

S5_GROUPS = 32
S5_GCH = 16
S5_STATE = 64
S5_WIDTH = S5_GROUPS * S5_GCH

N_BRANCH = 3
D_FF = 2816
N_EXPERTS = 8
TOP_K = 2

IN_SPLITS = (MLA_Q_RANK, MLA_KV_RANK, MLA_ROPE, NA_WIDTH, NA_WIDTH, NA_WIDTH, S5_WIDTH, N_BRANCH * D_MODEL)

LANES = 128
HEAD_PAD = MLA_HEADS * LANES
S5_T = 16
NA_QROWS = 4
NA_WROWS = NA_QROWS + NA_KH - 1
VMEM_LIMIT = 56 * 1024 * 1024
NEG = -1e30


def _pick(n, candidates):
    for c in candidates:
        if n % c == 0:
            return c
    raise ValueError(f"no tile in {candidates} divides {n}")


def _const_spec(shape):
    nd = len(shape)
    return pl.BlockSpec(shape, lambda *_: (0,) * nd, pipeline_mode=pl.Buffered(1))


def _params(sem):
    return pltpu.CompilerParams(dimension_semantics=sem, vmem_limit_bytes=VMEM_LIMIT)


def _dot(a, b):
    return jnp.dot(a, b, preferred_element_type=F32)


def _dot_t(a, b):
    return lax.dot_general(a, b, (((1,), (1,)), ((), ())), preferred_element_type=F32)


def _split(x):
    hi = x.astype(BF16)
    lo = (x - hi.astype(F32)).astype(BF16)
    return hi, lo


def _dot3(a, b):
    a_hi, a_lo = _split(a)
    b_hi, b_lo = _split(b)
    return _dot(a_hi, b_hi) + _dot(a_hi, b_lo) + _dot(a_lo, b_hi)


def _dot3_t(a, b):
    a_hi, a_lo = _split(a)
    b_hi, b_lo = _split(b)
    return _dot_t(a_hi, b_hi) + _dot_t(a_hi, b_lo) + _dot_t(a_lo, b_hi)


def _rms(x, g):
    ms = jnp.mean(x * x, axis=-1, keepdims=True)
    return x * lax.rsqrt(ms + EPS) * g


def _row_select(tile, tm, n_lat, mod_ref, idx):
    rows = tile * tm + lax.broadcasted_iota(jnp.int32, (tm, 1), 0)
    lat = mod_ref[0, :, idx * D_MODEL:(idx + 1) * D_MODEL]
    ctx = mod_ref[1, :, idx * D_MODEL:(idx + 1) * D_MODEL]
    return jnp.where(rows < n_lat, lat, ctx)


def _mod_kernel(c_ref, w_ref, b_ref, o_ref):
    c = c_ref[...]
    a = c * jax.nn.sigmoid(c)
    o_ref[...] = _dot3(a, w_ref[...]) + b_ref[...]


def _mod_call(cvec, w_mod, b_mod):
    depth, d, n6 = w_mod.shape
    tn = _pick(n6, (1536, 1024, 512, 128))
    return pl.pallas_call(
        _mod_kernel,
        grid=(depth, n6 // tn),
        in_specs=[
            pl.BlockSpec((8, d), lambda l, j: (0, 0)),
            pl.BlockSpec((None, d, tn), lambda l, j: (l, 0, j)),
            pl.BlockSpec((None, 1, tn), lambda l, j: (l, 0, j)),
        ],
        out_specs=pl.BlockSpec((None, 8, tn), lambda l, j: (l, 0, j)),
        out_shape=jax.ShapeDtypeStruct((depth, 8, n6), F32),
        compiler_params=_params(("arbitrary", "arbitrary")),
        name="mod",
    )(cvec, w_mod, b_mod.reshape(depth, 1, n6))


def _head_norm_rope(xp, gain, c_t, s1_t, s2_t, scale, o_ref):
    for h in range(MLA_HEADS):
        sl = slice(h * LANES, (h + 1) * LANES)
        xh = xp[:, sl]
        ms = jnp.sum(xh * xh, axis=-1, keepdims=True) * (1.0 / MLA_QK)
        y = xh * lax.rsqrt(ms + EPS) * gain[:, sl]
        y = y * c_t + pltpu.roll(y, LANES - MLA_ROPE // 2, 1) * s1_t + pltpu.roll(y, MLA_ROPE // 2, 1) * s2_t
        if scale != 1.0:
            y = y * scale
        o_ref[:, sl] = y.astype(o_ref.dtype)


def _pre_kernel(n_lat, tm, mla_scale, na_scale,
                x_ref, mod_ref, g1_ref, wq_ref, wkv_ref, wkr_ref, wnq_ref, wnk_ref, wnv_ref, wu_ref, wg_ref,
                gq_ref, gkv_ref, wuq_ref, wk_ref, wv_ref, vone_ref, gqn_ref, gkn_ref,
                rc_ref, rs1_ref, rs2_ref, e64_ref, gnq_ref, gnk_ref,
                q_ref, k_ref, v_ref, nq_ref, nk_ref, nv_ref, u_ref, gate_ref):
    i = pl.program_id(0)
    sh = _row_select(i, tm, n_lat, mod_ref, 0)
    sc = _row_select(i, tm, n_lat, mod_ref, 1)
    h = (_rms(x_ref[...], g1_ref[...]) * (1.0 + sc) + sh).astype(BF16)

    c_t, s1_t, s2_t = rc_ref[...], rs1_ref[...], rs2_ref[...]

    rq = _rms(_dot(h, wq_ref[...]), gq_ref[...]).astype(BF16)
    _head_norm_rope(_dot(rq, wuq_ref[...]), gqn_ref[...], c_t, s1_t, s2_t, mla_scale, q_ref)
    rkv = _rms(_dot(h, wkv_ref[...]), gkv_ref[...]).astype(BF16)
    pkr = _dot(h, wkr_ref[...]).astype(BF16)
    kin = jnp.concatenate([rkv, pkr], axis=-1)
    _head_norm_rope(_dot(kin, wk_ref[...]), gkn_ref[...], c_t, s1_t, s2_t, 1.0, k_ref)
    v_ref[...] = (_dot(rkv, wv_ref[...]) + vone_ref[...]).astype(BF16)

    def na_norm(w_ref, g_ref, scale):
        p = _dot(h, w_ref[...])
        sq_hi, sq_lo = _split(p * p)
        ss = (_dot(sq_hi, e64_ref[...]) + _dot(sq_lo, e64_ref[...])) * (1.0 / NA_DIM)
        return (p * lax.rsqrt(ss + EPS) * (g_ref[...] * scale)).astype(BF16)

    nq_ref[...] = na_norm(wnq_ref, gnq_ref, na_scale)
    nk_ref[...] = na_norm(wnk_ref, gnk_ref, 1.0)
    nv_ref[...] = _dot(h, wnv_ref[...]).astype(BF16)

    u_ref[...] = _dot(h, wu_ref[...])
    gate_ref[...] = jax.nn.sigmoid(_dot(h, wg_ref[...])).astype(BF16)


def _pre_call(x, mods, n_lat, lw, rope):
    L, d = x.shape
    tm = _pick(L, (320, 256, 128))
    row = lambda w: pl.BlockSpec((tm, w), lambda i: (i, 0))
    weights = [lw[k] for k in ("g1", "wq", "wkv", "wkr", "wnq", "wnk", "wnv", "wu", "wg", "gq", "gkv", "wuq",
                               "wk", "wv", "vone", "gqn", "gkn")]
    tail = [lw["e64"], lw["gnq"], lw["gnk"]]
    in_specs = ([row(d), _const_spec(mods.shape)] + [_const_spec(w.shape) for w in weights]
                + [row(LANES)] * 3 + [_const_spec(w.shape) for w in tail])
    outs = [(HEAD_PAD, BF16)] * 3 + [(NA_WIDTH, BF16)] * 3 + [(S5_WIDTH, F32), (N_BRANCH * d, BF16)]
    return pl.pallas_call(
        functools.partial(_pre_kernel, n_lat, tm, MLA_QK ** -0.5 * math.log2(math.e), NA_DIM ** -0.5),
        grid=(L // tm,),
        in_specs=in_specs,
        out_specs=[row(w) for w, _ in outs],
        out_shape=[jax.ShapeDtypeStruct((L, w), t) for w, t in outs],
        compiler_params=_params(("arbitrary",)),
        name="pre",
    )(x, mods, *weights, *rope, *tail)


def _mla_kernel(tq, tk, n_chunks, q_ref, k_ref, v_ref, o_ref, m_ref, acc_ref, sa_ref, sb_ref):
    m_ref[...] = jnp.full_like(m_ref, NEG)
    acc_ref[...] = jnp.zeros_like(acc_ref)
    q = q_ref[...]

    def scores(c):
        off = pl.multiple_of(c * tk, tk)
        return _dot_t(q, k_ref[pl.ds(off, tk), :])

    def absorb(s, c):
        off = pl.multiple_of(c * tk, tk)
        m_prev = m_ref[...]
        m_new = jnp.maximum(m_prev, jnp.max(s, axis=-1, keepdims=True))
        p = jnp.exp2(s - jnp.concatenate([m_new] * (tk // LANES), axis=1))
        acc_ref[...] = jnp.exp2(m_prev - m_new) * acc_ref[...] + _dot(p.astype(BF16), v_ref[pl.ds(off, tk), :])
        m_ref[...] = m_new

    sa_ref[...] = scores(0)

    def body(i, carry):
        c = 2 * i
        sb_ref[...] = scores(c + 1)
        absorb(sa_ref[...], c)
        sa_ref[...] = scores(c + 2)
        absorb(sb_ref[...], c + 1)
        return carry

    lax.fori_loop(0, (n_chunks - 1) // 2, body, 0)
    if n_chunks % 2 == 0:
        sb_ref[...] = scores(n_chunks - 1)
        absorb(sa_ref[...], n_chunks - 2)
        absorb(sb_ref[...], n_chunks - 1)
    else:
        absorb(sa_ref[...], n_chunks - 1)
    acc = acc_ref[...]
    o_ref[...] = (acc * pl.reciprocal(acc[:, MLA_V:MLA_V + 1], approx=False)).astype(o_ref.dtype)


def _mla_call(q, k, v, n_lat):
    L = q.shape[0]
    tq = _pick(n_lat, (512, 256))
    tk = _pick(L, (1280, 768, 512, 256))
    kv_spec = pl.BlockSpec((L, LANES), lambda h, i: (0, h))
    return pl.pallas_call(
        functools.partial(_mla_kernel, tq, tk, L // tk),
        grid=(MLA_HEADS, n_lat // tq),
        in_specs=[pl.BlockSpec((tq, LANES), lambda h, i: (i, h)), kv_spec, kv_spec],
        out_specs=pl.BlockSpec((tq, LANES), lambda h, i: (i, h)),
        out_shape=jax.ShapeDtypeStruct((L, HEAD_PAD), BF16),
        scratch_shapes=[pltpu.VMEM((tq, LANES), F32), pltpu.VMEM((tq, LANES), F32),
                        pltpu.VMEM((tq, tk), F32), pltpu.VMEM((tq, tk), F32)],
        compiler_params=_params(("arbitrary", "arbitrary")),
        name="mla",
    )(q, k, v)


def _softmax_pv(parts):
    m = parts[0][0].max(axis=-1, keepdims=True)
    for s, _ in parts[1:]:
        m = jnp.maximum(m, s.max(axis=-1, keepdims=True))
    num, den = 0.0, 0.0
    for s, v in parts:
        p = jnp.exp(s - m)
        den = den + p.sum(axis=-1, keepdims=True)
        num = num + _dot(p.astype(BF16), v)
    return num * pl.reciprocal(den, approx=False)


def _na_kernel(n_lat, n_ctx, n_blocks, q_ref, k_ref, v_ref, bias_ref, o_ref):
    b = pl.program_id(1)
    wtok = NA_WROWS * GRID_W
    row0 = jnp.clip(b * NA_QROWS - NA_KH // 2, 0, n_lat // GRID_W - NA_WROWS)
    off = pl.multiple_of(row0 * GRID_W, GRID_W)
    q = q_ref[...]
    kw, vw = k_ref[pl.ds(off, wtok), :], v_ref[pl.ds(off, wtok), :]
    kc, vc = k_ref[n_lat:n_lat + n_ctx, :], v_ref[n_lat:n_lat + n_ctx, :]
    outs = []
    for j in range(LANES // NA_DIM):
        sl = slice(j * NA_DIM, (j + 1) * NA_DIM)
        qh = q[:, sl]
        s_win = _dot_t(qh, kw[:, sl]) + bias_ref[j]
        s_ctx = _dot_t(qh, kc[:, sl])
        outs.append(_softmax_pv([(s_ctx, vc[:, sl]), (s_win, vw[:, sl])]))
    o_ref[...] = jnp.concatenate(outs, axis=-1).astype(o_ref.dtype)


def _na_bias(rpb, n_rows):
    n_blocks = n_rows // NA_QROWS
    cols = np.arange(GRID_W)
    c0 = np.clip(cols - NA_KW // 2, 0, GRID_W - NA_KW)
    col_ok = (cols[None, :] >= c0[:, None]) & (cols[None, :] < c0[:, None] + NA_KW)
    dc = cols[None, :] - cols[:, None] + NA_KW - 1
    pick_c = (col_ok[:, :, None] & (dc[:, :, None] == np.arange(2 * NA_KW - 1))).astype(np.float32)
    pick_r = np.zeros((3, NA_QROWS, NA_WROWS, 2 * NA_KH - 1), np.float32)
    row_oks = []
    for v, b in enumerate((0, 1, n_blocks - 1)):
        row0 = min(max(b * NA_QROWS - NA_KH // 2, 0), n_rows - NA_WROWS)
        qr = b * NA_QROWS + np.arange(NA_QROWS)
        kr = row0 + np.arange(NA_WROWS)
        r0 = np.clip(qr - NA_KH // 2, 0, n_rows - NA_KH)
        row_ok = (kr[None, :] >= r0[:, None]) & (kr[None, :] < r0[:, None] + NA_KH)
        dr = kr[None, :] - qr[:, None] + NA_KH - 1
        pick_r[v] = row_ok[:, :, None] & (dr[:, :, None] == np.arange(2 * NA_KH - 1))
        row_oks.append(row_ok)
    hi = lax.Precision.HIGHEST
    by_col = jnp.einsum("hrd,qkd->hrqk", rpb.astype(F32), pick_c, precision=hi)
    vals = jnp.einsum("vabr,hrqk->vhaqbk", pick_r, by_col, precision=hi)
    ok = np.stack(row_oks)[:, None, :, None, :, None] & col_ok[None, None, None, :, None, :]
    vals = jnp.where(ok, vals, NEG)
    return vals.reshape(3, rpb.shape[0], NA_QROWS * GRID_W, NA_WROWS * GRID_W)


def _na_call(nq, nk, nv, bias, n_lat, n_ctx):
    L = nq.shape[0]
    tq = NA_QROWS * GRID_W
    n_blocks = n_lat // tq
    hp = LANES // NA_DIM
    kv_spec = pl.BlockSpec((L, LANES), lambda h, b: (0, h))

    def bias_map(h, b):
        return (jnp.where(b == 0, 0, jnp.where(b == n_blocks - 1, 2, 1)), h, 0, 0)

    return pl.pallas_call(
        functools.partial(_na_kernel, n_lat, n_ctx, n_blocks),
        grid=(NA_HEADS // hp, n_blocks),
        in_specs=[pl.BlockSpec((tq, LANES), lambda h, b: (b, h)), kv_spec, kv_spec,
                  pl.BlockSpec((None, hp, tq, NA_WROWS * GRID_W), bias_map)],
        out_specs=pl.BlockSpec((tq, LANES), lambda h, b: (b, h)),
        out_shape=jax.ShapeDtypeStruct((L, NA_WIDTH), BF16),
        compiler_params=_params(("arbitrary", "arbitrary")),
        name="na",
    )(nq, nk, nv, bias)


def _ctx_kernel(q_ref, k_ref, v_ref, nq_ref, nk_ref, nv_ref, oa_ref, ob_ref):
    for h in range(MLA_HEADS):
        sl = slice(h * LANES, (h + 1) * LANES)
        s = _dot_t(q_ref[:, sl], k_ref[:, sl])
        p = jnp.exp2(s - s.max(axis=-1, keepdims=True))
        acc = _dot(p.astype(BF16), v_ref[:, sl])
        oa_ref[:, sl] = (acc * pl.reciprocal(acc[:, MLA_V:MLA_V + 1], approx=False)).astype(oa_ref.dtype)
    nq, nk, nv = nq_ref[...], nk_ref[...], nv_ref[...]
    outs = []
    for h in range(NA_HEADS):
        sl = slice(h * NA_DIM, (h + 1) * NA_DIM)
        outs.append(_softmax_pv([(_dot_t(nq[:, sl], nk[:, sl]), nv[:, sl])]))
    ob_ref[...] = jnp.concatenate(outs, axis=-1).astype(ob_ref.dtype)


def _ctx_call(q, k, v, nq, nk, nv, n_lat, n_ctx):
    blk = n_lat // n_ctx
    a_spec = pl.BlockSpec((n_ctx, HEAD_PAD), lambda i: (blk, 0))
    b_spec = pl.BlockSpec((n_ctx, NA_WIDTH), lambda i: (blk, 0))
    return pl.pallas_call(
        _ctx_kernel,
        grid=(1,),
        in_specs=[a_spec] * 3 + [b_spec] * 3,
        out_specs=[pl.BlockSpec((n_ctx, HEAD_PAD), lambda i: (0, 0)),
                   pl.BlockSpec((n_ctx, NA_WIDTH), lambda i: (0, 0))],
        out_shape=[jax.ShapeDtypeStruct((n_ctx, HEAD_PAD), BF16), jax.ShapeDtypeStruct((n_ctx, NA_WIDTH), BF16)],
        compiler_params=_params(("arbitrary",)),
        name="ctx_attn",
    )(q, k, v, nq, nk, nv)


S5_SG = LANES // S5_GCH
S5_SW = S5_SG * S5_STATE


def _s5prep_kernel(are_ref, aim_ref, ldt_ref, bre_ref, bim_ref, cre_ref, cim_ref,
                   arec_ref, aimc_ref, ldtc_ref, crec_ref, cimc_ref,
                   d_ref, bbr_ref, bbi_ref, wcr_ref, wci_ref, atr_ref, ati_ref):
    for ref in (d_ref, bbr_ref, bbi_ref, wcr_ref, wci_ref):
        ref[...] = jnp.zeros_like(ref)
    for g in range(S5_SG):
        rows, lanes = slice(g * S5_GCH, (g + 1) * S5_GCH), slice(g * S5_STATE, (g + 1) * S5_STATE)
        a_re, a_im = are_ref[g], aim_ref[g]
        dt = jnp.exp(ldt_ref[g])
        steps = lax.broadcasted_iota(jnp.int32, (S5_T + 1, S5_STATE), 0).astype(F32)
        mag = jnp.exp(a_re * dt * steps)
        p_re, p_im = mag * jnp.cos(a_im * dt * steps), mag * jnp.sin(a_im * dt * steps)
        nr, ni = p_re[1:2] - 1.0, p_im[1:2]
        den = 1.0 / (a_re * a_re + a_im * a_im)
        f_re, f_im = (nr * a_re + ni * a_im) * den, (ni * a_re - nr * a_im) * den
        b_re, b_im = bre_ref[g], bim_ref[g]
        bb_re, bb_im = f_re * b_re - f_im * b_im, f_re * b_im + f_im * b_re
        c_re, c_im = cre_ref[g], cim_ref[g]
        pw = [(p_re[e:e + 1], p_im[e:e + 1]) for e in range(S5_T + 1)]
        w_re = jnp.concatenate([c_re * r - c_im * i for r, i in pw[:S5_T]], axis=0)
        w_im = jnp.concatenate([c_re * i + c_im * r for r, i in pw[:S5_T]], axis=0)
        kt = _dot3_t(bb_re, w_re) - _dot3_t(bb_im, w_im)
        for e in range(S5_T):
            r, i = pw[e]
            d_ref[e, rows, rows] = kt[:, e * S5_GCH:(e + 1) * S5_GCH].astype(d_ref.dtype)
            bbr_ref[e, rows, lanes] = (r * bb_re - i * bb_im).astype(bbr_ref.dtype)
            bbi_ref[e, rows, lanes] = (r * bb_im + i * bb_re).astype(bbi_ref.dtype)
        atr_ref[:, lanes], ati_ref[:, lanes] = pw[S5_T]
        ac_re, ac_im = arec_ref[g], aimc_ref[g]
        dtc = jnp.exp(ldtc_ref[g])
        steps_c = lax.broadcasted_iota(jnp.int32, (S5_STATE, S5_T), 1).astype(F32) + 1.0
        mag_c = jnp.exp(ac_re * dtc * steps_c)
        q_re, q_im = mag_c * jnp.cos(ac_im * dtc * steps_c), mag_c * jnp.sin(ac_im * dtc * steps_c)
        ct_re, ct_im = crec_ref[g], cimc_ref[g]
        for e in range(S5_T):
            r, i = q_re[:, e:e + 1], q_im[:, e:e + 1]
            wcr_ref[e, lanes, rows] = (ct_re * r - ct_im * i).astype(wcr_ref.dtype)
            wci_ref[e, lanes, rows] = (-(ct_re * i + ct_im * r)).astype(wci_ref.dtype)


def _s5prep_call(a_re, a_im, log_dt, b_re, b_im, c_re, c_im):
    nd, g, p = a_re.shape
    cg = b_re.shape[-1]
    ns = g // S5_SG
    spec = lambda *s: pl.BlockSpec((None, S5_SG) + s, lambda d, j: (d, j) + (0,) * len(s))
    out = lambda *s: pl.BlockSpec((None, None) + s, lambda d, j: (d, j) + (0,) * len(s))
    sds = lambda t, *s: jax.ShapeDtypeStruct((nd, ns) + s, t)
    tr = lambda t: jnp.swapaxes(t, -1, -2)
    ldt = jnp.broadcast_to(log_dt[:, :, None, None], (nd, g, 1, p))
    return pl.pallas_call(
        _s5prep_kernel,
        grid=(nd, ns),
        in_specs=[spec(1, p)] * 3 + [spec(cg, p)] * 4 + [spec(p, 1)] * 3 + [spec(p, cg)] * 2,
        out_specs=[out(S5_T, LANES, LANES), out(S5_T, LANES, S5_SW), out(S5_T, LANES, S5_SW),
                   out(S5_T, S5_SW, LANES), out(S5_T, S5_SW, LANES), out(1, S5_SW), out(1, S5_SW)],
        out_shape=[sds(BF16, S5_T, LANES, LANES), sds(BF16, S5_T, LANES, S5_SW), sds(BF16, S5_T, LANES, S5_SW),
                   sds(BF16, S5_T, S5_SW, LANES), sds(BF16, S5_T, S5_SW, LANES), sds(F32, 1, S5_SW),
                   sds(F32, 1, S5_SW)],
        compiler_params=_params(("arbitrary", "arbitrary")),
        name="s5_prep",
    )(a_re[:, :, None, :], a_im[:, :, None, :], ldt, tr(b_re), tr(b_im), c_re, c_im,
      a_re[:, :, :, None], a_im[:, :, :, None], tr(ldt), tr(c_re), tr(c_im))


def _chunk_rows(u_ref, kb):
    return [u_ref[pl.ds(t, kb, stride=S5_T), :].astype(BF16) for t in range(S5_T)]


def _s5a_kernel(kb, u_ref, br_ref, bi_ref, ore_ref, oim_ref):
    x = _chunk_rows(u_ref, kb)
    for d in range(br_ref.shape[0]):
        ahead = lambda t: S5_T - 1 - t if d == 0 else t
        ore_ref[d] = sum(_dot(x[t], br_ref[d, ahead(t)]) for t in range(S5_T))
        oim_ref[d] = sum(_dot(x[t], bi_ref[d, ahead(t)]) for t in range(S5_T))


def _s5_blocks(L):
    nch = L // S5_T
    return nch, _pick(nch, (208, 144, 80, 72, 40, 16, 8))


def _s5a_call(u, bcr, bci):
    L = u.shape[0]
    nd, ns, _, n, ws = bcr.shape
    nch, kb = _s5_blocks(L)
    wspec = pl.BlockSpec((nd, None, S5_T, n, ws), lambda s, r: (0, s, 0, 0, 0))
    ospec = pl.BlockSpec((nd, kb, ws), lambda s, r: (0, r, s))
    return pl.pallas_call(
        functools.partial(_s5a_kernel, kb),
        grid=(ns, nch // kb),
        in_specs=[pl.BlockSpec((kb * S5_T, LANES), lambda s, r: (r, s)), wspec, wspec],
        out_specs=[ospec, ospec],
        out_shape=[jax.ShapeDtypeStruct((nd, nch, ns * ws), F32)] * 2,
        compiler_params=_params(("arbitrary", "arbitrary")),
        name="s5_chunk_inputs",
    )(u, bcr, bci)


def _s5scan_kernel(n_lat_ch, n_ctx_ch, bre_ref, bim_ref, atr_ref, ati_ref, sre_ref, sim_ref):
    d = pl.program_id(0)
    a_re, a_im = atr_ref[...], ati_ref[...]

    def segment(base, count, carry):
        def step(i, st):
            s_re, s_im = st
            k = base + jnp.where(d == 0, i, count - 1 - i)
            sre_ref[k] = s_re
            sim_ref[k] = s_im
            return (a_re * s_re - a_im * s_im + bre_ref[k], a_re * s_im + a_im * s_re + bim_ref[k])
        return lax.fori_loop(0, count, step, carry)

    zero = jnp.zeros(a_re.shape, F32)
    carry = segment(n_lat_ch, n_ctx_ch, (zero, zero))
    segment(0, n_lat_ch, carry)


def _s5scan_call(b_re, b_im, atr, ati, n_lat_ch, n_ctx_ch):
    nd, nch, w = b_re.shape
    shp = (nd, nch, w // LANES, LANES)
    spec = pl.BlockSpec((None, nch, 8, LANES), lambda d, j: (d, 0, j, 0))
    aspec = pl.BlockSpec((None, 8, LANES), lambda d, j: (d, j, 0))
    return pl.pallas_call(
        functools.partial(_s5scan_kernel, n_lat_ch, n_ctx_ch),
        grid=(nd, w // LANES // 8),
        in_specs=[spec, spec, aspec, aspec],
        out_specs=[spec, spec],
        out_shape=[jax.ShapeDtypeStruct(shp, F32)] * 2,
        compiler_params=_params(("arbitrary", "arbitrary")),
        name="s5_scan",
    )(b_re.reshape(shp), b_im.reshape(shp), atr.reshape(nd, w // LANES, LANES), ati.reshape(nd, w // LANES, LANES))


def _s5c_kernel(kb, u_ref, sre_ref, sim_ref, d_ref, wr_ref, wi_ref, y_ref, acc_ref):
    nd = d_ref.shape[0]
    x = jnp.concatenate(_chunk_rows(u_ref, kb), axis=0)
    s_re = [sre_ref[d].astype(BF16) for d in range(nd)]
    s_im = [sim_ref[d].astype(BF16) for d in range(nd)]
    for t in range(S5_T):
        later = lambda d: t if d == 0 else S5_T - 1 - t
        acc_ref[t * kb:(t + 1) * kb, :] = sum(
            _dot(s_re[d], wr_ref[d, later(d)]) + _dot(s_im[d], wi_ref[d, later(d)]) for d in range(nd))
    for e in range(S5_T):
        n = (S5_T - e) * kb
        acc_ref[e * kb:, :] += _dot(x[:n], d_ref[0, e])
        acc_ref[:n, :] += _dot(x[e * kb:], d_ref[1, e])
    for t in range(S5_T):
        y_ref[pl.ds(t, kb, stride=S5_T), :] = acc_ref[t * kb:(t + 1) * kb, :]


def _s5c_call(u, s_re, s_im, dm, wcr, wci):
    L = u.shape[0]
    nd, ns = dm.shape[:2]
    ws = wcr.shape[3]
    nch, kb = _s5_blocks(L)
    uspec = pl.BlockSpec((kb * S5_T, LANES), lambda s, r: (r, s))
    sspec = pl.BlockSpec((nd, kb, ws), lambda s, r: (0, r, s))
    wspec = pl.BlockSpec((nd, None, S5_T, ws, LANES), lambda s, r: (0, s, 0, 0, 0))
    return pl.pallas_call(
        functools.partial(_s5c_kernel, kb),
        grid=(ns, nch // kb),
        in_specs=[uspec, sspec, sspec,
                  pl.BlockSpec((nd, None, S5_T, LANES, LANES), lambda s, r: (0, s, 0, 0, 0)), wspec, wspec],
        out_specs=uspec,
        out_shape=jax.ShapeDtypeStruct(u.shape, F32),
        scratch_shapes=[pltpu.VMEM((kb * S5_T, LANES), F32)],
        compiler_params=_params(("arbitrary", "arbitrary")),
        name="s5_outputs",
    )(u, s_re, s_im, dm, wcr, wci)


def _s5_mix(u, ops, n_lat, n_ctx):
    dm, bcr, bci, wcr, wci, atr, ati = ops
    nd = dm.shape[0]
    b_re, b_im = _s5a_call(u, bcr, bci)
    s_re, s_im = _s5scan_call(b_re, b_im, atr.reshape(nd, -1), ati.reshape(nd, -1), n_lat // S5_T, n_ctx // S5_T)
    return _s5c_call(u, s_re.reshape(b_re.shape), s_im.reshape(b_im.shape), dm, wcr, wci)


def _gelu_tanh(x):
    return 0.5 * x * (1.0 + jnp.tanh(math.sqrt(2.0 / math.pi) * (x + 0.044715 * (x * x * x))))


def _post_kernel(n_lat, tm, moe,
                 x_ref, mod_ref, oa_ref, ob_ref, ys_ref, u_ref, gate_ref,
                 wa_ref, wb_ref, wc_ref, wglu_ref, bglu_ref, d_ref, wo_ref, g2_ref, *rest):
    i = pl.program_id(0)
    d = x_ref.shape[-1]
    g = _gelu_tanh(d_ref[...] * u_ref[...] + ys_ref[...])
    oc = (g * jax.nn.sigmoid(_dot(g.astype(BF16), wglu_ref[...]) + bglu_ref[...])).astype(BF16)
    gate = gate_ref[...].astype(F32)
    mix = (gate[:, :d] * _dot(oa_ref[...], wa_ref[...]) + gate[:, d:2 * d] * _dot(ob_ref[...], wb_ref[...])
           + gate[:, 2 * d:] * _dot(oc, wc_ref[...]))
    x_new = x_ref[...] + _row_select(i, tm, n_lat, mod_ref, 2) * _dot(mix.astype(BF16), wo_ref[...])
    h2 = (_rms(x_new, g2_ref[...]) * (1.0 + _row_select(i, tm, n_lat, mod_ref, 4))
          + _row_select(i, tm, n_lat, mod_ref, 3))
    if not moe:
        xo_ref, h2_ref = rest
    else:
        wr_ref, xo_ref, h2_ref, comb_ref = rest
        logits = _dot3(h2, wr_ref[...])
        lane = lax.broadcasted_iota(jnp.int32, logits.shape, 1)
        logits = jnp.where(lane < N_EXPERTS, logits, NEG)
        m1 = logits.max(axis=-1, keepdims=True)
        i1 = jnp.where(logits == m1, lane, LANES).min(axis=-1, keepdims=True)
        rest_l = jnp.where(lane == i1, NEG, logits)
        m2 = rest_l.max(axis=-1, keepdims=True)
        i2 = jnp.where(rest_l == m2, lane, LANES).min(axis=-1, keepdims=True)
        e2 = jnp.exp(m2 - m1)
        w1 = 1.0 / (1.0 + e2)
        comb_ref[...] = jnp.where(lane == i1, w1, 0.0) + jnp.where(lane == i2, e2 * w1, 0.0)
    xo_ref[...] = x_new
    h2_ref[...] = h2.astype(BF16)


def _post_call(x, mods, o_a, o_b, ys, u, gate, lw, n_lat, moe):
    L, d = x.shape
    tm = _pick(L, (320, 256, 128))
    row = lambda w: pl.BlockSpec((tm, w), lambda i: (i, 0))
    weights = [lw[k] for k in ("wa", "wb", "wc", "wglu", "bglu", "d", "wo", "g2")]
    if moe:
        weights.append(lw["wrouter"])
    outs = [(d, F32), (d, BF16)] + ([(LANES, F32)] if moe else [])
    return pl.pallas_call(
        functools.partial(_post_kernel, n_lat, tm, moe),
        grid=(L // tm,),
        in_specs=([row(d), _const_spec(mods.shape), row(HEAD_PAD), row(NA_WIDTH), row(S5_WIDTH), row(S5_WIDTH),
                   row(N_BRANCH * d)] + [_const_spec(w.shape) for w in weights]),
        out_specs=[row(w) for w, _ in outs],
        out_shape=[jax.ShapeDtypeStruct((L, w), t) for w, t in outs],
        compiler_params=_params(("arbitrary",)),
        name="post_moe" if moe else "post",
    )(x, mods, o_a, o_b, ys, u, gate, *weights)


def _swiglu(h, wg_ref, wu_ref, wd_ref):
    a = _dot(h, wg_ref[...])
    act = (a * jax.nn.sigmoid(a) * _dot(h, wu_ref[...])).astype(BF16)
    return _dot(act, wd_ref[...])


def _ffn_kernel(n_lat, tm, x_ref, h_ref, mod_ref, wg_ref, wu_ref, wd_ref, o_ref):
    f = _swiglu(h_ref[...], wg_ref, wu_ref, wd_ref)
    o_ref[...] = x_ref[...] + _row_select(pl.program_id(0), tm, n_lat, mod_ref, 5) * f


def _ffn_call(x, h2, mods, wg, wu, wd, n_lat):
    L, d = x.shape
    tm = _pick(L, (640, 256, 128))
    row = lambda w: pl.BlockSpec((tm, w), lambda i: (i, 0))
    return pl.pallas_call(
        functools.partial(_ffn_kernel, n_lat, tm),
        grid=(L // tm,),
        in_specs=[row(d), row(d), _const_spec(mods.shape), _const_spec(wg.shape), _const_spec(wu.shape),
                  _const_spec(wd.shape)],
        out_specs=row(d),
        out_shape=jax.ShapeDtypeStruct((L, d), F32),
        compiler_params=_params(("arbitrary",)),
        name="ffn",
    )(x, h2, mods, wg, wu, wd)


def _moe_kernel(n_lat, tm, x_ref, h_ref, comb_ref, mod_ref, wg_ref, wu_ref, wd_ref, o_ref, acc_ref):
    e, f_blk = pl.program_id(1), pl.program_id(2)
    comb = comb_ref[...]
    lane = lax.broadcasted_iota(jnp.int32, comb.shape, 1)
    w_e = jnp.sum(jnp.where(lane == e, comb, 0.0), axis=-1, keepdims=True)
    f = w_e * _swiglu(h_ref[...], wg_ref, wu_ref, wd_ref)
    first = (e == 0) & (f_blk == 0)

    @pl.when(first)
    def _():
        acc_ref[...] = f

    @pl.when(jnp.logical_not(first))
    def _():
        acc_ref[...] += f

    @pl.when((e == pl.num_programs(1) - 1) & (f_blk == pl.num_programs(2) - 1))
    def _():
        o_ref[...] = x_ref[...] + _row_select(pl.program_id(0), tm, n_lat, mod_ref, 5) * acc_ref[...]


def _moe_call(x, h2, comb, mods, wg, wu, wd, n_lat):
    L, d = x.shape
    ne, _, dff = wg.shape
    tm = _pick(L, (640, 256, 128))
    tf = dff // 2
    row = lambda w: pl.BlockSpec((tm, w), lambda i, e, f: (i, 0))
    return pl.pallas_call(
        functools.partial(_moe_kernel, n_lat, tm),
        grid=(L // tm, ne, dff // tf),
        in_specs=[row(d), row(d), row(LANES), _const_spec(mods.shape),
                  pl.BlockSpec((None, d, tf), lambda i, e, f: (e, 0, f)),
                  pl.BlockSpec((None, d, tf), lambda i, e, f: (e, 0, f)),
                  pl.BlockSpec((None, tf, d), lambda i, e, f: (e, f, 0))],
        out_specs=row(d),
        out_shape=jax.ShapeDtypeStruct((L, d), F32),
        scratch_shapes=[pltpu.VMEM((tm, d), F32)],
        compiler_params=_params(("arbitrary", "arbitrary", "arbitrary")),
        name="moe",
    )(x, h2, comb, mods, wg, wu, wd)


def _rope_tables(n_lat, n_ctx):
    t = jnp.arange(n_lat, dtype=jnp.int32)
    n_freq = MLA_ROPE // 4
    inv = ROPE_THETA ** (-jnp.arange(n_freq, dtype=F32) / n_freq)
    ang = jnp.concatenate([(t // GRID_W).astype(F32)[:, None] * inv[None],
                           (t % GRID_W).astype(F32)[:, None] * inv[None]], axis=-1)
    cos, sin = jnp.cos(ang), jnp.sin(ang)
    ones = jnp.ones((n_lat, MLA_NOPE), F32)
    zn = jnp.zeros((n_lat, MLA_NOPE), F32)
    zh = jnp.zeros_like(sin)
    zp = jnp.zeros((n_lat, LANES - MLA_QK), F32)
    c = jnp.concatenate([ones, cos, cos, zp], axis=-1)
    s1 = jnp.concatenate([zn, -sin, zh, zp], axis=-1)
    s2 = jnp.concatenate([zn, zh, sin, zp], axis=-1)
    ctx = lambda a, fill: jnp.concatenate([a, jnp.full((n_ctx, LANES), fill, F32)], axis=0)
    return ctx(c, 1.0), ctx(s1, 0.0), ctx(s2, 0.0)


def _head_pad_cols(w, width):
    r = w.shape[0]
    w = w.reshape(r, MLA_HEADS, width)
    return jnp.pad(w, ((0, 0), (0, 0), (0, LANES - width))).reshape(r, HEAD_PAD)


def _layer_weights(i, p):
    d = D_MODEL
    w_in = p["w_in"][i]
    cuts = np.cumsum((0,) + IN_SPLITS)
    piece = lambda j: w_in[:, cuts[j]:cuts[j + 1]]
    bf = lambda w: w.astype(BF16)
    row = lambda v: v.reshape(1, -1).astype(F32)
    ukv = p["w_mla_ukv"][i].reshape(MLA_KV_RANK, MLA_HEADS, MLA_NOPE + MLA_V)
    wk_nope = _head_pad_cols(ukv[:, :, :MLA_NOPE].reshape(MLA_KV_RANK, -1), MLA_NOPE)
    kr_place = jnp.zeros((LANES, MLA_HEADS, LANES), F32)
    eye = jnp.eye(MLA_ROPE, dtype=F32)
    kr_place = kr_place.at[:MLA_ROPE, :, MLA_NOPE:MLA_QK].set(jnp.broadcast_to(eye[:, None, :], (MLA_ROPE, MLA_HEADS, MLA_ROPE)))
    vone = jnp.zeros((MLA_HEADS, LANES), F32).at[:, MLA_V].set(1.0).reshape(1, HEAD_PAD)
    head_gain = lambda g: jnp.tile(jnp.pad(g, (0, LANES - MLA_QK)), MLA_HEADS).reshape(1, HEAD_PAD)
    e64 = jnp.kron(jnp.eye(NA_HEADS, dtype=F32), jnp.ones((NA_DIM, NA_DIM), F32))
    wa = jnp.pad(p["w_br_mla"][i].reshape(MLA_HEADS, MLA_V, d), ((0, 0), (0, LANES - MLA_V), (0, 0)))
    return {
        "g1": row(p["g_norm1"][i]), "g2": row(p["g_norm2"][i]),
        "wq": bf(piece(0)), "wkv": bf(piece(1)),
        "wkr": bf(jnp.pad(piece(2), ((0, 0), (0, LANES - MLA_ROPE)))),
        "wnq": bf(piece(3)), "wnk": bf(piece(4)), "wnv": bf(piece(5)), "wu": bf(piece(6)), "wg": bf(piece(7)),
        "gq": row(p["g_mla_q"][i]), "gkv": row(p["g_mla_kv"][i]),
        "wuq": bf(_head_pad_cols(p["w_mla_uq"][i], MLA_QK)),
        "wk": bf(jnp.concatenate([wk_nope, kr_place.reshape(LANES, HEAD_PAD)], axis=0)),
        "wv": bf(_head_pad_cols(ukv[:, :, MLA_NOPE:].reshape(MLA_KV_RANK, -1), MLA_V)),
        "vone": vone,
        "gqn": head_gain(p["g_mla_qn"][i]), "gkn": head_gain(p["g_mla_kn"][i]),
        "e64": bf(e64),
        "gnq": jnp.tile(p["g_na_qn"][i], NA_HEADS).reshape(1, -1), "gnk": jnp.tile(p["g_na_kn"][i], NA_HEADS).reshape(1, -1),
        "wa": bf(wa.reshape(HEAD_PAD, d)), "wb": bf(p["w_br_na"][i]), "wc": bf(p["w_br_s5"][i]),
        "wglu": bf(p["w_glu"][i]), "bglu": row(p["b_glu"][i]), "d": row(p["s5_d"][i]), "wo": bf(p["w_out"][i]),
    }


def kernel(x, c, ctx, c_ctx, w_mod, b_mod, g_norm1, g_norm2, w_in, g_mla_q, g_mla_kv, w_mla_uq, w_mla_ukv,
           g_mla_qn, g_mla_kn, g_na_qn, g_na_kn, na_rpb, s5_a_re, s5_a_im, s5_log_dt, s5_b_re, s5_b_im,
           s5_c_re, s5_c_im, s5_d, w_glu, b_glu, w_br_mla, w_br_na, w_br_s5, w_out, w_ffn_gate, w_ffn_up,
           w_ffn_down, w_router, w_exp_gate, w_exp_up, w_exp_down):
    p = dict(w_in=w_in, g_norm1=g_norm1, g_norm2=g_norm2, g_mla_q=g_mla_q, g_mla_kv=g_mla_kv, w_mla_uq=w_mla_uq,
             w_mla_ukv=w_mla_ukv, g_mla_qn=g_mla_qn, g_mla_kn=g_mla_kn, g_na_qn=g_na_qn, g_na_kn=g_na_kn,
             s5_d=s5_d, w_glu=w_glu, b_glu=b_glu, w_br_mla=w_br_mla, w_br_na=w_br_na, w_br_s5=w_br_s5, w_out=w_out)
    assert x.shape[0] == 1 and x.shape[2] == D_MODEL
    n_lat, n_ctx = x.shape[1], ctx.shape[1]
    depth = w_mod.shape[0]
    xs = jnp.concatenate([x[0], ctx[0]], axis=0)

    cvec = jnp.zeros((8, D_MODEL), F32).at[0].set(c[0]).at[1].set(c_ctx)
    mods_all = _mod_call(cvec, w_mod, b_mod)[:, :2, None, :]
    rope = _rope_tables(n_lat, n_ctx)

    for i in range(depth):
        lw = _layer_weights(i, p)
        mods = mods_all[i]
        q, k, v, nq, nk, nv, u, gate = _pre_call(xs, mods, n_lat, lw, rope)

        o_a = _mla_call(q, k, v, n_lat)
        bias = _na_bias(na_rpb[i], n_lat // GRID_W)
        o_b = _na_call(nq, nk, nv, bias, n_lat, n_ctx)
        o_a_c, o_b_c = _ctx_call(q, k, v, nq, nk, nv, n_lat, n_ctx)
        o_a = lax.dynamic_update_slice(o_a, o_a_c, (n_lat, 0))
        o_b = lax.dynamic_update_slice(o_b, o_b_c, (n_lat, 0))

        ops = _s5prep_call(s5_a_re[i], s5_a_im[i], s5_log_dt[i], s5_b_re[i], s5_b_im[i], s5_c_re[i], s5_c_im[i])
        ys = _s5_mix(u, ops, n_lat, n_ctx)

        moe = i % 2 == 1
        j = i // 2
        if moe:
            lw["wrouter"] = jnp.pad(w_router[j], ((0, 0), (0, LANES - N_EXPERTS)))
            xs, h2, comb = _post_call(xs, mods, o_a, o_b, ys, u, gate, lw, n_lat, True)
            xs = _moe_call(xs, h2, comb, mods, w_exp_gate[j].astype(BF16), w_exp_up[j].astype(BF16),
                           w_exp_down[j].astype(BF16), n_lat)
        else:
            xs, h2 = _post_call(xs, mods, o_a, o_b, ys, u, gate, lw, n_lat, False)
            xs = _ffn_call(xs, h2, mods, w_ffn_gate[j].astype(BF16), w_ffn_up[j].astype(BF16),
                           w_ffn_down[j].astype(BF16), n_lat)
    return xs[:n_lat][None]
```

```python
import functools
import math

import jax
import jax.numpy as jnp
import numpy as np
from jax import lax
from jax.experimental import pallas as pl
from jax.experimental.pallas import tpu as pltpu

F32 = jnp.float32
BF16 = jnp.bfloat16

D_MODEL = 1024
DEPTH = 4
GRID_W = 64
EPS = 1e-6
ROPE_THETA = 10000.0

MLA_HEADS = 8
MLA_NOPE = 64
MLA_ROPE = 32
MLA_V = 64
MLA_Q_RANK = 384
MLA_KV_RANK = 256
MLA_QK = MLA_NOPE + MLA_ROPE

NA_HEADS = 8
NA_DIM = 64
NA_KH = 8
NA_KW = 16
NA_WIDTH = NA_HEADS * NA_DIM

S5_GROUPS = 32
S5_GCH = 16
S5_STATE = 64
S5_WIDTH = S5_GROUPS * S5_GCH

N_BRANCH = 3
D_FF = 2816
N_EXPERTS = 8
TOP_K = 2

IN_SPLITS = (MLA_Q_RANK, MLA_KV_RANK, MLA_ROPE, NA_WIDTH, NA_WIDTH, NA_WIDTH, S5_WIDTH, N_BRANCH * D_MODEL)

LANES = 128
HEAD_PAD = MLA_HEADS * LANES
S5_T = 16
NA_QROWS = 4
NA_WROWS = NA_QROWS + NA_KH - 1
VMEM_LIMIT = 56 * 1024 * 1024
NEG = -1e30


def _pick(n, candidates):
    for c in candidates:
        if n % c == 0:
            return c
    raise ValueError(f"no tile in {candidates} divides {n}")


def _const_spec(shape):
    nd = len(shape)
    return pl.BlockSpec(shape, lambda *_: (0,) * nd, pipeline_mode=pl.Buffered(1))


def _params(sem):
    return pltpu.CompilerParams(dimension_semantics=sem, vmem_limit_bytes=VMEM_LIMIT)


def _dot(a, b):
    return jnp.dot(a, b, preferred_element_type=F32)


def _dot_t(a, b):
    return lax.dot_general(a, b, (((1,), (1,)), ((), ())), preferred_element_type=F32)


def _split(x):
    hi = x.astype(BF16)
    lo = (x - hi.astype(F32)).astype(BF16)
    return hi, lo


def _dot3(a, b):
    a_hi, a_lo = _split(a)
    b_hi, b_lo = _split(b)
    return _dot(a_hi, b_hi) + _dot(a_hi, b_lo) + _dot(a_lo, b_hi)


def _dot3_t(a, b):
    a_hi, a_lo = _split(a)
    b_hi, b_lo = _split(b)
    return _dot_t(a_hi, b_hi) + _dot_t(a_hi, b_lo) + _dot_t(a_lo, b_hi)


def _rms(x, g):
    ms = jnp.mean(x * x, axis=-1, keepdims=True)
    return x * lax.rsqrt(ms + EPS) * g


def _row_select(tile, tm, n_lat, mod_ref, idx):
    rows = tile * tm + lax.broadcasted_iota(jnp.int32, (tm, 1), 0)
    lat = mod_ref[0, :, idx * D_MODEL:(idx + 1) * D_MODEL]
    ctx = mod_ref[1, :, idx * D_MODEL:(idx + 1) * D_MODEL]
    return jnp.where(rows < n_lat, lat, ctx)


def _mod_kernel(c_ref, w_ref, b_ref, o_ref):
    c = c_ref[...]
    a = c * jax.nn.sigmoid(c)
    o_ref[...] = _dot3(a, w_ref[...]) + b_ref[...]


def _mod_call(cvec, w_mod, b_mod):
    depth, d, n6 = w_mod.shape
    tn = _pick(n6, (1536, 1024, 512, 128))
    return pl.pallas_call(
        _mod_kernel,
        grid=(depth, n6 // tn),
        in_specs=[
            pl.BlockSpec((8, d), lambda l, j: (0, 0)),
            pl.BlockSpec((None, d, tn), lambda l, j: (l, 0, j)),
            pl.BlockSpec((None, 1, tn), lambda l, j: (l, 0, j)),
        ],
        out_specs=pl.BlockSpec((None, 8, tn), lambda l, j: (l, 0, j)),
        out_shape=jax.ShapeDtypeStruct((depth, 8, n6), F32),
        compiler_params=_params(("arbitrary", "arbitrary")),
        name="mod",
    )(cvec, w_mod, b_mod.reshape(depth, 1, n6))


def _head_norm_rope(xp, gain, c_t, s1_t, s2_t, scale, o_ref):
    for h in range(MLA_HEADS):
        sl = slice(h * LANES, (h + 1) * LANES)
        xh = xp[:, sl]
        ms = jnp.sum(xh * xh, axis=-1, keepdims=True) * (1.0 / MLA_QK)
        y = xh * lax.rsqrt(ms + EPS) * gain[:, sl]
        y = y * c_t + pltpu.roll(y, LANES - MLA_ROPE // 2, 1) * s1_t + pltpu.roll(y, MLA_ROPE // 2, 1) * s2_t
        if scale != 1.0:
            y = y * scale
        o_ref[:, sl] = y.astype(o_ref.dtype)


def _pre_kernel(n_lat, tm, mla_scale, na_scale,
                x_ref, mod_ref, g1_ref, wq_ref, wkv_ref, wkr_ref, wnq_ref, wnk_ref, wnv_ref, wu_ref, wg_ref,
                gq_ref, gkv_ref, wuq_ref, wk_ref, wv_ref, vone_ref, gqn_ref, gkn_ref,
                rc_ref, rs1_ref, rs2_ref, e64_ref, gnq_ref, gnk_ref,
                q_ref, k_ref, v_ref, nq_ref, nk_ref, nv_ref, u_ref, gate_ref):
    i = pl.program_id(0)
    sh = _row_select(i, tm, n_lat, mod_ref, 0)
    sc = _row_select(i, tm, n_lat, mod_ref, 1)
    h = (_rms(x_ref[...], g1_ref[...]) * (1.0 + sc) + sh).astype(BF16)

    c_t, s1_t, s2_t = rc_ref[...], rs1_ref[...], rs2_ref[...]

    rq = _rms(_dot(h, wq_ref[...]), gq_ref[...]).astype(BF16)
    _head_norm_rope(_dot(rq, wuq_ref[...]), gqn_ref[...], c_t, s1_t, s2_t, mla_scale, q_ref)
    rkv = _rms(_dot(h, wkv_ref[...]), gkv_ref[...]).astype(BF16)
    pkr = _dot(h, wkr_ref[...]).astype(BF16)
    kin = jnp.concatenate([rkv, pkr], axis=-1)
    _head_norm_rope(_dot(kin, wk_ref[...]), gkn_ref[...], c_t, s1_t, s2_t, 1.0, k_ref)
    v_ref[...] = (_dot(rkv, wv_ref[...]) + vone_ref[...]).astype(BF16)

    def na_norm(w_ref, g_ref, scale):
        p = _dot(h, w_ref[...])
        sq_hi, sq_lo = _split(p * p)
        ss = (_dot(sq_hi, e64_ref[...]) + _dot(sq_lo, e64_ref[...])) * (1.0 / NA_DIM)
        return (p * lax.rsqrt(ss + EPS) * (g_ref[...] * scale)).astype(BF16)

    nq_ref[...] = na_norm(wnq_ref, gnq_ref, na_scale)
    nk_ref[...] = na_norm(wnk_ref, gnk_ref, 1.0)
    nv_ref[...] = _dot(h, wnv_ref[...]).astype(BF16)

    u_ref[...] = _dot(h, wu_ref[...])
    gate_ref[...] = jax.nn.sigmoid(_dot(h, wg_ref[...])).astype(BF16)


def _pre_call(x, mods, n_lat, lw, rope):
    L, d = x.shape
    tm = _pick(L, (320, 256, 128))
    row = lambda w: pl.BlockSpec((tm, w), lambda i: (i, 0))
    weights = [lw[k] for k in ("g1", "wq", "wkv", "wkr", "wnq", "wnk", "wnv", "wu", "wg", "gq", "gkv", "wuq",
                               "wk", "wv", "vone", "gqn", "gkn")]
    tail = [lw["e64"], lw["gnq"], lw["gnk"]]
    in_specs = ([row(d), _const_spec(mods.shape)] + [_const_spec(w.shape) for w in weights]
                + [row(LANES)] * 3 + [_const_spec(w.shape) for w in tail])
    outs = [(HEAD_PAD, BF16)] * 3 + [(NA_WIDTH, BF16)] * 3 + [(S5_WIDTH, F32), (N_BRANCH * d, BF16)]
    return pl.pallas_call(
        functools.partial(_pre_kernel, n_lat, tm, MLA_QK ** -0.5 * math.log2(math.e), NA_DIM ** -0.5),
        grid=(L // tm,),
        in_specs=in_specs,
        out_specs=[row(w) for w, _ in outs],
        out_shape=[jax.ShapeDtypeStruct((L, w), t) for w, t in outs],
        compiler_params=_params(("arbitrary",)),
        name="pre",
    )(x, mods, *weights, *rope, *tail)


def _mla_kernel(tq, tk, n_chunks, q_ref, k_ref, v_ref, o_ref, m_ref, acc_ref, sa_ref, sb_ref):
    m_ref[...] = jnp.full_like(m_ref, NEG)
    acc_ref[...] = jnp.zeros_like(acc_ref)
    q = q_ref[...]

    def scores(c):
        off = pl.multiple_of(c * tk, tk)
        return _dot_t(q, k_ref[pl.ds(off, tk), :])

    def absorb(s, c):
        off = pl.multiple_of(c * tk, tk)
        m_prev = m_ref[...]
        m_new = jnp.maximum(m_prev, jnp.max(s, axis=-1, keepdims=True))
        p = jnp.exp2(s - jnp.concatenate([m_new] * (tk // LANES), axis=1))
        acc_ref[...] = jnp.exp2(m_prev - m_new) * acc_ref[...] + _dot(p.astype(BF16), v_ref[pl.ds(off, tk), :])
        m_ref[...] = m_new

    sa_ref[...] = scores(0)

    def body(i, carry):
        c = 2 * i
        sb_ref[...] = scores(c + 1)
        absorb(sa_ref[...], c)
        sa_ref[...] = scores(c + 2)
        absorb(sb_ref[...], c + 1)
        return carry

    lax.fori_loop(0, (n_chunks - 1) // 2, body, 0)
    if n_chunks % 2 == 0:
        sb_ref[...] = scores(n_chunks - 1)
        absorb(sa_ref[...], n_chunks - 2)
        absorb(sb_ref[...], n_chunks - 1)
    else:
        absorb(sa_ref[...], n_chunks - 1)
    acc = acc_ref[...]
    o_ref[...] = (acc * pl.reciprocal(acc[:, MLA_V:MLA_V + 1], approx=False)).astype(o_ref.dtype)


def _mla_call(q, k, v, n_lat):
    L = q.shape[0]
    tq = _pick(n_lat, (512, 256))
    tk = _pick(L, (1280, 768, 512, 256))
    kv_spec = pl.BlockSpec((L, LANES), lambda h, i: (0, h))
    return pl.pallas_call(
        functools.partial(_mla_kernel, tq, tk, L // tk),
        grid=(MLA_HEADS, n_lat // tq),
        in_specs=[pl.BlockSpec((tq, LANES), lambda h, i: (i, h)), kv_spec, kv_spec],
        out_specs=pl.BlockSpec((tq, LANES), lambda h, i: (i, h)),
        out_shape=jax.ShapeDtypeStruct((n_lat, HEAD_PAD), BF16),
        scratch_shapes=[pltpu.VMEM((tq, LANES), F32), pltpu.VMEM((tq, LANES), F32),
                        pltpu.VMEM((tq, tk), F32), pltpu.VMEM((tq, tk), F32)],
        compiler_params=_params(("arbitrary", "arbitrary")),
        name="mla",
    )(q, k, v)


def _softmax_pv(parts):
    m = parts[0][0].max(axis=-1, keepdims=True)
    for s, _ in parts[1:]:
        m = jnp.maximum(m, s.max(axis=-1, keepdims=True))
    num, den = 0.0, 0.0
    for s, v in parts:
        p = jnp.exp(s - m)
        den = den + p.sum(axis=-1, keepdims=True)
        num = num + _dot(p.astype(BF16), v)
    return num * pl.reciprocal(den, approx=False)


def _na_kernel(n_lat, n_ctx, n_blocks, q_ref, k_ref, v_ref, bias_ref, o_ref):
    b = pl.program_id(1)
    wtok = NA_WROWS * GRID_W
    row0 = jnp.clip(b * NA_QROWS - NA_KH // 2, 0, n_lat // GRID_W - NA_WROWS)
    off = pl.multiple_of(row0 * GRID_W, GRID_W)
    q = q_ref[...]
    kw, vw = k_ref[pl.ds(off, wtok), :], v_ref[pl.ds(off, wtok), :]
    kc, vc = k_ref[n_lat:n_lat + n_ctx, :], v_ref[n_lat:n_lat + n_ctx, :]
    outs = []
    for j in range(LANES // NA_DIM):
        sl = slice(j * NA_DIM, (j + 1) * NA_DIM)
        qh = q[:, sl]
        s_win = _dot_t(qh, kw[:, sl]) + bias_ref[j]
        s_ctx = _dot_t(qh, kc[:, sl])
        outs.append(_softmax_pv([(s_ctx, vc[:, sl]), (s_win, vw[:, sl])]))
    o_ref[...] = jnp.concatenate(outs, axis=-1).astype(o_ref.dtype)


def _na_bias(rpb, n_rows):
    n_blocks = n_rows // NA_QROWS
    cols = np.arange(GRID_W)
    c0 = np.clip(cols - NA_KW // 2, 0, GRID_W - NA_KW)
    col_ok = (cols[None, :] >= c0[:, None]) & (cols[None, :] < c0[:, None] + NA_KW)
    dc = cols[None, :] - cols[:, None] + NA_KW - 1
    pick_c = (col_ok[:, :, None] & (dc[:, :, None] == np.arange(2 * NA_KW - 1))).astype(np.float32)
    pick_r = np.zeros((3, NA_QROWS, NA_WROWS, 2 * NA_KH - 1), np.float32)
    row_oks = []
    for v, b in enumerate((0, 1, n_blocks - 1)):
        row0 = min(max(b * NA_QROWS - NA_KH // 2, 0), n_rows - NA_WROWS)
        qr = b * NA_QROWS + np.arange(NA_QROWS)
        kr = row0 + np.arange(NA_WROWS)
        r0 = np.clip(qr - NA_KH // 2, 0, n_rows - NA_KH)
        row_ok = (kr[None, :] >= r0[:, None]) & (kr[None, :] < r0[:, None] + NA_KH)
        dr = kr[None, :] - qr[:, None] + NA_KH - 1
        pick_r[v] = row_ok[:, :, None] & (dr[:, :, None] == np.arange(2 * NA_KH - 1))
        row_oks.append(row_ok)
    hi = lax.Precision.HIGHEST
    by_col = jnp.einsum("hrd,qkd->hrqk", rpb.astype(F32), pick_c, precision=hi)
    vals = jnp.einsum("vabr,hrqk->vhaqbk", pick_r, by_col, precision=hi)
    ok = np.stack(row_oks)[:, None, :, None, :, None] & col_ok[None, None, None, :, None, :]
    vals = jnp.where(ok, vals, NEG)
    return vals.reshape(3, rpb.shape[0], NA_QROWS * GRID_W, NA_WROWS * GRID_W)


def _na_call(nq, nk, nv, bias, n_lat, n_ctx):
    L = nq.shape[0]
    tq = NA_QROWS * GRID_W
    n_blocks = n_lat // tq
    hp = LANES // NA_DIM
    kv_spec = pl.BlockSpec((L, LANES), lambda h, b: (0, h))

    def bias_map(h, b):
        return (jnp.where(b == 0, 0, jnp.where(b == n_blocks - 1, 2, 1)), h, 0, 0)

    return pl.pallas_call(
        functools.partial(_na_kernel, n_lat, n_ctx, n_blocks),
        grid=(NA_HEADS // hp, n_blocks),
        in_specs=[pl.BlockSpec((tq, LANES), lambda h, b: (b, h)), kv_spec, kv_spec,
                  pl.BlockSpec((None, hp, tq, NA_WROWS * GRID_W), bias_map)],
        out_specs=pl.BlockSpec((tq, LANES), lambda h, b: (b, h)),
        out_shape=jax.ShapeDtypeStruct((n_lat, NA_WIDTH), BF16),
        compiler_params=_params(("arbitrary", "arbitrary")),
        name="na",
    )(nq, nk, nv, bias)


def _ctx_kernel(q_ref, k_ref, v_ref, nq_ref, nk_ref, nv_ref, oa_ref, ob_ref):
    for h in range(MLA_HEADS):
        sl = slice(h * LANES, (h + 1) * LANES)
        s = _dot_t(q_ref[:, sl], k_ref[:, sl])
        p = jnp.exp2(s - s.max(axis=-1, keepdims=True))
        acc = _dot(p.astype(BF16), v_ref[:, sl])
        oa_ref[:, sl] = (acc * pl.reciprocal(acc[:, MLA_V:MLA_V + 1], approx=False)).astype(oa_ref.dtype)
    nq, nk, nv = nq_ref[...], nk_ref[...], nv_ref[...]
    outs = []
    for h in range(NA_HEADS):
        sl = slice(h * NA_DIM, (h + 1) * NA_DIM)
        outs.append(_softmax_pv([(_dot_t(nq[:, sl], nk[:, sl]), nv[:, sl])]))
    ob_ref[...] = jnp.concatenate(outs, axis=-1).astype(ob_ref.dtype)


def _ctx_call(q, k, v, nq, nk, nv, n_lat, n_ctx):
    blk = n_lat // n_ctx
    a_spec = pl.BlockSpec((n_ctx, HEAD_PAD), lambda i: (blk, 0))
    b_spec = pl.BlockSpec((n_ctx, NA_WIDTH), lambda i: (blk, 0))
    return pl.pallas_call(
        _ctx_kernel,
        grid=(1,),
        in_specs=[a_spec] * 3 + [b_spec] * 3,
        out_specs=[pl.BlockSpec((n_ctx, HEAD_PAD), lambda i: (0, 0)),
                   pl.BlockSpec((n_ctx, NA_WIDTH), lambda i: (0, 0))],
        out_shape=[jax.ShapeDtypeStruct((n_ctx, HEAD_PAD), BF16), jax.ShapeDtypeStruct((n_ctx, NA_WIDTH), BF16)],
        compiler_params=_params(("arbitrary",)),
        name="ctx_attn",
    )(q, k, v, nq, nk, nv)


S5_SG = LANES // S5_GCH
S5_SW = S5_SG * S5_STATE


def _s5prep_kernel(are_ref, aim_ref, ldt_ref, bre_ref, bim_ref, cre_ref, cim_ref,
                   arec_ref, aimc_ref, ldtc_ref, crec_ref, cimc_ref,
                   d_ref, bbr_ref, bbi_ref, wcr_ref, wci_ref, atr_ref, ati_ref):
    for ref in (d_ref, bbr_ref, bbi_ref, wcr_ref, wci_ref):
        ref[...] = jnp.zeros_like(ref)
    for g in range(S5_SG):
        rows, lanes = slice(g * S5_GCH, (g + 1) * S5_GCH), slice(g * S5_STATE, (g + 1) * S5_STATE)
        a_re, a_im = are_ref[g], aim_ref[g]
        dt = jnp.exp(ldt_ref[g])
        steps = lax.broadcasted_iota(jnp.int32, (S5_T + 1, S5_STATE), 0).astype(F32)
        mag = jnp.exp(a_re * dt * steps)
        p_re, p_im = mag * jnp.cos(a_im * dt * steps), mag * jnp.sin(a_im * dt * steps)
        nr, ni = p_re[1:2] - 1.0, p_im[1:2]
        den = 1.0 / (a_re * a_re + a_im * a_im)
        f_re, f_im = (nr * a_re + ni * a_im) * den, (ni * a_re - nr * a_im) * den
        b_re, b_im = bre_ref[g], bim_ref[g]
        bb_re, bb_im = f_re * b_re - f_im * b_im, f_re * b_im + f_im * b_re
        c_re, c_im = cre_ref[g], cim_ref[g]
        pw = [(p_re[e:e + 1], p_im[e:e + 1]) for e in range(S5_T + 1)]
        w_re = jnp.concatenate([c_re * r - c_im * i for r, i in pw[:S5_T]], axis=0)
        w_im = jnp.concatenate([c_re * i + c_im * r for r, i in pw[:S5_T]], axis=0)
        kt = _dot3_t(bb_re, w_re) - _dot3_t(bb_im, w_im)
        for e in range(S5_T):
            r, i = pw[e]
            d_ref[e, rows, rows] = kt[:, e * S5_GCH:(e + 1) * S5_GCH].astype(d_ref.dtype)
            bbr_ref[e, rows, lanes] = (r * bb_re - i * bb_im).astype(bbr_ref.dtype)
            bbi_ref[e, rows, lanes] = (r * bb_im + i * bb_re).astype(bbi_ref.dtype)
        atr_ref[:, lanes], ati_ref[:, lanes] = pw[S5_T]
        ac_re, ac_im = arec_ref[g], aimc_ref[g]
        dtc = jnp.exp(ldtc_ref[g])
        steps_c = lax.broadcasted_iota(jnp.int32, (S5_STATE, S5_T), 1).astype(F32) + 1.0
        mag_c = jnp.exp(ac_re * dtc * steps_c)
        q_re, q_im = mag_c * jnp.cos(ac_im * dtc * steps_c), mag_c * jnp.sin(ac_im * dtc * steps_c)
        ct_re, ct_im = crec_ref[g], cimc_ref[g]
        for e in range(S5_T):
            r, i = q_re[:, e:e + 1], q_im[:, e:e + 1]
            wcr_ref[e, lanes, rows] = (ct_re * r - ct_im * i).astype(wcr_ref.dtype)
            wci_ref[e, lanes, rows] = (-(ct_re * i + ct_im * r)).astype(wci_ref.dtype)


def _s5prep_call(a_re, a_im, log_dt, b_re, b_im, c_re, c_im):
    nd, g, p = a_re.shape
    cg = b_re.shape[-1]
    ns = g // S5_SG
    spec = lambda *s: pl.BlockSpec((None, S5_SG) + s, lambda d, j: (d, j) + (0,) * len(s))
    out = lambda *s: pl.BlockSpec((None, None) + s, lambda d, j: (d, j) + (0,) * len(s))
    sds = lambda t, *s: jax.ShapeDtypeStruct((nd, ns) + s, t)
    tr = lambda t: jnp.swapaxes(t, -1, -2)
    ldt = jnp.broadcast_to(log_dt[:, :, None, None], (nd, g, 1, p))
    return pl.pallas_call(
        _s5prep_kernel,
        grid=(nd, ns),
        in_specs=[spec(1, p)] * 3 + [spec(cg, p)] * 4 + [spec(p, 1)] * 3 + [spec(p, cg)] * 2,
        out_specs=[out(S5_T, LANES, LANES), out(S5_T, LANES, S5_SW), out(S5_T, LANES, S5_SW),
                   out(S5_T, S5_SW, LANES), out(S5_T, S5_SW, LANES), out(1, S5_SW), out(1, S5_SW)],
        out_shape=[sds(BF16, S5_T, LANES, LANES), sds(BF16, S5_T, LANES, S5_SW), sds(BF16, S5_T, LANES, S5_SW),
                   sds(BF16, S5_T, S5_SW, LANES), sds(BF16, S5_T, S5_SW, LANES), sds(F32, 1, S5_SW),
                   sds(F32, 1, S5_SW)],
        compiler_params=_params(("arbitrary", "arbitrary")),
        name="s5_prep",
    )(a_re[:, :, None, :], a_im[:, :, None, :], ldt, tr(b_re), tr(b_im), c_re, c_im,
      a_re[:, :, :, None], a_im[:, :, :, None], tr(ldt), tr(c_re), tr(c_im))


def _chunk_rows(u_ref, kb):
    return [u_ref[pl.ds(t, kb, stride=S5_T), :].astype(BF16) for t in range(S5_T)]


def _s5a_kernel(kb, u_ref, br_ref, bi_ref, ore_ref, oim_ref):
    x = _chunk_rows(u_ref, kb)
    for d in range(br_ref.shape[0]):
        ahead = lambda t: S5_T - 1 - t if d == 0 else t
        ore_ref[d] = sum(_dot(x[t], br_ref[d, ahead(t)]) for t in range(S5_T))
        oim_ref[d] = sum(_dot(x[t], bi_ref[d, ahead(t)]) for t in range(S5_T))


def _s5_blocks(L):
    nch = L // S5_T
    return nch, _pick(nch, (208, 144, 80, 72, 40, 16, 8))


def _s5a_call(u, bcr, bci):
    L = u.shape[0]
    nd, ns, _, n, ws = bcr.shape
    nch, kb = _s5_blocks(L)
    wspec = pl.BlockSpec((nd, None, S5_T, n, ws), lambda s, r: (0, s, 0, 0, 0))
    ospec = pl.BlockSpec((nd, kb, ws), lambda s, r: (0, r, s))
    return pl.pallas_call(
        functools.partial(_s5a_kernel, kb),
        grid=(ns, nch // kb),
        in_specs=[pl.BlockSpec((kb * S5_T, LANES), lambda s, r: (r, s)), wspec, wspec],
        out_specs=[ospec, ospec],
        out_shape=[jax.ShapeDtypeStruct((nd, nch, ns * ws), F32)] * 2,
        compiler_params=_params(("arbitrary", "arbitrary")),
        name="s5_chunk_inputs",
    )(u, bcr, bci)


def _s5scan_kernel(n_lat_ch, n_ctx_ch, bre_ref, bim_ref, atr_ref, ati_ref, sre_ref, sim_ref):
    d = pl.program_id(0)
    a_re, a_im = atr_ref[...], ati_ref[...]

    def segment(base, count, carry):
        def step(i, st):
            s_re, s_im = st
            k = base + jnp.where(d == 0, i, count - 1 - i)
            sre_ref[k] = s_re
            sim_ref[k] = s_im
            return (a_re * s_re - a_im * s_im + bre_ref[k], a_re * s_im + a_im * s_re + bim_ref[k])
        return lax.fori_loop(0, count, step, carry)

    zero = jnp.zeros(a_re.shape, F32)
    carry = segment(n_lat_ch, n_ctx_ch, (zero, zero))
    segment(0, n_lat_ch, carry)


def _s5scan_call(b_re, b_im, atr, ati, n_lat_ch, n_ctx_ch):
    nd, nch, w = b_re.shape
    shp = (nd, nch, w // LANES, LANES)
    spec = pl.BlockSpec((None, nch, 8, LANES), lambda d, j: (d, 0, j, 0))
    aspec = pl.BlockSpec((None, 8, LANES), lambda d, j: (d, j, 0))
    return pl.pallas_call(
        functools.partial(_s5scan_kernel, n_lat_ch, n_ctx_ch),
        grid=(nd, w // LANES // 8),
        in_specs=[spec, spec, aspec, aspec],
        out_specs=[spec, spec],
        out_shape=[jax.ShapeDtypeStruct(shp, F32)] * 2,
        compiler_params=_params(("arbitrary", "arbitrary")),
        name="s5_scan",
    )(b_re.reshape(shp), b_im.reshape(shp), atr.reshape(nd, w // LANES, LANES), ati.reshape(nd, w // LANES, LANES))


def _s5c_kernel(kb, u_ref, sre_ref, sim_ref, d_ref, wr_ref, wi_ref, y_ref, acc_ref):
    nd = d_ref.shape[0]
    x = jnp.concatenate(_chunk_rows(u_ref, kb), axis=0)
    s_re = [sre_ref[d].astype(BF16) for d in range(nd)]
    s_im = [sim_ref[d].astype(BF16) for d in range(nd)]
    for t in range(S5_T):
        later = lambda d: t if d == 0 else S5_T - 1 - t
        acc_ref[t * kb:(t + 1) * kb, :] = sum(
            _dot(s_re[d], wr_ref[d, later(d)]) + _dot(s_im[d], wi_ref[d, later(d)]) for d in range(nd))
    for e in range(S5_T):
        n = (S5_T - e) * kb
        acc_ref[e * kb:, :] += _dot(x[:n], d_ref[0, e])
        acc_ref[:n, :] += _dot(x[e * kb:], d_ref[1, e])
    for t in range(S5_T):
        y_ref[pl.ds(t, kb, stride=S5_T), :] = acc_ref[t * kb:(t + 1) * kb, :]


def _s5c_call(u, s_re, s_im, dm, wcr, wci):
    L = u.shape[0]
    nd, ns = dm.shape[:2]
    ws = wcr.shape[3]
    nch, kb = _s5_blocks(L)
    uspec = pl.BlockSpec((kb * S5_T, LANES), lambda s, r: (r, s))
    sspec = pl.BlockSpec((nd, kb, ws), lambda s, r: (0, r, s))
    wspec = pl.BlockSpec((nd, None, S5_T, ws, LANES), lambda s, r: (0, s, 0, 0, 0))
    return pl.pallas_call(
        functools.partial(_s5c_kernel, kb),
        grid=(ns, nch // kb),
        in_specs=[uspec, sspec, sspec,
                  pl.BlockSpec((nd, None, S5_T, LANES, LANES), lambda s, r: (0, s, 0, 0, 0)), wspec, wspec],
        out_specs=uspec,
        out_shape=jax.ShapeDtypeStruct(u.shape, F32),
        scratch_shapes=[pltpu.VMEM((kb * S5_T, LANES), F32)],
        compiler_params=_params(("arbitrary", "arbitrary")),
        name="s5_outputs",
    )(u, s_re, s_im, dm, wcr, wci)


def _s5_mix(u, ops, n_lat, n_ctx):
    dm, bcr, bci, wcr, wci, atr, ati = ops
    nd = dm.shape[0]
    b_re, b_im = _s5a_call(u, bcr, bci)
    s_re, s_im = _s5scan_call(b_re, b_im, atr.reshape(nd, -1), ati.reshape(nd, -1), n_lat // S5_T, n_ctx // S5_T)
    return _s5c_call(u, s_re.reshape(b_re.shape), s_im.reshape(b_im.shape), dm, wcr, wci)


def _gelu_tanh(x):
    return 0.5 * x * (1.0 + jnp.tanh(math.sqrt(2.0 / math.pi) * (x + 0.044715 * (x * x * x))))


def _post_kernel(n_lat, tm, moe,
                 x_ref, mod_ref, oa_ref, ob_ref, ys_ref, u_ref, gate_ref,
                 wa_ref, wb_ref, wc_ref, wglu_ref, bglu_ref, d_ref, wo_ref, g2_ref, *rest):
    i = pl.program_id(0)
    d = x_ref.shape[-1]
    g = _gelu_tanh(d_ref[...] * u_ref[...] + ys_ref[...])
    oc = (g * jax.nn.sigmoid(_dot(g.astype(BF16), wglu_ref[...]) + bglu_ref[...])).astype(BF16)
    gate = gate_ref[...].astype(F32)
    mix = (gate[:, :d] * _dot(oa_ref[...], wa_ref[...]) + gate[:, d:2 * d] * _dot(ob_ref[...], wb_ref[...])
           + gate[:, 2 * d:] * _dot(oc, wc_ref[...]))
    x_new = x_ref[...] + _row_select(i, tm, n_lat, mod_ref, 2) * _dot(mix.astype(BF16), wo_ref[...])
    h2 = (_rms(x_new, g2_ref[...]) * (1.0 + _row_select(i, tm, n_lat, mod_ref, 4))
          + _row_select(i, tm, n_lat, mod_ref, 3))
    if not moe:
        xo_ref, h2_ref = rest
    else:
        wr_ref, tri_ref, xo_ref, h2_ref, route_ref, count_ref, carry_ref = rest
        logits = _dot3(h2, wr_ref[...])
        lane = lax.broadcasted_iota(jnp.int32, logits.shape, 1)
        logits = jnp.where(lane < N_EXPERTS, logits, NEG)
        m1 = logits.max(axis=-1, keepdims=True)
        i1 = jnp.where(logits == m1, lane, LANES).min(axis=-1, keepdims=True)
        rest_l = jnp.where(lane == i1, NEG, logits)
        m2 = rest_l.max(axis=-1, keepdims=True)
        i2 = jnp.where(rest_l == m2, lane, LANES).min(axis=-1, keepdims=True)
        e2 = jnp.exp(m2 - m1)
        w1 = 1.0 / (1.0 + e2)

        @pl.when(i == 0)
        def _():
            carry_ref[...] = jnp.zeros_like(carry_ref)

        chosen = jnp.where((lane == i1) | (lane == i2), 1.0, 0.0)
        before = _dot(tri_ref[...], chosen.astype(BF16)) + carry_ref[...]
        r1 = jnp.sum(jnp.where(lane == i1, before, 0.0), axis=-1, keepdims=True)
        r2 = jnp.sum(jnp.where(lane == i2, before, 0.0), axis=-1, keepdims=True)
        total = carry_ref[...] + jnp.sum(chosen, axis=0, keepdims=True)
        carry_ref[...] = total
        count_ref[...] = jnp.broadcast_to(total, count_ref.shape)
        cols = [i1.astype(F32), i2.astype(F32), r1, r2, w1, e2 * w1]
        route = jnp.zeros(logits.shape, F32)
        for j, col in enumerate(cols):
            route = jnp.where(lane == j, col, route)
        route_ref[...] = route
    xo_ref[...] = x_new
    h2_ref[...] = h2.astype(h2_ref.dtype)


def _post_call(x, mods, o_a, o_b, ys, u, gate, lw, n_lat, moe):
    L, d = x.shape
    tm = _pick(L, (320, 256, 128))
    row = lambda w: pl.BlockSpec((tm, w), lambda i: (i, 0))
    weights = [lw[k] for k in ("wa", "wb", "wc", "wglu", "bglu", "d", "wo", "g2")]
    if moe:
        tri = jnp.asarray(np.tril(np.ones((tm, tm), np.float32), -1), BF16)
        weights += [lw["wrouter"], tri]
        outs = [(L, d, F32), (L, d, F32), (L, LANES, F32), (8, LANES, F32)]
        out_specs = [row(d), row(d), row(LANES), pl.BlockSpec((8, LANES), lambda i: (0, 0))]
        scratch = [pltpu.VMEM((1, LANES), F32)]
    else:
        outs = [(L, d, F32), (L, d, BF16)]
        out_specs = [row(d), row(d)]
        scratch = []
    return pl.pallas_call(
        functools.partial(_post_kernel, n_lat, tm, moe),
        grid=(L // tm,),
        in_specs=([row(d), _const_spec(mods.shape), row(HEAD_PAD), row(NA_WIDTH), row(S5_WIDTH), row(S5_WIDTH),
                   row(N_BRANCH * d)] + [_const_spec(w.shape) for w in weights]),
        out_specs=out_specs,
        out_shape=[jax.ShapeDtypeStruct((r, w), t) for r, w, t in outs],
        scratch_shapes=scratch,
        compiler_params=_params(("arbitrary",)),
        name="post_moe" if moe else "post",
    )(x, mods, o_a, o_b, ys, u, gate, *weights)


def _swiglu(h, wg_ref, wu_ref, wd_ref):
    a = _dot(h, wg_ref[...])
    act = (a * jax.nn.sigmoid(a) * _dot(h, wu_ref[...])).astype(BF16)
    return _dot(act, wd_ref[...])


def _ffn_kernel(n_lat, tm, x_ref, h_ref, mod_ref, wg_ref, wu_ref, wd_ref, o_ref):
    f = _swiglu(h_ref[...], wg_ref, wu_ref, wd_ref)
    o_ref[...] = x_ref[...] + _row_select(pl.program_id(0), tm, n_lat, mod_ref, 5) * f


def _ffn_call(x, h2, mods, wg, wu, wd, n_lat):
    L, d = x.shape
    tm = _pick(L, (640, 256, 128))
    row = lambda w: pl.BlockSpec((tm, w), lambda i: (i, 0))
    return pl.pallas_call(
        functools.partial(_ffn_kernel, n_lat, tm),
        grid=(L // tm,),
        in_specs=[row(d), row(d), _const_spec(mods.shape), _const_spec(wg.shape), _const_spec(wu.shape),
                  _const_spec(wd.shape)],
        out_specs=row(d),
        out_shape=jax.ShapeDtypeStruct((L, d), F32),
        compiler_params=_params(("arbitrary",)),
        name="ffn",
    )(x, h2, mods, wg, wu, wd)


MOE_TS = 512


def _row_copy(src, i, dst, j, sem):
    return pltpu.make_async_copy(src.at[pl.ds(i, 1)], dst.at[pl.ds(j, 1)], sem)


def _dispatch_kernel(tb, n_tok, slot_ref, h_hbm, xs_in, xs_hbm, sem):
    del xs_in
    base = pl.program_id(0) * tb

    def issue(t, carry):
        for k in range(TOP_K):
            _row_copy(h_hbm, base + t, xs_hbm, slot_ref[k * n_tok + base + t], sem).start()
        return carry

    def drain(t, carry):
        for k in range(TOP_K):
            _row_copy(h_hbm, 0, xs_hbm, 0, sem).wait()
        return carry

    lax.fori_loop(0, tb, issue, 0)
    lax.fori_loop(0, tb, drain, 0)


def _dispatch_call(slots, h2, n_slots):
    L, d = h2.shape
    tb = _pick(L, (640, 256, 128))
    xs0 = jnp.zeros((n_slots, d), h2.dtype)
    return pl.pallas_call(
        functools.partial(_dispatch_kernel, tb, L),
        grid_spec=pltpu.PrefetchScalarGridSpec(
            num_scalar_prefetch=1, grid=(L // tb,),
            in_specs=[pl.BlockSpec(memory_space=pl.ANY), pl.BlockSpec(memory_space=pl.ANY)],
            out_specs=pl.BlockSpec(memory_space=pl.ANY),
            scratch_shapes=[pltpu.SemaphoreType.DMA(())]),
        out_shape=jax.ShapeDtypeStruct((n_slots, d), h2.dtype),
        input_output_aliases={2: 0},
        compiler_params=_params(("arbitrary",)),
        name="moe_dispatch",
    )(slots, h2, xs0)


def _expert_kernel(te_ref, nu_ref, x_ref, wg_ref, wu_ref, wd_ref, o_ref):
    used = pl.program_id(0) < nu_ref[0]

    @pl.when(used)
    def _():
        o_ref[...] = _swiglu(x_ref[...].astype(BF16), wg_ref, wu_ref, wd_ref)

    @pl.when(jnp.logical_not(used))
    def _():
        o_ref[...] = jnp.zeros_like(o_ref)


def _expert_call(tile_expert, n_used, xs, wg, wu, wd):
    n_slots, d = xs.shape
    dff = wg.shape[-1]
    once = pl.Buffered(1)
    return pl.pallas_call(
        _expert_kernel,
        grid_spec=pltpu.PrefetchScalarGridSpec(
            num_scalar_prefetch=2, grid=(n_slots // MOE_TS,),
            in_specs=[pl.BlockSpec((MOE_TS, d), lambda j, te, nu: (j, 0)),
                      pl.BlockSpec((None, d, dff), lambda j, te, nu: (te[j], 0, 0), pipeline_mode=once),
                      pl.BlockSpec((None, d, dff), lambda j, te, nu: (te[j], 0, 0), pipeline_mode=once),
                      pl.BlockSpec((None, dff, d), lambda j, te, nu: (te[j], 0, 0), pipeline_mode=once)],
            out_specs=pl.BlockSpec((MOE_TS, d), lambda j, te, nu: (j, 0))),
        out_shape=jax.ShapeDtypeStruct((n_slots, d), F32),
        compiler_params=_params(("arbitrary",)),
        name="moe_experts",
    )(tile_expert, n_used, xs, wg, wu, wd)


def _combine_kernel(n_lat, tb, n_tok, slot_ref, x_ref, route_ref, mod_ref, zs_hbm, o_ref, g_ref, sem):
    i = pl.program_id(0)
    base = i * tb

    def issue(t, carry):
        for k in range(TOP_K):
            _row_copy(zs_hbm, slot_ref[k * n_tok + base + t], g_ref.at[k], t, sem).start()
        return carry

    def drain(t, carry):
        for k in range(TOP_K):
            _row_copy(zs_hbm, 0, g_ref.at[k], 0, sem).wait()
        return carry

    lax.fori_loop(0, tb, issue, 0)
    lax.fori_loop(0, tb, drain, 0)
    route = route_ref[...]
    f = route[:, 4:5] * g_ref[0] + route[:, 5:6] * g_ref[1]
    o_ref[...] = x_ref[...] + _row_select(i, tb, n_lat, mod_ref, 5) * f


def _combine_call(slots, x, route, mods, zs, n_lat):
    L, d = x.shape
    tb = _pick(L, (320, 256, 128))
    row = lambda w: pl.BlockSpec((tb, w), lambda i, s: (i, 0))
    return pl.pallas_call(
        functools.partial(_combine_kernel, n_lat, tb, L),
        grid_spec=pltpu.PrefetchScalarGridSpec(
            num_scalar_prefetch=1, grid=(L // tb,),
            in_specs=[row(d), row(LANES), pl.BlockSpec(mods.shape, lambda i, s: (0, 0, 0)),
                      pl.BlockSpec(memory_space=pl.ANY)],
            out_specs=row(d),
            scratch_shapes=[pltpu.VMEM((TOP_K, tb, d), F32), pltpu.SemaphoreType.DMA(())]),
        out_shape=jax.ShapeDtypeStruct((L, d), F32),
        compiler_params=_params(("arbitrary",)),
        name="moe_combine",
    )(slots, x, route, mods, zs)


def _moe_call(x, h2, route, counts, mods, wg, wu, wd, n_lat):
    L, d = x.shape
    ne = wg.shape[0]
    n_slots = (pl.cdiv(TOP_K * L, MOE_TS) + ne) * MOE_TS
    cnt = counts[0, :ne].astype(jnp.int32)
    size = (cnt + MOE_TS - 1) // MOE_TS * MOE_TS
    ends = jnp.cumsum(size)
    offs = ends - size
    eid = jnp.arange(ne, dtype=jnp.int32)
    slot = lambda e, r: jnp.sum(jnp.where(e[:, None] == eid[None, :], offs[None, :], 0), axis=1) + r
    r = route.astype(jnp.int32)
    slots = jnp.concatenate([slot(r[:, 0], r[:, 2]), slot(r[:, 1], r[:, 3])])
    starts = jnp.arange(n_slots // MOE_TS, dtype=jnp.int32) * MOE_TS
    tile_expert = jnp.minimum(jnp.sum(starts[:, None] >= ends[None, :], axis=1), ne - 1).astype(jnp.int32)
    n_used = (ends[-1:] // MOE_TS).astype(jnp.int32)
    xs = _dispatch_call(slots, h2, n_slots)
    zs = _expert_call(tile_expert, n_used, xs, wg, wu, wd)
    return _combine_call(slots, x, route, mods, zs, n_lat)


def _rope_tables(n_lat, n_ctx):
    t = jnp.arange(n_lat, dtype=jnp.int32)
    n_freq = MLA_ROPE // 4
    inv = ROPE_THETA ** (-jnp.arange(n_freq, dtype=F32) / n_freq)
    ang = jnp.concatenate([(t // GRID_W).astype(F32)[:, None] * inv[None],
                           (t % GRID_W).astype(F32)[:, None] * inv[None]], axis=-1)
    cos, sin = jnp.cos(ang), jnp.sin(ang)
    ones = jnp.ones((n_lat, MLA_NOPE), F32)
    zn = jnp.zeros((n_lat, MLA_NOPE), F32)
    zh = jnp.zeros_like(sin)
    zp = jnp.zeros((n_lat, LANES - MLA_QK), F32)
    c = jnp.concatenate([ones, cos, cos, zp], axis=-1)
    s1 = jnp.concatenate([zn, -sin, zh, zp], axis=-1)
    s2 = jnp.concatenate([zn, zh, sin, zp], axis=-1)
    ctx = lambda a, fill: jnp.concatenate([a, jnp.full((n_ctx, LANES), fill, F32)], axis=0)
    return ctx(c, 1.0), ctx(s1, 0.0), ctx(s2, 0.0)


def _head_pad_cols(w, width):
    r = w.shape[0]
    w = w.reshape(r, MLA_HEADS, width)
    return jnp.pad(w, ((0, 0), (0, 0), (0, LANES - width))).reshape(r, HEAD_PAD)


def _layer_weights(i, p):
    d = D_MODEL
    w_in = p["w_in"][i]
    cuts = np.cumsum((0,) + IN_SPLITS)
    piece = lambda j: w_in[:, cuts[j]:cuts[j + 1]]
    bf = lambda w: w.astype(BF16)
    row = lambda v: v.reshape(1, -1).astype(F32)
    ukv = p["w_mla_ukv"][i].reshape(MLA_KV_RANK, MLA_HEADS, MLA_NOPE + MLA_V)
    wk_nope = _head_pad_cols(ukv[:, :, :MLA_NOPE].reshape(MLA_KV_RANK, -1), MLA_NOPE)
    kr_place = jnp.zeros((LANES, MLA_HEADS, LANES), F32)
    eye = jnp.eye(MLA_ROPE, dtype=F32)
    kr_place = kr_place.at[:MLA_ROPE, :, MLA_NOPE:MLA_QK].set(jnp.broadcast_to(eye[:, None, :], (MLA_ROPE, MLA_HEADS, MLA_ROPE)))
    vone = jnp.zeros((MLA_HEADS, LANES), F32).at[:, MLA_V].set(1.0).reshape(1, HEAD_PAD)
    head_gain = lambda g: jnp.tile(jnp.pad(g, (0, LANES - MLA_QK)), MLA_HEADS).reshape(1, HEAD_PAD)
    e64 = jnp.kron(jnp.eye(NA_HEADS, dtype=F32), jnp.ones((NA_DIM, NA_DIM), F32))
    wa = jnp.pad(p["w_br_mla"][i].reshape(MLA_HEADS, MLA_V, d), ((0, 0), (0, LANES - MLA_V), (0, 0)))
    return {
        "g1": row(p["g_norm1"][i]), "g2": row(p["g_norm2"][i]),
        "wq": bf(piece(0)), "wkv": bf(piece(1)),
        "wkr": bf(jnp.pad(piece(2), ((0, 0), (0, LANES - MLA_ROPE)))),
        "wnq": bf(piece(3)), "wnk": bf(piece(4)), "wnv": bf(piece(5)), "wu": bf(piece(6)), "wg": bf(piece(7)),
        "gq": row(p["g_mla_q"][i]), "gkv": row(p["g_mla_kv"][i]),
        "wuq": bf(_head_pad_cols(p["w_mla_uq"][i], MLA_QK)),
        "wk": bf(jnp.concatenate([wk_nope, kr_place.reshape(LANES, HEAD_PAD)], axis=0)),
        "wv": bf(_head_pad_cols(ukv[:, :, MLA_NOPE:].reshape(MLA_KV_RANK, -1), MLA_V)),
        "vone": vone,
        "gqn": head_gain(p["g_mla_qn"][i]), "gkn": head_gain(p["g_mla_kn"][i]),
        "e64": bf(e64),
        "gnq": jnp.tile(p["g_na_qn"][i], NA_HEADS).reshape(1, -1), "gnk": jnp.tile(p["g_na_kn"][i], NA_HEADS).reshape(1, -1),
        "wa": bf(wa.reshape(HEAD_PAD, d)), "wb": bf(p["w_br_na"][i]), "wc": bf(p["w_br_s5"][i]),
        "wglu": bf(p["w_glu"][i]), "bglu": row(p["b_glu"][i]), "d": row(p["s5_d"][i]), "wo": bf(p["w_out"][i]),
    }


def kernel(x, c, ctx, c_ctx, w_mod, b_mod, g_norm1, g_norm2, w_in, g_mla_q, g_mla_kv, w_mla_uq, w_mla_ukv,
           g_mla_qn, g_mla_kn, g_na_qn, g_na_kn, na_rpb, s5_a_re, s5_a_im, s5_log_dt, s5_b_re, s5_b_im,
           s5_c_re, s5_c_im, s5_d, w_glu, b_glu, w_br_mla, w_br_na, w_br_s5, w_out, w_ffn_gate, w_ffn_up,
           w_ffn_down, w_router, w_exp_gate, w_exp_up, w_exp_down):
    p = dict(w_in=w_in, g_norm1=g_norm1, g_norm2=g_norm2, g_mla_q=g_mla_q, g_mla_kv=g_mla_kv, w_mla_uq=w_mla_uq,
             w_mla_ukv=w_mla_ukv, g_mla_qn=g_mla_qn, g_mla_kn=g_mla_kn, g_na_qn=g_na_qn, g_na_kn=g_na_kn,
             s5_d=s5_d, w_glu=w_glu, b_glu=b_glu, w_br_mla=w_br_mla, w_br_na=w_br_na, w_br_s5=w_br_s5, w_out=w_out)
    assert x.shape[0] == 1 and x.shape[2] == D_MODEL
    n_lat, n_ctx = x.shape[1], ctx.shape[1]
    depth = w_mod.shape[0]
    xs = jnp.concatenate([x[0], ctx[0]], axis=0)

    cvec = jnp.zeros((8, D_MODEL), F32).at[0].set(c[0]).at[1].set(c_ctx)
    mods_all = _mod_call(cvec, w_mod, b_mod)[:, :2, None, :]
    rope = _rope_tables(n_lat, n_ctx)

    for i in range(depth):
        lw = _layer_weights(i, p)
        mods = mods_all[i]
        q, k, v, nq, nk, nv, u, gate = _pre_call(xs, mods, n_lat, lw, rope)

        o_a = _mla_call(q, k, v, n_lat)
        bias = _na_bias(na_rpb[i], n_lat // GRID_W)
        o_b = _na_call(nq, nk, nv, bias, n_lat, n_ctx)
        o_a_c, o_b_c = _ctx_call(q, k, v, nq, nk, nv, n_lat, n_ctx)
        o_a = jnp.concatenate([o_a, o_a_c], axis=0)
        o_b = jnp.concatenate([o_b, o_b_c], axis=0)

        ops = _s5prep_call(s5_a_re[i], s5_a_im[i], s5_log_dt[i], s5_b_re[i], s5_b_im[i], s5_c_re[i], s5_c_im[i])
        ys = _s5_mix(u, ops, n_lat, n_ctx)

        moe = i % 2 == 1
        j = i // 2
        if moe:
            lw["wrouter"] = jnp.pad(w_router[j], ((0, 0), (0, LANES - N_EXPERTS)))
            xs, h2, route, counts = _post_call(xs, mods, o_a, o_b, ys, u, gate, lw, n_lat, True)
            xs = _moe_call(xs, h2, route, counts, mods, w_exp_gate[j].astype(BF16), w_exp_up[j].astype(BF16),
                           w_exp_down[j].astype(BF16), n_lat)
        else:
            xs, h2 = _post_call(xs, mods, o_a, o_b, ys, u, gate, lw, n_lat, False)
            xs = _ffn_call(xs, h2, mods, w_ffn_gate[j].astype(BF16), w_ffn_up[j].astype(BF16),
                           w_ffn_down[j].astype(BF16), n_lat)
    return xs[:n_lat][None]
```

```python
import functools
import math

import jax
import jax.numpy as jnp
import numpy as np
from jax import lax
from jax.experimental import pallas as pl
from jax.experimental.pallas import tpu as pltpu

F32 = jnp.float32
BF16 = jnp.bfloat16

D_MODEL = 1024
DEPTH = 4
GRID_W = 64
EPS = 1e-6
ROPE_THETA = 10000.0

MLA_HEADS = 8
MLA_NOPE = 64
MLA_ROPE = 32
MLA_V = 64
MLA_Q_RANK = 384
MLA_KV_RANK = 256
MLA_QK = MLA_NOPE + MLA_ROPE

NA_HEADS = 8
NA_DIM = 64
NA_KH = 8
NA_KW = 16
NA_WIDTH = NA_HEADS * NA_DIM

S5_GROUPS = 32
S5_GCH = 16
S5_STATE = 64
S5_WIDTH = S5_GROUPS * S5_GCH

N_BRANCH = 3
D_FF = 2816
N_EXPERTS = 8
TOP_K = 2

IN_SPLITS = (MLA_Q_RANK, MLA_KV_RANK, MLA_ROPE, NA_WIDTH, NA_WIDTH, NA_WIDTH, S5_WIDTH, N_BRANCH * D_MODEL)

LANES = 128
HEAD_PAD = MLA_HEADS * LANES
S5_T = 16
NA_QROWS = 4
NA_WROWS = NA_QROWS + NA_KH - 1
VMEM_LIMIT = 56 * 1024 * 1024
NEG = -1e30


def _pick(n, candidates):
    for c in candidates:
        if n % c == 0:
            return c
    raise ValueError(f"no tile in {candidates} divides {n}")


def _const_spec(shape):
    nd = len(shape)
    return pl.BlockSpec(shape, lambda *_: (0,) * nd, pipeline_mode=pl.Buffered(1))


def _params(sem):
    return pltpu.CompilerParams(dimension_semantics=sem, vmem_limit_bytes=VMEM_LIMIT)


def _dot(a, b):
    return jnp.dot(a, b, preferred_element_type=F32)


def _dot_t(a, b):
    return lax.dot_general(a, b, (((1,), (1,)), ((), ())), preferred_element_type=F32)


def _split(x):
    hi = x.astype(BF16)
    lo = (x - hi.astype(F32)).astype(BF16)
    return hi, lo


def _dot3(a, b):
    a_hi, a_lo = _split(a)
    b_hi, b_lo = _split(b)
    return _dot(a_hi, b_hi) + _dot(a_hi, b_lo) + _dot(a_lo, b_hi)


def _dot3_t(a, b):
    a_hi, a_lo = _split(a)
    b_hi, b_lo = _split(b)
    return _dot_t(a_hi, b_hi) + _dot_t(a_hi, b_lo) + _dot_t(a_lo, b_hi)


def _rms(x, g):
    ms = jnp.mean(x * x, axis=-1, keepdims=True)
    return x * lax.rsqrt(ms + EPS) * g


def _row_select(tile, tm, n_lat, mod_ref, idx):
    rows = tile * tm + lax.broadcasted_iota(jnp.int32, (tm, 1), 0)
    lat = mod_ref[0, :, idx * D_MODEL:(idx + 1) * D_MODEL]
    ctx = mod_ref[1, :, idx * D_MODEL:(idx + 1) * D_MODEL]
    return jnp.where(rows < n_lat, lat, ctx)


def _mod_kernel(c_ref, w_ref, b_ref, o_ref):
    c = c_ref[...]
    a = c * jax.nn.sigmoid(c)
    o_ref[...] = _dot3(a, w_ref[...]) + b_ref[...]


def _mod_call(cvec, w_mod, b_mod):
    depth, d, n6 = w_mod.shape
    tn = _pick(n6, (1536, 1024, 512, 128))
    return pl.pallas_call(
        _mod_kernel,
        grid=(depth, n6 // tn),
        in_specs=[
            pl.BlockSpec((8, d), lambda l, j: (0, 0)),
            pl.BlockSpec((None, d, tn), lambda l, j: (l, 0, j)),
            pl.BlockSpec((None, 1, tn), lambda l, j: (l, 0, j)),
        ],
        out_specs=pl.BlockSpec((None, 8, tn), lambda l, j: (l, 0, j)),
        out_shape=jax.ShapeDtypeStruct((depth, 8, n6), F32),
        compiler_params=_params(("arbitrary", "arbitrary")),
        name="mod",
    )(cvec, w_mod, b_mod.reshape(depth, 1, n6))


def _head_norm_rope(xp, gain, c_t, s1_t, s2_t, scale, o_ref):
    for h in range(MLA_HEADS):
        sl = slice(h * LANES, (h + 1) * LANES)
        xh = xp[:, sl]
        ms = jnp.sum(xh * xh, axis=-1, keepdims=True) * (1.0 / MLA_QK)
        y = xh * lax.rsqrt(ms + EPS) * gain[:, sl]
        y = y * c_t + pltpu.roll(y, LANES - MLA_ROPE // 2, 1) * s1_t + pltpu.roll(y, MLA_ROPE // 2, 1) * s2_t
        if scale != 1.0:
            y = y * scale
        o_ref[:, sl] = y.astype(o_ref.dtype)


def _pre_kernel(n_lat, tm, mla_scale, na_scale,
                x_ref, mod_ref, g1_ref, wq_ref, wkv_ref, wkr_ref, wnq_ref, wnk_ref, wnv_ref, wu_ref, wg_ref,
                gq_ref, gkv_ref, wuq_ref, wk_ref, wv_ref, vone_ref, gqn_ref, gkn_ref,
                rc_ref, rs1_ref, rs2_ref, e64_ref, gnq_ref, gnk_ref,
                q_ref, k_ref, v_ref, nq_ref, nk_ref, nv_ref, u_ref, gate_ref):
    i = pl.program_id(0)
    sh = _row_select(i, tm, n_lat, mod_ref, 0)
    sc = _row_select(i, tm, n_lat, mod_ref, 1)
    h = (_rms(x_ref[...], g1_ref[...]) * (1.0 + sc) + sh).astype(BF16)

    c_t, s1_t, s2_t = rc_ref[...], rs1_ref[...], rs2_ref[...]

    rq = _rms(_dot(h, wq_ref[...]), gq_ref[...]).astype(BF16)
    _head_norm_rope(_dot(rq, wuq_ref[...]), gqn_ref[...], c_t, s1_t, s2_t, mla_scale, q_ref)
    rkv = _rms(_dot(h, wkv_ref[...]), gkv_ref[...]).astype(BF16)
    pkr = _dot(h, wkr_ref[...]).astype(BF16)
    kin = jnp.concatenate([rkv, pkr], axis=-1)
    _head_norm_rope(_dot(kin, wk_ref[...]), gkn_ref[...], c_t, s1_t, s2_t, 1.0, k_ref)
    v_ref[...] = (_dot(rkv, wv_ref[...]) + vone_ref[...]).astype(BF16)

    def na_norm(w_ref, g_ref, scale):
        p = _dot(h, w_ref[...])
        sq_hi, sq_lo = _split(p * p)
        ss = (_dot(sq_hi, e64_ref[...]) + _dot(sq_lo, e64_ref[...])) * (1.0 / NA_DIM)
        return (p * lax.rsqrt(ss + EPS) * (g_ref[...] * scale)).astype(BF16)

    nq_ref[...] = na_norm(wnq_ref, gnq_ref, na_scale)
    nk_ref[...] = na_norm(wnk_ref, gnk_ref, 1.0)
    nv_ref[...] = _dot(h, wnv_ref[...]).astype(BF16)

    u_ref[...] = _dot(h, wu_ref[...])
    gate_ref[...] = jax.nn.sigmoid(_dot(h, wg_ref[...])).astype(BF16)


def _pre_call(x, mods, n_lat, lw, rope):
    L, d = x.shape
    tm = _pick(L, (320, 256, 128))
    row = lambda w: pl.BlockSpec((tm, w), lambda i: (i, 0))
    weights = [lw[k] for k in ("g1", "wq", "wkv", "wkr", "wnq", "wnk", "wnv", "wu", "wg", "gq", "gkv", "wuq",
                               "wk", "wv", "vone", "gqn", "gkn")]
    tail = [lw["e64"], lw["gnq"], lw["gnk"]]
    in_specs = ([row(d), _const_spec(mods.shape)] + [_const_spec(w.shape) for w in weights]
                + [row(LANES)] * 3 + [_const_spec(w.shape) for w in tail])
    outs = [(HEAD_PAD, BF16)] * 3 + [(NA_WIDTH, BF16)] * 3 + [(S5_WIDTH, F32), (N_BRANCH * d, BF16)]
    return pl.pallas_call(
        functools.partial(_pre_kernel, n_lat, tm, MLA_QK ** -0.5 * math.log2(math.e), NA_DIM ** -0.5),
        grid=(L // tm,),
        in_specs=in_specs,
        out_specs=[row(w) for w, _ in outs],
        out_shape=[jax.ShapeDtypeStruct((L, w), t) for w, t in outs],
        compiler_params=_params(("arbitrary",)),
        name="pre",
    )(x, mods, *weights, *rope, *tail)


def _mla_kernel(tq, tk, n_chunks, q_ref, k_ref, v_ref, o_ref, m_ref, acc_ref, sa_ref, sb_ref):
    m_ref[...] = jnp.full_like(m_ref, NEG)
    acc_ref[...] = jnp.zeros_like(acc_ref)
    q = q_ref[...]

    def scores(c):
        off = pl.multiple_of(c * tk, tk)
        return _dot_t(q, k_ref[pl.ds(off, tk), :])

    def absorb(s, c):
        off = pl.multiple_of(c * tk, tk)
        m_prev = m_ref[...]
        m_new = jnp.maximum(m_prev, jnp.max(s, axis=-1, keepdims=True))
        p = jnp.exp2(s - jnp.concatenate([m_new] * (tk // LANES), axis=1))
        acc_ref[...] = jnp.exp2(m_prev - m_new) * acc_ref[...] + _dot(p.astype(BF16), v_ref[pl.ds(off, tk), :])
        m_ref[...] = m_new

    sa_ref[...] = scores(0)

    def body(i, carry):
        c = 2 * i
        sb_ref[...] = scores(c + 1)
        absorb(sa_ref[...], c)
        sa_ref[...] = scores(c + 2)
        absorb(sb_ref[...], c + 1)
        return carry

    lax.fori_loop(0, (n_chunks - 1) // 2, body, 0)
    if n_chunks % 2 == 0:
        sb_ref[...] = scores(n_chunks - 1)
        absorb(sa_ref[...], n_chunks - 2)
        absorb(sb_ref[...], n_chunks - 1)
    else:
        absorb(sa_ref[...], n_chunks - 1)
    acc = acc_ref[...]
    o_ref[...] = (acc * pl.reciprocal(acc[:, MLA_V:MLA_V + 1], approx=False)).astype(o_ref.dtype)


def _mla_call(q, k, v, n_lat):
    L = q.shape[0]
    tq = _pick(n_lat, (512, 256))
    tk = _pick(L, (1280, 768, 512, 256))
    kv_spec = pl.BlockSpec((L, LANES), lambda h, i: (0, h))
    return pl.pallas_call(
        functools.partial(_mla_kernel, tq, tk, L // tk),
        grid=(MLA_HEADS, n_lat // tq),
        in_specs=[pl.BlockSpec((tq, LANES), lambda h, i: (i, h)), kv_spec, kv_spec],
        out_specs=pl.BlockSpec((tq, LANES), lambda h, i: (i, h)),
        out_shape=jax.ShapeDtypeStruct((n_lat, HEAD_PAD), BF16),
        scratch_shapes=[pltpu.VMEM((tq, LANES), F32), pltpu.VMEM((tq, LANES), F32),
                        pltpu.VMEM((tq, tk), F32), pltpu.VMEM((tq, tk), F32)],
        compiler_params=_params(("arbitrary", "arbitrary")),
        name="mla",
    )(q, k, v)


def _softmax_pv(parts):
    m = parts[0][0].max(axis=-1, keepdims=True)
    for s, _ in parts[1:]:
        m = jnp.maximum(m, s.max(axis=-1, keepdims=True))
    num, den = 0.0, 0.0
    for s, v in parts:
        p = jnp.exp(s - m)
        den = den + p.sum(axis=-1, keepdims=True)
        num = num + _dot(p.astype(BF16), v)
    return num * pl.reciprocal(den, approx=False)


def _na_kernel(n_lat, n_ctx, n_blocks, q_ref, k_ref, v_ref, bias_ref, o_ref):
    b = pl.program_id(1)
    wtok = NA_WROWS * GRID_W
    row0 = jnp.clip(b * NA_QROWS - NA_KH // 2, 0, n_lat // GRID_W - NA_WROWS)
    off = pl.multiple_of(row0 * GRID_W, GRID_W)
    q = q_ref[...]
    kw, vw = k_ref[pl.ds(off, wtok), :], v_ref[pl.ds(off, wtok), :]
    kc, vc = k_ref[n_lat:n_lat + n_ctx, :], v_ref[n_lat:n_lat + n_ctx, :]
    lane = lax.broadcasted_iota(jnp.int32, (1, LANES), 1)
    out = jnp.zeros(q.shape, F32)
    for j in range(LANES // NA_DIM):
        head = (lane >= j * NA_DIM) & (lane < (j + 1) * NA_DIM)
        qh = jnp.where(head, q, jnp.zeros_like(q))
        s_win = _dot_t(qh, kw) + bias_ref[j]
        s_ctx = _dot_t(qh, kc)
        out = jnp.where(head, _softmax_pv([(s_ctx, vc), (s_win, vw)]), out)
    o_ref[...] = out.astype(o_ref.dtype)


def _na_bias(rpb, n_rows):
    n_blocks = n_rows // NA_QROWS
    cols = np.arange(GRID_W)
    c0 = np.clip(cols - NA_KW // 2, 0, GRID_W - NA_KW)
    col_ok = (cols[None, :] >= c0[:, None]) & (cols[None, :] < c0[:, None] + NA_KW)
    dc = cols[None, :] - cols[:, None] + NA_KW - 1
    pick_c = (col_ok[:, :, None] & (dc[:, :, None] == np.arange(2 * NA_KW - 1))).astype(np.float32)
    pick_r = np.zeros((3, NA_QROWS, NA_WROWS, 2 * NA_KH - 1), np.float32)
    row_oks = []
    for v, b in enumerate((0, 1, n_blocks - 1)):
        row0 = min(max(b * NA_QROWS - NA_KH // 2, 0), n_rows - NA_WROWS)
        qr = b * NA_QROWS + np.arange(NA_QROWS)
        kr = row0 + np.arange(NA_WROWS)
        r0 = np.clip(qr - NA_KH // 2, 0, n_rows - NA_KH)
        row_ok = (kr[None, :] >= r0[:, None]) & (kr[None, :] < r0[:, None] + NA_KH)
        dr = kr[None, :] - qr[:, None] + NA_KH - 1
        pick_r[v] = row_ok[:, :, None] & (dr[:, :, None] == np.arange(2 * NA_KH - 1))
        row_oks.append(row_ok)
    hi = lax.Precision.HIGHEST
    by_col = jnp.einsum("hrd,qkd->hrqk", rpb.astype(F32), pick_c, precision=hi)
    vals = jnp.einsum("vabr,hrqk->vhaqbk", pick_r, by_col, precision=hi)
    ok = np.stack(row_oks)[:, None, :, None, :, None] & col_ok[None, None, None, :, None, :]
    vals = jnp.where(ok, vals, NEG)
    return vals.reshape(3, rpb.shape[0], NA_QROWS * GRID_W, NA_WROWS * GRID_W)


def _na_call(nq, nk, nv, bias, n_lat, n_ctx):
    L = nq.shape[0]
    tq = NA_QROWS * GRID_W
    n_blocks = n_lat // tq
    hp = LANES // NA_DIM
    kv_spec = pl.BlockSpec((L, LANES), lambda h, b: (0, h))

    def bias_map(h, b):
        return (jnp.where(b == 0, 0, jnp.where(b == n_blocks - 1, 2, 1)), h, 0, 0)

    return pl.pallas_call(
        functools.partial(_na_kernel, n_lat, n_ctx, n_blocks),
        grid=(NA_HEADS // hp, n_blocks),
        in_specs=[pl.BlockSpec((tq, LANES), lambda h, b: (b, h)), kv_spec, kv_spec,
                  pl.BlockSpec((None, hp, tq, NA_WROWS * GRID_W), bias_map)],
        out_specs=pl.BlockSpec((tq, LANES), lambda h, b: (b, h)),
        out_shape=jax.ShapeDtypeStruct((n_lat, NA_WIDTH), BF16),
        compiler_params=_params(("arbitrary", "arbitrary")),
        name="na",
    )(nq, nk, nv, bias)


def _ctx_kernel(q_ref, k_ref, v_ref, nq_ref, nk_ref, nv_ref, oa_ref, ob_ref):
    for h in range(MLA_HEADS):
        sl = slice(h * LANES, (h + 1) * LANES)
        s = _dot_t(q_ref[:, sl], k_ref[:, sl])
        p = jnp.exp2(s - s.max(axis=-1, keepdims=True))
        acc = _dot(p.astype(BF16), v_ref[:, sl])
        oa_ref[:, sl] = (acc * pl.reciprocal(acc[:, MLA_V:MLA_V + 1], approx=False)).astype(oa_ref.dtype)
    nq, nk, nv = nq_ref[...], nk_ref[...], nv_ref[...]
    outs = []
    for h in range(NA_HEADS):
        sl = slice(h * NA_DIM, (h + 1) * NA_DIM)
        outs.append(_softmax_pv([(_dot_t(nq[:, sl], nk[:, sl]), nv[:, sl])]))
    ob_ref[...] = jnp.concatenate(outs, axis=-1).astype(ob_ref.dtype)


def _ctx_call(q, k, v, nq, nk, nv, n_lat, n_ctx):
    blk = n_lat // n_ctx
    a_spec = pl.BlockSpec((n_ctx, HEAD_PAD), lambda i: (blk, 0))
    b_spec = pl.BlockSpec((n_ctx, NA_WIDTH), lambda i: (blk, 0))
    return pl.pallas_call(
        _ctx_kernel,
        grid=(1,),
        in_specs=[a_spec] * 3 + [b_spec] * 3,
        out_specs=[pl.BlockSpec((n_ctx, HEAD_PAD), lambda i: (0, 0)),
                   pl.BlockSpec((n_ctx, NA_WIDTH), lambda i: (0, 0))],
        out_shape=[jax.ShapeDtypeStruct((n_ctx, HEAD_PAD), BF16), jax.ShapeDtypeStruct((n_ctx, NA_WIDTH), BF16)],
        compiler_params=_params(("arbitrary",)),
        name="ctx_attn",
    )(q, k, v, nq, nk, nv)


S5_SG = LANES // S5_GCH
S5_SW = S5_SG * S5_STATE


def _s5prep_kernel(are_ref, aim_ref, ldt_ref, bre_ref, bim_ref, cre_ref, cim_ref,
                   arec_ref, aimc_ref, ldtc_ref, crec_ref, cimc_ref,
                   d_ref, bbr_ref, bbi_ref, wcr_ref, wci_ref, atr_ref, ati_ref):
    for ref in (d_ref, bbr_ref, bbi_ref, wcr_ref, wci_ref):
        ref[...] = jnp.zeros_like(ref)
    for g in range(S5_SG):
        rows, lanes = slice(g * S5_GCH, (g + 1) * S5_GCH), slice(g * S5_STATE, (g + 1) * S5_STATE)
        a_re, a_im = are_ref[g], aim_ref[g]
        dt = jnp.exp(ldt_ref[g])
        steps = lax.broadcasted_iota(jnp.int32, (S5_T + 1, S5_STATE), 0).astype(F32)
        mag = jnp.exp(a_re * dt * steps)
        p_re, p_im = mag * jnp.cos(a_im * dt * steps), mag * jnp.sin(a_im * dt * steps)
        nr, ni = p_re[1:2] - 1.0, p_im[1:2]
        den = 1.0 / (a_re * a_re + a_im * a_im)
        f_re, f_im = (nr * a_re + ni * a_im) * den, (ni * a_re - nr * a_im) * den
        b_re, b_im = bre_ref[g], bim_ref[g]
        bb_re, bb_im = f_re * b_re - f_im * b_im, f_re * b_im + f_im * b_re
        c_re, c_im = cre_ref[g], cim_ref[g]
        pw = [(p_re[e:e + 1], p_im[e:e + 1]) for e in range(S5_T + 1)]
        w_re = jnp.concatenate([c_re * r - c_im * i for r, i in pw[:S5_T]], axis=0)
        w_im = jnp.concatenate([c_re * i + c_im * r for r, i in pw[:S5_T]], axis=0)
        kt = _dot3_t(bb_re, w_re) - _dot3_t(bb_im, w_im)
        for e in range(S5_T):
            r, i = pw[e]
            d_ref[e, rows, rows] = kt[:, e * S5_GCH:(e + 1) * S5_GCH].astype(d_ref.dtype)
            bbr_ref[e, rows, lanes] = (r * bb_re - i * bb_im).astype(bbr_ref.dtype)
            bbi_ref[e, rows, lanes] = (r * bb_im + i * bb_re).astype(bbi_ref.dtype)
        atr_ref[:, lanes], ati_ref[:, lanes] = pw[S5_T]
        ac_re, ac_im = arec_ref[g], aimc_ref[g]
        dtc = jnp.exp(ldtc_ref[g])
        steps_c = lax.broadcasted_iota(jnp.int32, (S5_STATE, S5_T), 1).astype(F32) + 1.0
        mag_c = jnp.exp(ac_re * dtc * steps_c)
        q_re, q_im = mag_c * jnp.cos(ac_im * dtc * steps_c), mag_c * jnp.sin(ac_im * dtc * steps_c)
        ct_re, ct_im = crec_ref[g], cimc_ref[g]
        for e in range(S5_T):
            r, i = q_re[:, e:e + 1], q_im[:, e:e + 1]
            wcr_ref[e, lanes, rows] = (ct_re * r - ct_im * i).astype(wcr_ref.dtype)
            wci_ref[e, lanes, rows] = (-(ct_re * i + ct_im * r)).astype(wci_ref.dtype)


def _s5prep_call(a_re, a_im, log_dt, b_re, b_im, c_re, c_im):
    nd, g, p = a_re.shape
    cg = b_re.shape[-1]
    ns = g // S5_SG
    spec = lambda *s: pl.BlockSpec((None, S5_SG) + s, lambda d, j: (d, j) + (0,) * len(s))
    out = lambda *s: pl.BlockSpec((None, None) + s, lambda d, j: (d, j) + (0,) * len(s))
    sds = lambda t, *s: jax.ShapeDtypeStruct((nd, ns) + s, t)
    tr = lambda t: jnp.swapaxes(t, -1, -2)
    ldt = jnp.broadcast_to(log_dt[:, :, None, None], (nd, g, 1, p))
    return pl.pallas_call(
        _s5prep_kernel,
        grid=(nd, ns),
        in_specs=[spec(1, p)] * 3 + [spec(cg, p)] * 4 + [spec(p, 1)] * 3 + [spec(p, cg)] * 2,
        out_specs=[out(S5_T, LANES, LANES), out(S5_T, LANES, S5_SW), out(S5_T, LANES, S5_SW),
                   out(S5_T, S5_SW, LANES), out(S5_T, S5_SW, LANES), out(1, S5_SW), out(1, S5_SW)],
        out_shape=[sds(BF16, S5_T, LANES, LANES), sds(BF16, S5_T, LANES, S5_SW), sds(BF16, S5_T, LANES, S5_SW),
                   sds(BF16, S5_T, S5_SW, LANES), sds(BF16, S5_T, S5_SW, LANES), sds(F32, 1, S5_SW),
                   sds(F32, 1, S5_SW)],
        compiler_params=_params(("arbitrary", "arbitrary")),
        name="s5_prep",
    )(a_re[:, :, None, :], a_im[:, :, None, :], ldt, tr(b_re), tr(b_im), c_re, c_im,
      a_re[:, :, :, None], a_im[:, :, :, None], tr(ldt), tr(c_re), tr(c_im))


def _chunk_rows(u_ref, kb):
    return [u_ref[pl.ds(t, kb, stride=S5_T), :].astype(BF16) for t in range(S5_T)]


def _s5a_kernel(kb, u_ref, br_ref, bi_ref, ore_ref, oim_ref):
    x = _chunk_rows(u_ref, kb)
    for d in range(br_ref.shape[0]):
        ahead = lambda t: S5_T - 1 - t if d == 0 else t
        ore_ref[d] = sum(_dot(x[t], br_ref[d, ahead(t)]) for t in range(S5_T))
        oim_ref[d] = sum(_dot(x[t], bi_ref[d, ahead(t)]) for t in range(S5_T))


def _s5_blocks(L):
    nch = L // S5_T
    return nch, _pick(nch, (208, 144, 80, 72, 40, 16, 8))


def _s5a_call(u, bcr, bci):
    L = u.shape[0]
    nd, ns, _, n, ws = bcr.shape
    nch, kb = _s5_blocks(L)
    wspec = pl.BlockSpec((nd, None, S5_T, n, ws), lambda s, r: (0, s, 0, 0, 0))
    ospec = pl.BlockSpec((nd, kb, ws), lambda s, r: (0, r, s))
    return pl.pallas_call(
        functools.partial(_s5a_kernel, kb),
        grid=(ns, nch // kb),
        in_specs=[pl.BlockSpec((kb * S5_T, LANES), lambda s, r: (r, s)), wspec, wspec],
        out_specs=[ospec, ospec],
        out_shape=[jax.ShapeDtypeStruct((nd, nch, ns * ws), F32)] * 2,
        compiler_params=_params(("arbitrary", "arbitrary")),
        name="s5_chunk_inputs",
    )(u, bcr, bci)


def _s5scan_kernel(n_lat_ch, n_ctx_ch, bre_ref, bim_ref, atr_ref, ati_ref, sre_ref, sim_ref):
    d = pl.program_id(0)
    a_re, a_im = atr_ref[...], ati_ref[...]

    def segment(base, count, carry):
        def step(i, st):
            s_re, s_im = st
            k = base + jnp.where(d == 0, i, count - 1 - i)
            sre_ref[k] = s_re
            sim_ref[k] = s_im
            return (a_re * s_re - a_im * s_im + bre_ref[k], a_re * s_im + a_im * s_re + bim_ref[k])
        return lax.fori_loop(0, count, step, carry)

    zero = jnp.zeros(a_re.shape, F32)
    carry = segment(n_lat_ch, n_ctx_ch, (zero, zero))
    segment(0, n_lat_ch, carry)


def _s5scan_call(b_re, b_im, atr, ati, n_lat_ch, n_ctx_ch):
    nd, nch, w = b_re.shape
    shp = (nd, nch, w // LANES, LANES)
    spec = pl.BlockSpec((None, nch, 8, LANES), lambda d, j: (d, 0, j, 0))
    aspec = pl.BlockSpec((None, 8, LANES), lambda d, j: (d, j, 0))
    return pl.pallas_call(
        functools.partial(_s5scan_kernel, n_lat_ch, n_ctx_ch),
        grid=(nd, w // LANES // 8),
        in_specs=[spec, spec, aspec, aspec],
        out_specs=[spec, spec],
        out_shape=[jax.ShapeDtypeStruct(shp, F32)] * 2,
        compiler_params=_params(("arbitrary", "arbitrary")),
        name="s5_scan",
    )(b_re.reshape(shp), b_im.reshape(shp), atr.reshape(nd, w // LANES, LANES), ati.reshape(nd, w // LANES, LANES))


def _s5c_kernel(kb, u_ref, sre_ref, sim_ref, d_ref, wr_ref, wi_ref, y_ref, acc_ref):
    nd = d_ref.shape[0]
    x = jnp.concatenate(_chunk_rows(u_ref, kb), axis=0)
    s_re = [sre_ref[d].astype(BF16) for d in range(nd)]
    s_im = [sim_ref[d].astype(BF16) for d in range(nd)]
    for t in range(S5_T):
        later = lambda d: t if d == 0 else S5_T - 1 - t
        acc_ref[t * kb:(t + 1) * kb, :] = sum(
            _dot(s_re[d], wr_ref[d, later(d)]) + _dot(s_im[d], wi_ref[d, later(d)]) for d in range(nd))
    for e in range(S5_T):
        n = (S5_T - e) * kb
        acc_ref[e * kb:, :] += _dot(x[:n], d_ref[0, e])
        acc_ref[:n, :] += _dot(x[e * kb:], d_ref[1, e])
    for t in range(S5_T):
        y_ref[pl.ds(t, kb, stride=S5_T), :] = acc_ref[t * kb:(t + 1) * kb, :]


def _s5c_call(u, s_re, s_im, dm, wcr, wci):
    L = u.shape[0]
    nd, ns = dm.shape[:2]
    ws = wcr.shape[3]
    nch, kb = _s5_blocks(L)
    uspec = pl.BlockSpec((kb * S5_T, LANES), lambda s, r: (r, s))
    sspec = pl.BlockSpec((nd, kb, ws), lambda s, r: (0, r, s))
    wspec = pl.BlockSpec((nd, None, S5_T, ws, LANES), lambda s, r: (0, s, 0, 0, 0))
    return pl.pallas_call(
        functools.partial(_s5c_kernel, kb),
        grid=(ns, nch // kb),
        in_specs=[uspec, sspec, sspec,
                  pl.BlockSpec((nd, None, S5_T, LANES, LANES), lambda s, r: (0, s, 0, 0, 0)), wspec, wspec],
        out_specs=uspec,
        out_shape=jax.ShapeDtypeStruct(u.shape, F32),
        scratch_shapes=[pltpu.VMEM((kb * S5_T, LANES), F32)],
        compiler_params=_params(("arbitrary", "arbitrary")),
        name="s5_outputs",
    )(u, s_re, s_im, dm, wcr, wci)


def _s5_mix(u, ops, n_lat, n_ctx):
    dm, bcr, bci, wcr, wci, atr, ati = ops
    nd = dm.shape[0]
    b_re, b_im = _s5a_call(u, bcr, bci)
    s_re, s_im = _s5scan_call(b_re, b_im, atr.reshape(nd, -1), ati.reshape(nd, -1), n_lat // S5_T, n_ctx // S5_T)
    return _s5c_call(u, s_re.reshape(b_re.shape), s_im.reshape(b_im.shape), dm, wcr, wci)


def _gelu_tanh(x):
    return 0.5 * x * (1.0 + jnp.tanh(math.sqrt(2.0 / math.pi) * (x + 0.044715 * (x * x * x))))


def _post_kernel(n_lat, tm, moe,
                 x_ref, mod_ref, oa_ref, ob_ref, ys_ref, u_ref, gate_ref,
                 wa_ref, wb_ref, wc_ref, wglu_ref, bglu_ref, d_ref, wo_ref, g2_ref, *rest):
    i = pl.program_id(0)
    d = x_ref.shape[-1]
    g = _gelu_tanh(d_ref[...] * u_ref[...] + ys_ref[...])
    oc = (g * jax.nn.sigmoid(_dot(g.astype(BF16), wglu_ref[...]) + bglu_ref[...])).astype(BF16)
    gate = gate_ref[...].astype(F32)
    mix = (gate[:, :d] * _dot(oa_ref[...], wa_ref[...]) + gate[:, d:2 * d] * _dot(ob_ref[...], wb_ref[...])
           + gate[:, 2 * d:] * _dot(oc, wc_ref[...]))
    x_new = x_ref[...] + _row_select(i, tm, n_lat, mod_ref, 2) * _dot(mix.astype(BF16), wo_ref[...])
    h2 = (_rms(x_new, g2_ref[...]) * (1.0 + _row_select(i, tm, n_lat, mod_ref, 4))
          + _row_select(i, tm, n_lat, mod_ref, 3))
    if not moe:
        xo_ref, h2_ref = rest
    else:
        wr_ref, tri_ref, xo_ref, h2_ref, route_ref, count_ref, carry_ref = rest
        logits = _dot3(h2, wr_ref[...])
        lane = lax.broadcasted_iota(jnp.int32, logits.shape, 1)
        logits = jnp.where(lane < N_EXPERTS, logits, NEG)
        m1 = logits.max(axis=-1, keepdims=True)
        i1 = jnp.where(logits == m1, lane, LANES).min(axis=-1, keepdims=True)
        rest_l = jnp.where(lane == i1, NEG, logits)
        m2 = rest_l.max(axis=-1, keepdims=True)
        i2 = jnp.where(rest_l == m2, lane, LANES).min(axis=-1, keepdims=True)
        e2 = jnp.exp(m2 - m1)
        w1 = 1.0 / (1.0 + e2)

        @pl.when(i == 0)
        def _():
            carry_ref[...] = jnp.zeros_like(carry_ref)

        chosen = jnp.where((lane == i1) | (lane == i2), 1.0, 0.0)
        before = _dot(tri_ref[...], chosen.astype(BF16)) + carry_ref[...]
        r1 = jnp.sum(jnp.where(lane == i1, before, 0.0), axis=-1, keepdims=True)
        r2 = jnp.sum(jnp.where(lane == i2, before, 0.0), axis=-1, keepdims=True)
        total = carry_ref[...] + jnp.sum(chosen, axis=0, keepdims=True)
        carry_ref[...] = total
        count_ref[...] = jnp.broadcast_to(total, count_ref.shape)
        cols = [i1.astype(F32), i2.astype(F32), r1, r2, w1, e2 * w1]
        route = jnp.zeros(logits.shape, F32)
        for j, col in enumerate(cols):
            route = jnp.where(lane == j, col, route)
        route_ref[...] = route
    xo_ref[...] = x_new
    h2_ref[...] = h2.astype(h2_ref.dtype)


def _post_call(x, mods, o_a, o_b, ys, u, gate, lw, n_lat, moe):
    L, d = x.shape
    tm = _pick(L, (320, 256, 128))
    row = lambda w: pl.BlockSpec((tm, w), lambda i: (i, 0))
    weights = [lw[k] for k in ("wa", "wb", "wc", "wglu", "bglu", "d", "wo", "g2")]
    if moe:
        tri = jnp.asarray(np.tril(np.ones((tm, tm), np.float32), -1), BF16)
        weights += [lw["wrouter"], tri]
        outs = [(L, d, F32), (L, d, F32), (L, LANES, F32), (8, LANES, F32)]
        out_specs = [row(d), row(d), row(LANES), pl.BlockSpec((8, LANES), lambda i: (0, 0))]
        scratch = [pltpu.VMEM((1, LANES), F32)]
    else:
        outs = [(L, d, F32), (L, d, BF16)]
        out_specs = [row(d), row(d)]
        scratch = []
    return pl.pallas_call(
        functools.partial(_post_kernel, n_lat, tm, moe),
        grid=(L // tm,),
        in_specs=([row(d), _const_spec(mods.shape), row(HEAD_PAD), row(NA_WIDTH), row(S5_WIDTH), row(S5_WIDTH),
                   row(N_BRANCH * d)] + [_const_spec(w.shape) for w in weights]),
        out_specs=out_specs,
        out_shape=[jax.ShapeDtypeStruct((r, w), t) for r, w, t in outs],
        scratch_shapes=scratch,
        compiler_params=_params(("arbitrary",)),
        name="post_moe" if moe else "post",
    )(x, mods, o_a, o_b, ys, u, gate, *weights)


def _swiglu(h, wg_ref, wu_ref, wd_ref):
    a = _dot(h, wg_ref[...])
    act = (a * jax.nn.sigmoid(a) * _dot(h, wu_ref[...])).astype(BF16)
    return _dot(act, wd_ref[...])


def _ffn_kernel(n_lat, tm, x_ref, h_ref, mod_ref, wg_ref, wu_ref, wd_ref, o_ref):
    f = _swiglu(h_ref[...], wg_ref, wu_ref, wd_ref)
    o_ref[...] = x_ref[...] + _row_select(pl.program_id(0), tm, n_lat, mod_ref, 5) * f


def _ffn_call(x, h2, mods, wg, wu, wd, n_lat):
    L, d = x.shape
    tm = _pick(L, (640, 256, 128))
    row = lambda w: pl.BlockSpec((tm, w), lambda i: (i, 0))
    return pl.pallas_call(
        functools.partial(_ffn_kernel, n_lat, tm),
        grid=(L // tm,),
        in_specs=[row(d), row(d), _const_spec(mods.shape), _const_spec(wg.shape), _const_spec(wu.shape),
                  _const_spec(wd.shape)],
        out_specs=row(d),
        out_shape=jax.ShapeDtypeStruct((L, d), F32),
        compiler_params=_params(("arbitrary",)),
        name="ffn",
    )(x, h2, mods, wg, wu, wd)


MOE_TS = 512


def _row_copy(src, i, dst, j, sem):
    return pltpu.make_async_copy(src.at[pl.ds(i, 1)], dst.at[pl.ds(j, 1)], sem)


ROW_DMA_UNROLL = 8


def _dispatch_kernel(tb, n_tok, slot_ref, h_ref, xs_in, xs_hbm, sem):
    del xs_in
    base = pl.program_id(0) * tb

    def issue(t, carry):
        for k in range(TOP_K):
            _row_copy(h_ref, t, xs_hbm, slot_ref[k * n_tok + base + t], sem).start()
        return carry

    def drain(t, carry):
        for k in range(TOP_K):
            _row_copy(h_ref, 0, xs_hbm, 0, sem).wait()
        return carry

    lax.fori_loop(0, tb, issue, 0, unroll=ROW_DMA_UNROLL)
    lax.fori_loop(0, tb, drain, 0, unroll=ROW_DMA_UNROLL)


def _dispatch_call(slots, h2, n_slots):
    L, d = h2.shape
    tb = _pick(L, (640, 256, 128))
    xs0 = jnp.zeros((n_slots, d), h2.dtype)
    return pl.pallas_call(
        functools.partial(_dispatch_kernel, tb, L),
        grid_spec=pltpu.PrefetchScalarGridSpec(
            num_scalar_prefetch=1, grid=(L // tb,),
            in_specs=[pl.BlockSpec((tb, d), lambda i, s: (i, 0)), pl.BlockSpec(memory_space=pl.ANY)],
            out_specs=pl.BlockSpec(memory_space=pl.ANY),
            scratch_shapes=[pltpu.SemaphoreType.DMA(())]),
        out_shape=jax.ShapeDtypeStruct((n_slots, d), h2.dtype),
        input_output_aliases={2: 0},
        compiler_params=_params(("arbitrary",)),
        name="moe_dispatch",
    )(slots, h2, xs0)


def _expert_kernel(te_ref, nu_ref, x_ref, wg_ref, wu_ref, wd_ref, o_ref):
    used = pl.program_id(0) < nu_ref[0]

    @pl.when(used)
    def _():
        o_ref[...] = _swiglu(x_ref[...].astype(BF16), wg_ref, wu_ref, wd_ref)

    @pl.when(jnp.logical_not(used))
    def _():
        o_ref[...] = jnp.zeros_like(o_ref)


def _expert_call(tile_expert, n_used, xs, wg, wu, wd):
    n_slots, d = xs.shape
    dff = wg.shape[-1]
    once = pl.Buffered(1)
    return pl.pallas_call(
        _expert_kernel,
        grid_spec=pltpu.PrefetchScalarGridSpec(
            num_scalar_prefetch=2, grid=(n_slots // MOE_TS,),
            in_specs=[pl.BlockSpec((MOE_TS, d), lambda j, te, nu: (j, 0)),
                      pl.BlockSpec((None, d, dff), lambda j, te, nu: (te[j], 0, 0), pipeline_mode=once),
                      pl.BlockSpec((None, d, dff), lambda j, te, nu: (te[j], 0, 0), pipeline_mode=once),
                      pl.BlockSpec((None, dff, d), lambda j, te, nu: (te[j], 0, 0), pipeline_mode=once)],
            out_specs=pl.BlockSpec((MOE_TS, d), lambda j, te, nu: (j, 0))),
        out_shape=jax.ShapeDtypeStruct((n_slots, d), F32),
        compiler_params=_params(("arbitrary",)),
        name="moe_experts",
    )(tile_expert, n_used, xs, wg, wu, wd)


def _combine_kernel(n_lat, tb, n_tok, slot_ref, x_ref, route_ref, mod_ref, zs_hbm, o_ref, g_ref, sem):
    i = pl.program_id(0)
    base = i * tb

    def issue(t, carry):
        for k in range(TOP_K):
            _row_copy(zs_hbm, slot_ref[k * n_tok + base + t], g_ref.at[k], t, sem).start()
        return carry

    def drain(t, carry):
        for k in range(TOP_K):
            _row_copy(zs_hbm, 0, g_ref.at[k], 0, sem).wait()
        return carry

    lax.fori_loop(0, tb, issue, 0, unroll=ROW_DMA_UNROLL)
    lax.fori_loop(0, tb, drain, 0, unroll=ROW_DMA_UNROLL)
    route = route_ref[...]
    f = route[:, 4:5] * g_ref[0] + route[:, 5:6] * g_ref[1]
    o_ref[...] = x_ref[...] + _row_select(i, tb, n_lat, mod_ref, 5) * f


def _combine_call(slots, x, route, mods, zs, n_lat):
    L, d = x.shape
    tb = _pick(L, (320, 256, 128))
    row = lambda w: pl.BlockSpec((tb, w), lambda i, s: (i, 0))
    return pl.pallas_call(
        functools.partial(_combine_kernel, n_lat, tb, L),
        grid_spec=pltpu.PrefetchScalarGridSpec(
            num_scalar_prefetch=1, grid=(L // tb,),
            in_specs=[row(d), row(LANES), pl.BlockSpec(mods.shape, lambda i, s: (0, 0, 0)),
                      pl.BlockSpec(memory_space=pl.ANY)],
            out_specs=row(d),
            scratch_shapes=[pltpu.VMEM((TOP_K, tb, d), F32), pltpu.SemaphoreType.DMA(())]),
        out_shape=jax.ShapeDtypeStruct((L, d), F32),
        compiler_params=_params(("arbitrary",)),
        name="moe_combine",
    )(slots, x, route, mods, zs)


def _moe_call(x, h2, route, counts, mods, wg, wu, wd, n_lat):
    L, d = x.shape
    ne = wg.shape[0]
    n_slots = (pl.cdiv(TOP_K * L, MOE_TS) + ne) * MOE_TS
    cnt = counts[0, :ne].astype(jnp.int32)
    size = (cnt + MOE_TS - 1) // MOE_TS * MOE_TS
    ends = jnp.cumsum(size)
    offs = ends - size
    eid = jnp.arange(ne, dtype=jnp.int32)
    slot = lambda e, r: jnp.sum(jnp.where(e[:, None] == eid[None, :], offs[None, :], 0), axis=1) + r
    r = route.astype(jnp.int32)
    slots = jnp.concatenate([slot(r[:, 0], r[:, 2]), slot(r[:, 1], r[:, 3])])
    starts = jnp.arange(n_slots // MOE_TS, dtype=jnp.int32) * MOE_TS
    tile_expert = jnp.minimum(jnp.sum(starts[:, None] >= ends[None, :], axis=1), ne - 1).astype(jnp.int32)
    n_used = (ends[-1:] // MOE_TS).astype(jnp.int32)
    xs = _dispatch_call(slots, h2, n_slots)
    zs = _expert_call(tile_expert, n_used, xs, wg, wu, wd)
    return _combine_call(slots, x, route, mods, zs, n_lat)


def _rope_tables(n_lat, n_ctx):
    t = jnp.arange(n_lat, dtype=jnp.int32)
    n_freq = MLA_ROPE // 4
    inv = ROPE_THETA ** (-jnp.arange(n_freq, dtype=F32) / n_freq)
    ang = jnp.concatenate([(t // GRID_W).astype(F32)[:, None] * inv[None],
                           (t % GRID_W).astype(F32)[:, None] * inv[None]], axis=-1)
    cos, sin = jnp.cos(ang), jnp.sin(ang)
    ones = jnp.ones((n_lat, MLA_NOPE), F32)
    zn = jnp.zeros((n_lat, MLA_NOPE), F32)
    zh = jnp.zeros_like(sin)
    zp = jnp.zeros((n_lat, LANES - MLA_QK), F32)
    c = jnp.concatenate([ones, cos, cos, zp], axis=-1)
    s1 = jnp.concatenate([zn, -sin, zh, zp], axis=-1)
    s2 = jnp.concatenate([zn, zh, sin, zp], axis=-1)
    ctx = lambda a, fill: jnp.concatenate([a, jnp.full((n_ctx, LANES), fill, F32)], axis=0)
    return ctx(c, 1.0), ctx(s1, 0.0), ctx(s2, 0.0)


def _head_pad_cols(w, width):
    r = w.shape[0]
    w = w.reshape(r, MLA_HEADS, width)
    return jnp.pad(w, ((0, 0), (0, 0), (0, LANES - width))).reshape(r, HEAD_PAD)


def _layer_weights(i, p):
    d = D_MODEL
    w_in = p["w_in"][i]
    cuts = np.cumsum((0,) + IN_SPLITS)
    piece = lambda j: w_in[:, cuts[j]:cuts[j + 1]]
    bf = lambda w: w.astype(BF16)
    row = lambda v: v.reshape(1, -1).astype(F32)
    ukv = p["w_mla_ukv"][i].reshape(MLA_KV_RANK, MLA_HEADS, MLA_NOPE + MLA_V)
    wk_nope = _head_pad_cols(ukv[:, :, :MLA_NOPE].reshape(MLA_KV_RANK, -1), MLA_NOPE)
    kr_place = jnp.zeros((LANES, MLA_HEADS, LANES), F32)
    eye = jnp.eye(MLA_ROPE, dtype=F32)
    kr_place = kr_place.at[:MLA_ROPE, :, MLA_NOPE:MLA_QK].set(jnp.broadcast_to(eye[:, None, :], (MLA_ROPE, MLA_HEADS, MLA_ROPE)))
    vone = jnp.zeros((MLA_HEADS, LANES), F32).at[:, MLA_V].set(1.0).reshape(1, HEAD_PAD)
    head_gain = lambda g: jnp.tile(jnp.pad(g, (0, LANES - MLA_QK)), MLA_HEADS).reshape(1, HEAD_PAD)
    e64 = jnp.kron(jnp.eye(NA_HEADS, dtype=F32), jnp.ones((NA_DIM, NA_DIM), F32))
    wa = jnp.pad(p["w_br_mla"][i].reshape(MLA_HEADS, MLA_V, d), ((0, 0), (0, LANES - MLA_V), (0, 0)))
    return {
        "g1": row(p["g_norm1"][i]), "g2": row(p["g_norm2"][i]),
        "wq": bf(piece(0)), "wkv": bf(piece(1)),
        "wkr": bf(jnp.pad(piece(2), ((0, 0), (0, LANES - MLA_ROPE)))),
        "wnq": bf(piece(3)), "wnk": bf(piece(4)), "wnv": bf(piece(5)), "wu": bf(piece(6)), "wg": bf(piece(7)),
        "gq": row(p["g_mla_q"][i]), "gkv": row(p["g_mla_kv"][i]),
        "wuq": bf(_head_pad_cols(p["w_mla_uq"][i], MLA_QK)),
        "wk": bf(jnp.concatenate([wk_nope, kr_place.reshape(LANES, HEAD_PAD)], axis=0)),
        "wv": bf(_head_pad_cols(ukv[:, :, MLA_NOPE:].reshape(MLA_KV_RANK, -1), MLA_V)),
        "vone": vone,
        "gqn": head_gain(p["g_mla_qn"][i]), "gkn": head_gain(p["g_mla_kn"][i]),
        "e64": bf(e64),
        "gnq": jnp.tile(p["g_na_qn"][i], NA_HEADS).reshape(1, -1), "gnk": jnp.tile(p["g_na_kn"][i], NA_HEADS).reshape(1, -1),
        "wa": bf(wa.reshape(HEAD_PAD, d)), "wb": bf(p["w_br_na"][i]), "wc": bf(p["w_br_s5"][i]),
        "wglu": bf(p["w_glu"][i]), "bglu": row(p["b_glu"][i]), "d": row(p["s5_d"][i]), "wo": bf(p["w_out"][i]),
    }


def kernel(x, c, ctx, c_ctx, w_mod, b_mod, g_norm1, g_norm2, w_in, g_mla_q, g_mla_kv, w_mla_uq, w_mla_ukv,
           g_mla_qn, g_mla_kn, g_na_qn, g_na_kn, na_rpb, s5_a_re, s5_a_im, s5_log_dt, s5_b_re, s5_b_im,
           s5_c_re, s5_c_im, s5_d, w_glu, b_glu, w_br_mla, w_br_na, w_br_s5, w_out, w_ffn_gate, w_ffn_up,
           w_ffn_down, w_router, w_exp_gate, w_exp_up, w_exp_down):
    p = dict(w_in=w_in, g_norm1=g_norm1, g_norm2=g_norm2, g_mla_q=g_mla_q, g_mla_kv=g_mla_kv, w_mla_uq=w_mla_uq,
             w_mla_ukv=w_mla_ukv, g_mla_qn=g_mla_qn, g_mla_kn=g_mla_kn, g_na_qn=g_na_qn, g_na_kn=g_na_kn,
             s5_d=s5_d, w_glu=w_glu, b_glu=b_glu, w_br_mla=w_br_mla, w_br_na=w_br_na, w_br_s5=w_br_s5, w_out=w_out)
    assert x.shape[0] == 1 and x.shape[2] == D_MODEL
    n_lat, n_ctx = x.shape[1], ctx.shape[1]
    depth = w_mod.shape[0]
    xs = jnp.concatenate([x[0], ctx[0]], axis=0)

    cvec = jnp.zeros((8, D_MODEL), F32).at[0].set(c[0]).at[1].set(c_ctx)
    mods_all = _mod_call(cvec, w_mod, b_mod)[:, :2, None, :]
    rope = _rope_tables(n_lat, n_ctx)

    for i in range(depth):
        lw = _layer_weights(i, p)
        mods = mods_all[i]
        q, k, v, nq, nk, nv, u, gate = _pre_call(xs, mods, n_lat, lw, rope)

        o_a = _mla_call(q, k, v, n_lat)
        bias = _na_bias(na_rpb[i], n_lat // GRID_W)
        o_b = _na_call(nq, nk, nv, bias, n_lat, n_ctx)
        o_a_c, o_b_c = _ctx_call(q, k, v, nq, nk, nv, n_lat, n_ctx)
        o_a = jnp.concatenate([o_a, o_a_c], axis=0)
        o_b = jnp.concatenate([o_b, o_b_c], axis=0)

        ops = _s5prep_call(s5_a_re[i], s5_a_im[i], s5_log_dt[i], s5_b_re[i], s5_b_im[i], s5_c_re[i], s5_c_im[i])
        ys = _s5_mix(u, ops, n_lat, n_ctx)

        moe = i % 2 == 1
        j = i // 2
        if moe:
            lw["wrouter"] = jnp.pad(w_router[j], ((0, 0), (0, LANES - N_EXPERTS)))
            xs, h2, route, counts = _post_call(xs, mods, o_a, o_b, ys, u, gate, lw, n_lat, True)
            xs = _moe_call(xs, h2, route, counts, mods, w_exp_gate[j].astype(BF16), w_exp_up[j].astype(BF16),
                           w_exp_down[j].astype(BF16), n_lat)
        else:
            xs, h2 = _post_call(xs, mods, o_a, o_b, ys, u, gate, lw, n_lat, False)
            xs = _ffn_call(xs, h2, mods, w_ffn_gate[j].astype(BF16), w_ffn_up[j].astype(BF16),
                           w_ffn_down[j].astype(BF16), n_lat)
    return xs[:n_lat][None]
```

```python
import functools
import math

import jax
import jax.numpy as jnp
import numpy as np
from jax import lax
from jax.experimental import pallas as pl
from jax.experimental.pallas import tpu as pltpu

F32 = jnp.float32
BF16 = jnp.bfloat16

D_MODEL = 1024
DEPTH = 4
GRID_W = 64
EPS = 1e-6
ROPE_THETA = 10000.0

MLA_HEADS = 8
MLA_NOPE = 64
MLA_ROPE = 32
MLA_V = 64
MLA_Q_RANK = 384
MLA_KV_RANK = 256
MLA_QK = MLA_NOPE + MLA_ROPE

NA_HEADS = 8
NA_DIM = 64
NA_KH = 8
NA_KW = 16
NA_WIDTH = NA_HEADS * NA_DIM

S5_GROUPS = 32
S5_GCH = 16
S5_STATE = 64
S5_WIDTH = S5_GROUPS * S5_GCH

N_BRANCH = 3
D_FF = 2816
N_EXPERTS = 8
TOP_K = 2

IN_SPLITS = (MLA_Q_RANK, MLA_KV_RANK, MLA_ROPE, NA_WIDTH, NA_WIDTH, NA_WIDTH, S5_WIDTH, N_BRANCH * D_MODEL)

LANES = 128
HEAD_PAD = MLA_HEADS * LANES
S5_T = 16
NA_QROWS = 4
NA_WROWS = NA_QROWS + NA_KH - 1
VMEM_LIMIT = 56 * 1024 * 1024
NEG = -1e30


def _pick(n, candidates):
    for c in candidates:
        if n % c == 0:
            return c
    raise ValueError(f"no tile in {candidates} divides {n}")


def _const_spec(shape):
    nd = len(shape)
    return pl.BlockSpec(shape, lambda *_: (0,) * nd, pipeline_mode=pl.Buffered(1))


def _params(sem):
    return pltpu.CompilerParams(dimension_semantics=sem, vmem_limit_bytes=VMEM_LIMIT)


def _dot(a, b):
    return jnp.dot(a, b, preferred_element_type=F32)


def _dot_t(a, b):
    return lax.dot_general(a, b, (((1,), (1,)), ((), ())), preferred_element_type=F32)


def _split(x):
    hi = x.astype(BF16)
    lo = (x - hi.astype(F32)).astype(BF16)
    return hi, lo


def _dot3(a, b):
    a_hi, a_lo = _split(a)
    b_hi, b_lo = _split(b)
    return _dot(a_hi, b_hi) + _dot(a_hi, b_lo) + _dot(a_lo, b_hi)


def _dot3_t(a, b):
    a_hi, a_lo = _split(a)
    b_hi, b_lo = _split(b)
    return _dot_t(a_hi, b_hi) + _dot_t(a_hi, b_lo) + _dot_t(a_lo, b_hi)


def _rms(x, g):
    ms = jnp.mean(x * x, axis=-1, keepdims=True)
    return x * lax.rsqrt(ms + EPS) * g


def _row_select(tile, tm, n_lat, mod_ref, idx):
    rows = tile * tm + lax.broadcasted_iota(jnp.int32, (tm, 1), 0)
    lat = mod_ref[0, :, idx * D_MODEL:(idx + 1) * D_MODEL]
    ctx = mod_ref[1, :, idx * D_MODEL:(idx + 1) * D_MODEL]
    return jnp.where(rows < n_lat, lat, ctx)


def _mod_kernel(c_ref, w_ref, b_ref, o_ref):
    c = c_ref[...]
    a = c * jax.nn.sigmoid(c)
    o_ref[...] = _dot3(a, w_ref[...]) + b_ref[...]


def _mod_call(cvec, w_mod, b_mod):
    depth, d, n6 = w_mod.shape
    tn = _pick(n6, (1536, 1024, 512, 128))
    return pl.pallas_call(
        _mod_kernel,
        grid=(depth, n6 // tn),
        in_specs=[
            pl.BlockSpec((8, d), lambda l, j: (0, 0)),
            pl.BlockSpec((None, d, tn), lambda l, j: (l, 0, j)),
            pl.BlockSpec((None, 1, tn), lambda l, j: (l, 0, j)),
        ],
        out_specs=pl.BlockSpec((None, 8, tn), lambda l, j: (l, 0, j)),
        out_shape=jax.ShapeDtypeStruct((depth, 8, n6), F32),
        compiler_params=_params(("arbitrary", "arbitrary")),
        name="mod",
    )(cvec, w_mod, b_mod.reshape(depth, 1, n6))


def _head_norm_rope(xp, gain, c_t, s1_t, s2_t, scale, o_ref):
    for h in range(MLA_HEADS):
        sl = slice(h * LANES, (h + 1) * LANES)
        xh = xp[:, sl]
        ms = jnp.sum(xh * xh, axis=-1, keepdims=True) * (1.0 / MLA_QK)
        y = xh * lax.rsqrt(ms + EPS) * gain[:, sl]
        y = y * c_t + pltpu.roll(y, LANES - MLA_ROPE // 2, 1) * s1_t + pltpu.roll(y, MLA_ROPE // 2, 1) * s2_t
        if scale != 1.0:
            y = y * scale
        o_ref[:, sl] = y.astype(o_ref.dtype)


def _pre_kernel(n_lat, tm, mla_scale, na_scale,
                x_ref, mod_ref, g1_ref, wq_ref, wkv_ref, wkr_ref, wnq_ref, wnk_ref, wnv_ref, wu_ref, wg_ref,
                gq_ref, gkv_ref, wuq_ref, wk_ref, wv_ref, vone_ref, gqn_ref, gkn_ref,
                rc_ref, rs1_ref, rs2_ref, e64_ref, gnq_ref, gnk_ref,
                q_ref, k_ref, v_ref, nq_ref, nk_ref, nv_ref, u_ref, gate_ref):
    i = pl.program_id(0)
    sh = _row_select(i, tm, n_lat, mod_ref, 0)
    sc = _row_select(i, tm, n_lat, mod_ref, 1)
    h = (_rms(x_ref[...], g1_ref[...]) * (1.0 + sc) + sh).astype(BF16)

    c_t, s1_t, s2_t = rc_ref[...], rs1_ref[...], rs2_ref[...]

    rq = _rms(_dot(h, wq_ref[...]), gq_ref[...]).astype(BF16)
    _head_norm_rope(_dot(rq, wuq_ref[...]), gqn_ref[...], c_t, s1_t, s2_t, mla_scale, q_ref)
    rkv = _rms(_dot(h, wkv_ref[...]), gkv_ref[...]).astype(BF16)
    pkr = _dot(h, wkr_ref[...]).astype(BF16)
    kin = jnp.concatenate([rkv, pkr], axis=-1)
    _head_norm_rope(_dot(kin, wk_ref[...]), gkn_ref[...], c_t, s1_t, s2_t, 1.0, k_ref)
    v_ref[...] = (_dot(rkv, wv_ref[...]) + vone_ref[...]).astype(BF16)

    def na_norm(w_ref, g_ref, scale):
        p = _dot(h, w_ref[...])
        sq_hi, sq_lo = _split(p * p)
        ss = (_dot(sq_hi, e64_ref[...]) + _dot(sq_lo, e64_ref[...])) * (1.0 / NA_DIM)
        return (p * lax.rsqrt(ss + EPS) * (g_ref[...] * scale)).astype(BF16)

    nq_ref[...] = na_norm(wnq_ref, gnq_ref, na_scale)
    nk_ref[...] = na_norm(wnk_ref, gnk_ref, 1.0)
    nv_ref[...] = _dot(h, wnv_ref[...]).astype(BF16)

    u_ref[...] = _dot(h, wu_ref[...])
    gate_ref[...] = jax.nn.sigmoid(_dot(h, wg_ref[...])).astype(BF16)


def _pre_call(x, mods, n_lat, lw, rope):
    L, d = x.shape
    tm = _pick(L, (320, 256, 128))
    row = lambda w: pl.BlockSpec((tm, w), lambda i: (i, 0))
    weights = [lw[k] for k in ("g1", "wq", "wkv", "wkr", "wnq", "wnk", "wnv", "wu", "wg", "gq", "gkv", "wuq",
                               "wk", "wv", "vone", "gqn", "gkn")]
    tail = [lw["e64"], lw["gnq"], lw["gnk"]]
    in_specs = ([row(d), _const_spec(mods.shape)] + [_const_spec(w.shape) for w in weights]
                + [row(LANES)] * 3 + [_const_spec(w.shape) for w in tail])
    outs = [(HEAD_PAD, BF16)] * 3 + [(NA_WIDTH, BF16)] * 3 + [(S5_WIDTH, F32), (N_BRANCH * d, BF16)]
    return pl.pallas_call(
        functools.partial(_pre_kernel, n_lat, tm, MLA_QK ** -0.5 * math.log2(math.e), NA_DIM ** -0.5),
        grid=(L // tm,),
        in_specs=in_specs,
        out_specs=[row(w) for w, _ in outs],
        out_shape=[jax.ShapeDtypeStruct((L, w), t) for w, t in outs],
        compiler_params=_params(("arbitrary",)),
        name="pre",
    )(x, mods, *weights, *rope, *tail)


def _mla_kernel(tq, tk, n_chunks, q_ref, k_ref, v_ref, o_ref, m_ref, acc_ref, sa_ref, sb_ref):
    m_ref[...] = jnp.full_like(m_ref, NEG)
    acc_ref[...] = jnp.zeros_like(acc_ref)
    q = q_ref[...]

    def scores(c):
        off = pl.multiple_of(c * tk, tk)
        return _dot_t(q, k_ref[pl.ds(off, tk), :])

    def absorb(s, c):
        off = pl.multiple_of(c * tk, tk)
        m_prev = m_ref[...]
        m_new = jnp.maximum(m_prev, jnp.max(s, axis=-1, keepdims=True))
        p = jnp.exp2(s - jnp.concatenate([m_new] * (tk // LANES), axis=1))
        acc_ref[...] = jnp.exp2(m_prev - m_new) * acc_ref[...] + _dot(p.astype(BF16), v_ref[pl.ds(off, tk), :])
        m_ref[...] = m_new

    sa_ref[...] = scores(0)

    def body(i, carry):
        c = 2 * i
        sb_ref[...] = scores(c + 1)
        absorb(sa_ref[...], c)
        sa_ref[...] = scores(c + 2)
        absorb(sb_ref[...], c + 1)
        return carry

    lax.fori_loop(0, (n_chunks - 1) // 2, body, 0)
    if n_chunks % 2 == 0:
        sb_ref[...] = scores(n_chunks - 1)
        absorb(sa_ref[...], n_chunks - 2)
        absorb(sb_ref[...], n_chunks - 1)
    else:
        absorb(sa_ref[...], n_chunks - 1)
    acc = acc_ref[...]
    o_ref[...] = (acc * pl.reciprocal(acc[:, MLA_V:MLA_V + 1], approx=False)).astype(o_ref.dtype)


def _mla_call(q, k, v, n_lat):
    L = q.shape[0]
    tq = _pick(n_lat, (1024, 512, 256))
    tk = _pick(L, (1280, 768, 512, 256))
    kv_spec = pl.BlockSpec((L, LANES), lambda h, i: (0, h))
    return pl.pallas_call(
        functools.partial(_mla_kernel, tq, tk, L // tk),
        grid=(MLA_HEADS, n_lat // tq),
        in_specs=[pl.BlockSpec((tq, LANES), lambda h, i: (i, h)), kv_spec, kv_spec],
        out_specs=pl.BlockSpec((tq, LANES), lambda h, i: (i, h)),
        out_shape=jax.ShapeDtypeStruct((n_lat, HEAD_PAD), BF16),
        scratch_shapes=[pltpu.VMEM((tq, LANES), F32), pltpu.VMEM((tq, LANES), F32),
                        pltpu.VMEM((tq, tk), F32), pltpu.VMEM((tq, tk), F32)],
        compiler_params=_params(("arbitrary", "arbitrary")),
        name="mla",
    )(q, k, v)


def _softmax_pv(parts):
    m = parts[0][0].max(axis=-1, keepdims=True)
    for s, _ in parts[1:]:
        m = jnp.maximum(m, s.max(axis=-1, keepdims=True))
    num, den = 0.0, 0.0
    for s, v in parts:
        p = jnp.exp(s - m)
        den = den + p.sum(axis=-1, keepdims=True)
        num = num + _dot(p.astype(BF16), v)
    return num * pl.reciprocal(den, approx=False)


def _na_kernel(n_lat, n_ctx, n_blocks, q_ref, k_ref, v_ref, bias_ref, o_ref):
    b = pl.program_id(1)
    wtok = NA_WROWS * GRID_W
    row0 = jnp.clip(b * NA_QROWS - NA_KH // 2, 0, n_lat // GRID_W - NA_WROWS)
    off = pl.multiple_of(row0 * GRID_W, GRID_W)
    q = q_ref[...]
    kw, vw = k_ref[pl.ds(off, wtok), :], v_ref[pl.ds(off, wtok), :]
    kc, vc = k_ref[n_lat:n_lat + n_ctx, :], v_ref[n_lat:n_lat + n_ctx, :]
    lane = lax.broadcasted_iota(jnp.int32, (1, LANES), 1)
    out = jnp.zeros(q.shape, F32)
    for j in range(LANES // NA_DIM):
        head = (lane >= j * NA_DIM) & (lane < (j + 1) * NA_DIM)
        qh = jnp.where(head, q, jnp.zeros_like(q))
        s_win = _dot_t(qh, kw) + bias_ref[j]
        s_ctx = _dot_t(qh, kc)
        out = jnp.where(head, _softmax_pv([(s_ctx, vc), (s_win, vw)]), out)
    o_ref[...] = out.astype(o_ref.dtype)


def _na_bias(rpb, n_rows):
    n_blocks = n_rows // NA_QROWS
    cols = np.arange(GRID_W)
    c0 = np.clip(cols - NA_KW // 2, 0, GRID_W - NA_KW)
    col_ok = (cols[None, :] >= c0[:, None]) & (cols[None, :] < c0[:, None] + NA_KW)
    dc = cols[None, :] - cols[:, None] + NA_KW - 1
    pick_c = (col_ok[:, :, None] & (dc[:, :, None] == np.arange(2 * NA_KW - 1))).astype(np.float32)
    pick_r = np.zeros((3, NA_QROWS, NA_WROWS, 2 * NA_KH - 1), np.float32)
    row_oks = []
    for v, b in enumerate((0, 1, n_blocks - 1)):
        row0 = min(max(b * NA_QROWS - NA_KH // 2, 0), n_rows - NA_WROWS)
        qr = b * NA_QROWS + np.arange(NA_QROWS)
        kr = row0 + np.arange(NA_WROWS)
        r0 = np.clip(qr - NA_KH // 2, 0, n_rows - NA_KH)
        row_ok = (kr[None, :] >= r0[:, None]) & (kr[None, :] < r0[:, None] + NA_KH)
        dr = kr[None, :] - qr[:, None] + NA_KH - 1
        pick_r[v] = row_ok[:, :, None] & (dr[:, :, None] == np.arange(2 * NA_KH - 1))
        row_oks.append(row_ok)
    hi = lax.Precision.HIGHEST
    by_col = jnp.einsum("hrd,qkd->hrqk", rpb.astype(F32), pick_c, precision=hi)
    vals = jnp.einsum("vabr,hrqk->vhaqbk", pick_r, by_col, precision=hi)
    ok = np.stack(row_oks)[:, None, :, None, :, None] & col_ok[None, None, None, :, None, :]
    vals = jnp.where(ok, vals, NEG)
    return vals.reshape(3, rpb.shape[0], NA_QROWS * GRID_W, NA_WROWS * GRID_W)


def _na_call(nq, nk, nv, bias, n_lat, n_ctx):
    L = nq.shape[0]
    tq = NA_QROWS * GRID_W
    n_blocks = n_lat // tq
    hp = LANES // NA_DIM
    kv_spec = pl.BlockSpec((L, LANES), lambda h, b: (0, h))

    def bias_map(h, b):
        return (jnp.where(b == 0, 0, jnp.where(b == n_blocks - 1, 2, 1)), h, 0, 0)

    return pl.pallas_call(
        functools.partial(_na_kernel, n_lat, n_ctx, n_blocks),
        grid=(NA_HEADS // hp, n_blocks),
        in_specs=[pl.BlockSpec((tq, LANES), lambda h, b: (b, h)), kv_spec, kv_spec,
                  pl.BlockSpec((None, hp, tq, NA_WROWS * GRID_W), bias_map)],
        out_specs=pl.BlockSpec((tq, LANES), lambda h, b: (b, h)),
        out_shape=jax.ShapeDtypeStruct((n_lat, NA_WIDTH), BF16),
        compiler_params=_params(("arbitrary", "arbitrary")),
        name="na",
    )(nq, nk, nv, bias)


def _ctx_kernel(q_ref, k_ref, v_ref, nq_ref, nk_ref, nv_ref, oa_ref, ob_ref):
    for h in range(MLA_HEADS):
        sl = slice(h * LANES, (h + 1) * LANES)
        s = _dot_t(q_ref[:, sl], k_ref[:, sl])
        p = jnp.exp2(s - s.max(axis=-1, keepdims=True))
        acc = _dot(p.astype(BF16), v_ref[:, sl])
        oa_ref[:, sl] = (acc * pl.reciprocal(acc[:, MLA_V:MLA_V + 1], approx=False)).astype(oa_ref.dtype)
    nq, nk, nv = nq_ref[...], nk_ref[...], nv_ref[...]
    outs = []
    for h in range(NA_HEADS):
        sl = slice(h * NA_DIM, (h + 1) * NA_DIM)
        outs.append(_softmax_pv([(_dot_t(nq[:, sl], nk[:, sl]), nv[:, sl])]))
    ob_ref[...] = jnp.concatenate(outs, axis=-1).astype(ob_ref.dtype)


def _ctx_call(q, k, v, nq, nk, nv, n_lat, n_ctx):
    blk = n_lat // n_ctx
    a_spec = pl.BlockSpec((n_ctx, HEAD_PAD), lambda i: (blk, 0))
    b_spec = pl.BlockSpec((n_ctx, NA_WIDTH), lambda i: (blk, 0))
    return pl.pallas_call(
        _ctx_kernel,
        grid=(1,),
        in_specs=[a_spec] * 3 + [b_spec] * 3,
        out_specs=[pl.BlockSpec((n_ctx, HEAD_PAD), lambda i: (0, 0)),
                   pl.BlockSpec((n_ctx, NA_WIDTH), lambda i: (0, 0))],
        out_shape=[jax.ShapeDtypeStruct((n_ctx, HEAD_PAD), BF16), jax.ShapeDtypeStruct((n_ctx, NA_WIDTH), BF16)],
        compiler_params=_params(("arbitrary",)),
        name="ctx_attn",
    )(q, k, v, nq, nk, nv)


S5_SG = LANES // S5_GCH
S5_SW = S5_SG * S5_STATE


def _s5prep_kernel(are_ref, aim_ref, ldt_ref, bre_ref, bim_ref, cre_ref, cim_ref,
                   arec_ref, aimc_ref, ldtc_ref, crec_ref, cimc_ref,
                   d_ref, bbr_ref, bbi_ref, wcr_ref, wci_ref, atr_ref, ati_ref):
    for ref in (d_ref, bbr_ref, bbi_ref, wcr_ref, wci_ref):
        ref[...] = jnp.zeros_like(ref)
    for g in range(S5_SG):
        rows, lanes = slice(g * S5_GCH, (g + 1) * S5_GCH), slice(g * S5_STATE, (g + 1) * S5_STATE)
        a_re, a_im = are_ref[g], aim_ref[g]
        dt = jnp.exp(ldt_ref[g])
        steps = lax.broadcasted_iota(jnp.int32, (S5_T + 1, S5_STATE), 0).astype(F32)
        mag = jnp.exp(a_re * dt * steps)
        p_re, p_im = mag * jnp.cos(a_im * dt * steps), mag * jnp.sin(a_im * dt * steps)
        nr, ni = p_re[1:2] - 1.0, p_im[1:2]
        den = 1.0 / (a_re * a_re + a_im * a_im)
        f_re, f_im = (nr * a_re + ni * a_im) * den, (ni * a_re - nr * a_im) * den
        b_re, b_im = bre_ref[g], bim_ref[g]
        bb_re, bb_im = f_re * b_re - f_im * b_im, f_re * b_im + f_im * b_re
        c_re, c_im = cre_ref[g], cim_ref[g]
        pw = [(p_re[e:e + 1], p_im[e:e + 1]) for e in range(S5_T + 1)]
        w_re = jnp.concatenate([c_re * r - c_im * i for r, i in pw[:S5_T]], axis=0)
        w_im = jnp.concatenate([c_re * i + c_im * r for r, i in pw[:S5_T]], axis=0)
        kt = _dot3_t(bb_re, w_re) - _dot3_t(bb_im, w_im)
        for e in range(S5_T):
            r, i = pw[e]
            d_ref[e, rows, rows] = kt[:, e * S5_GCH:(e + 1) * S5_GCH].astype(d_ref.dtype)
            bbr_ref[e, rows, lanes] = (r * bb_re - i * bb_im).astype(bbr_ref.dtype)
            bbi_ref[e, rows, lanes] = (r * bb_im + i * bb_re).astype(bbi_ref.dtype)
        atr_ref[:, lanes], ati_ref[:, lanes] = pw[S5_T]
        ac_re, ac_im = arec_ref[g], aimc_ref[g]
        dtc = jnp.exp(ldtc_ref[g])
        steps_c = lax.broadcasted_iota(jnp.int32, (S5_STATE, S5_T), 1).astype(F32) + 1.0
        mag_c = jnp.exp(ac_re * dtc * steps_c)
        q_re, q_im = mag_c * jnp.cos(ac_im * dtc * steps_c), mag_c * jnp.sin(ac_im * dtc * steps_c)
        ct_re, ct_im = crec_ref[g], cimc_ref[g]
        for e in range(S5_T):
            r, i = q_re[:, e:e + 1], q_im[:, e:e + 1]
            wcr_ref[e, lanes, rows] = (ct_re * r - ct_im * i).astype(wcr_ref.dtype)
            wci_ref[e, lanes, rows] = (-(ct_re * i + ct_im * r)).astype(wci_ref.dtype)


def _s5prep_call(a_re, a_im, log_dt, b_re, b_im, c_re, c_im):
    nd, g, p = a_re.shape
    cg = b_re.shape[-1]
    ns = g // S5_SG
    spec = lambda *s: pl.BlockSpec((None, S5_SG) + s, lambda d, j: (d, j) + (0,) * len(s))
    out = lambda *s: pl.BlockSpec((None, None) + s, lambda d, j: (d, j) + (0,) * len(s))
    sds = lambda t, *s: jax.ShapeDtypeStruct((nd, ns) + s, t)
    tr = lambda t: jnp.swapaxes(t, -1, -2)
    ldt = jnp.broadcast_to(log_dt[:, :, None, None], (nd, g, 1, p))
    return pl.pallas_call(
        _s5prep_kernel,
        grid=(nd, ns),
        in_specs=[spec(1, p)] * 3 + [spec(cg, p)] * 4 + [spec(p, 1)] * 3 + [spec(p, cg)] * 2,
        out_specs=[out(S5_T, LANES, LANES), out(S5_T, LANES, S5_SW), out(S5_T, LANES, S5_SW),
                   out(S5_T, S5_SW, LANES), out(S5_T, S5_SW, LANES), out(1, S5_SW), out(1, S5_SW)],
        out_shape=[sds(BF16, S5_T, LANES, LANES), sds(BF16, S5_T, LANES, S5_SW), sds(BF16, S5_T, LANES, S5_SW),
                   sds(BF16, S5_T, S5_SW, LANES), sds(BF16, S5_T, S5_SW, LANES), sds(F32, 1, S5_SW),
                   sds(F32, 1, S5_SW)],
        compiler_params=_params(("arbitrary", "arbitrary")),
        name="s5_prep",
    )(a_re[:, :, None, :], a_im[:, :, None, :], ldt, tr(b_re), tr(b_im), c_re, c_im,
      a_re[:, :, :, None], a_im[:, :, :, None], tr(ldt), tr(c_re), tr(c_im))


def _chunk_rows(u_ref, kb):
    return [u_ref[pl.ds(t, kb, stride=S5_T), :].astype(BF16) for t in range(S5_T)]


def _s5a_kernel(kb, u_ref, br_ref, bi_ref, ore_ref, oim_ref):
    x = _chunk_rows(u_ref, kb)
    x2 = [jnp.concatenate(x[t:t + 2], axis=1) for t in range(0, S5_T, 2)]
    for d in range(br_ref.shape[0]):
        ahead = lambda t: S5_T - 1 - t if d == 0 else t
        pair = lambda ref, t: jnp.concatenate([ref[d, ahead(t)], ref[d, ahead(t + 1)]], axis=0)
        ore_ref[d] = sum(_dot(x2[t // 2], pair(br_ref, t)) for t in range(0, S5_T, 2))
        oim_ref[d] = sum(_dot(x2[t // 2], pair(bi_ref, t)) for t in range(0, S5_T, 2))


def _s5_blocks(L):
    nch = L // S5_T
    return nch, _pick(nch, (208, 144, 80, 72, 40, 16, 8))


def _s5a_call(u, bcr, bci):
    L = u.shape[0]
    nd, ns, _, n, ws = bcr.shape
    nch, kb = _s5_blocks(L)
    wspec = pl.BlockSpec((nd, None, S5_T, n, ws), lambda s, r: (0, s, 0, 0, 0))
    ospec = pl.BlockSpec((nd, kb, ws), lambda s, r: (0, r, s))
    return pl.pallas_call(
        functools.partial(_s5a_kernel, kb),
        grid=(ns, nch // kb),
        in_specs=[pl.BlockSpec((kb * S5_T, LANES), lambda s, r: (r, s)), wspec, wspec],
        out_specs=[ospec, ospec],
        out_shape=[jax.ShapeDtypeStruct((nd, nch, ns * ws), F32)] * 2,
        compiler_params=_params(("arbitrary", "arbitrary")),
        name="s5_chunk_inputs",
    )(u, bcr, bci)


def _s5scan_kernel(n_lat_ch, n_ctx_ch, bre_ref, bim_ref, atr_ref, ati_ref, sre_ref, sim_ref):
    d = pl.program_id(0)
    a_re, a_im = atr_ref[...], ati_ref[...]

    def segment(base, count, carry):
        def step(i, st):
            s_re, s_im = st
            k = base + jnp.where(d == 0, i, count - 1 - i)
            sre_ref[k] = s_re
            sim_ref[k] = s_im
            return (a_re * s_re - a_im * s_im + bre_ref[k], a_re * s_im + a_im * s_re + bim_ref[k])
        return lax.fori_loop(0, count, step, carry)

    zero = jnp.zeros(a_re.shape, F32)
    carry = segment(n_lat_ch, n_ctx_ch, (zero, zero))
    segment(0, n_lat_ch, carry)


def _s5scan_call(b_re, b_im, atr, ati, n_lat_ch, n_ctx_ch):
    nd, nch, w = b_re.shape
    shp = (nd, nch, w // LANES, LANES)
    spec = pl.BlockSpec((None, nch, 8, LANES), lambda d, j: (d, 0, j, 0))
    aspec = pl.BlockSpec((None, 8, LANES), lambda d, j: (d, j, 0))
    return pl.pallas_call(
        functools.partial(_s5scan_kernel, n_lat_ch, n_ctx_ch),
        grid=(nd, w // LANES // 8),
        in_specs=[spec, spec, aspec, aspec],
        out_specs=[spec, spec],
        out_shape=[jax.ShapeDtypeStruct(shp, F32)] * 2,
        compiler_params=_params(("arbitrary", "arbitrary")),
        name="s5_scan",
    )(b_re.reshape(shp), b_im.reshape(shp), atr.reshape(nd, w // LANES, LANES), ati.reshape(nd, w // LANES, LANES))


def _s5c_kernel(kb, u_ref, sre_ref, sim_ref, d_ref, wr_ref, wi_ref, y_ref, acc_ref):
    nd = d_ref.shape[0]
    x = jnp.concatenate(_chunk_rows(u_ref, kb), axis=0)
    s_re = [sre_ref[d].astype(BF16) for d in range(nd)]
    s_im = [sim_ref[d].astype(BF16) for d in range(nd)]
    for t in range(0, S5_T, 2):
        later = lambda d, tt: tt if d == 0 else S5_T - 1 - tt
        both = lambda ref, d: jnp.concatenate([ref[d, later(d, t)], ref[d, later(d, t + 1)]], axis=1)
        r = sum(_dot(s_re[d], both(wr_ref, d)) + _dot(s_im[d], both(wi_ref, d)) for d in range(nd))
        acc_ref[t * kb:(t + 1) * kb, :] = r[:, :LANES]
        acc_ref[(t + 1) * kb:(t + 2) * kb, :] = r[:, LANES:]
    for e in range(0, S5_T, 2):
        n, n1 = (S5_T - e) * kb, (S5_T - e - 1) * kb
        both = lambda d: jnp.concatenate([d_ref[d, e], d_ref[d, e + 1]], axis=1)
        fwd = _dot(x[:n], both(0))
        acc_ref[e * kb:, :] += fwd[:, :LANES]
        acc_ref[(e + 1) * kb:, :] += fwd[:n1, LANES:]
        bwd = _dot(x[e * kb:], both(1))
        acc_ref[:n, :] += bwd[:, :LANES]
        acc_ref[:n1, :] += bwd[kb:, LANES:]
    for t in range(S5_T):
        y_ref[pl.ds(t, kb, stride=S5_T), :] = acc_ref[t * kb:(t + 1) * kb, :]


def _s5c_call(u, s_re, s_im, dm, wcr, wci):
    L = u.shape[0]
    nd, ns = dm.shape[:2]
    ws = wcr.shape[3]
    nch, kb = _s5_blocks(L)
    uspec = pl.BlockSpec((kb * S5_T, LANES), lambda s, r: (r, s))
    sspec = pl.BlockSpec((nd, kb, ws), lambda s, r: (0, r, s))
    wspec = pl.BlockSpec((nd, None, S5_T, ws, LANES), lambda s, r: (0, s, 0, 0, 0))
    return pl.pallas_call(
        functools.partial(_s5c_kernel, kb),
        grid=(ns, nch // kb),
        in_specs=[uspec, sspec, sspec,
                  pl.BlockSpec((nd, None, S5_T, LANES, LANES), lambda s, r: (0, s, 0, 0, 0)), wspec, wspec],
        out_specs=uspec,
        out_shape=jax.ShapeDtypeStruct(u.shape, F32),
        scratch_shapes=[pltpu.VMEM((kb * S5_T, LANES), F32)],
        compiler_params=_params(("arbitrary", "arbitrary")),
        name="s5_outputs",
    )(u, s_re, s_im, dm, wcr, wci)


def _s5_mix(u, ops, n_lat, n_ctx):
    dm, bcr, bci, wcr, wci, atr, ati = ops
    nd = dm.shape[0]
    b_re, b_im = _s5a_call(u, bcr, bci)
    s_re, s_im = _s5scan_call(b_re, b_im, atr.reshape(nd, -1), ati.reshape(nd, -1), n_lat // S5_T, n_ctx // S5_T)
    return _s5c_call(u, s_re.reshape(b_re.shape), s_im.reshape(b_im.shape), dm, wcr, wci)


def _gelu_tanh(x):
    return 0.5 * x * (1.0 + jnp.tanh(math.sqrt(2.0 / math.pi) * (x + 0.044715 * (x * x * x))))


def _post_kernel(n_lat, tm, moe,
                 x_ref, mod_ref, oa_ref, ob_ref, ys_ref, u_ref, gate_ref,
                 wa_ref, wb_ref, wc_ref, wglu_ref, bglu_ref, d_ref, wo_ref, g2_ref, *rest):
    i = pl.program_id(0)
    d = x_ref.shape[-1]
    g = _gelu_tanh(d_ref[...] * u_ref[...] + ys_ref[...])
    oc = (g * jax.nn.sigmoid(_dot(g.astype(BF16), wglu_ref[...]) + bglu_ref[...])).astype(BF16)
    gate = gate_ref[...].astype(F32)
    mix = (gate[:, :d] * _dot(oa_ref[...], wa_ref[...]) + gate[:, d:2 * d] * _dot(ob_ref[...], wb_ref[...])
           + gate[:, 2 * d:] * _dot(oc, wc_ref[...]))
    x_new = x_ref[...] + _row_select(i, tm, n_lat, mod_ref, 2) * _dot(mix.astype(BF16), wo_ref[...])
    h2 = (_rms(x_new, g2_ref[...]) * (1.0 + _row_select(i, tm, n_lat, mod_ref, 4))
          + _row_select(i, tm, n_lat, mod_ref, 3))
    if not moe:
        xo_ref, h2_ref = rest
    else:
        wr_ref, tri_ref, xo_ref, h2_ref, route_ref, count_ref, carry_ref = rest
        logits = _dot3(h2, wr_ref[...])
        lane = lax.broadcasted_iota(jnp.int32, logits.shape, 1)
        logits = jnp.where(lane < N_EXPERTS, logits, NEG)
        m1 = logits.max(axis=-1, keepdims=True)
        i1 = jnp.where(logits == m1, lane, LANES).min(axis=-1, keepdims=True)
        rest_l = jnp.where(lane == i1, NEG, logits)
        m2 = rest_l.max(axis=-1, keepdims=True)
        i2 = jnp.where(rest_l == m2, lane, LANES).min(axis=-1, keepdims=True)
        e2 = jnp.exp(m2 - m1)
        w1 = 1.0 / (1.0 + e2)

        @pl.when(i == 0)
        def _():
            carry_ref[...] = jnp.zeros_like(carry_ref)

        chosen = jnp.where((lane == i1) | (lane == i2), 1.0, 0.0)
        before = _dot(tri_ref[...], chosen.astype(BF16)) + carry_ref[...]
        r1 = jnp.sum(jnp.where(lane == i1, before, 0.0), axis=-1, keepdims=True)
        r2 = jnp.sum(jnp.where(lane == i2, before, 0.0), axis=-1, keepdims=True)
        total = carry_ref[...] + jnp.sum(chosen, axis=0, keepdims=True)
        carry_ref[...] = total
        count_ref[...] = jnp.broadcast_to(total, count_ref.shape)
        cols = [i1.astype(F32), i2.astype(F32), r1, r2, w1, e2 * w1]
        route = jnp.zeros(logits.shape, F32)
        for j, col in enumerate(cols):
            route = jnp.where(lane == j, col, route)
        route_ref[...] = route
    xo_ref[...] = x_new
    h2_ref[...] = h2.astype(h2_ref.dtype)


def _post_call(x, mods, o_a, o_b, ys, u, gate, lw, n_lat, moe):
    L, d = x.shape
    tm = _pick(L, (320, 256, 128))
    row = lambda w: pl.BlockSpec((tm, w), lambda i: (i, 0))
    weights = [lw[k] for k in ("wa", "wb", "wc", "wglu", "bglu", "d", "wo", "g2")]
    if moe:
        tri = jnp.asarray(np.tril(np.ones((tm, tm), np.float32), -1), BF16)
        weights += [lw["wrouter"], tri]
        outs = [(L, d, F32), (L, d, F32), (L, LANES, F32), (8, LANES, F32)]
        out_specs = [row(d), row(d), row(LANES), pl.BlockSpec((8, LANES), lambda i: (0, 0))]
        scratch = [pltpu.VMEM((1, LANES), F32)]
    else:
        outs = [(L, d, F32), (L, d, BF16)]
        out_specs = [row(d), row(d)]
        scratch = []
    return pl.pallas_call(
        functools.partial(_post_kernel, n_lat, tm, moe),
        grid=(L // tm,),
        in_specs=([row(d), _const_spec(mods.shape), row(HEAD_PAD), row(NA_WIDTH), row(S5_WIDTH), row(S5_WIDTH),
                   row(N_BRANCH * d)] + [_const_spec(w.shape) for w in weights]),
        out_specs=out_specs,
        out_shape=[jax.ShapeDtypeStruct((r, w), t) for r, w, t in outs],
        scratch_shapes=scratch,
        compiler_params=_params(("arbitrary",)),
        name="post_moe" if moe else "post",
    )(x, mods, o_a, o_b, ys, u, gate, *weights)


def _swiglu(h, wg_ref, wu_ref, wd_ref):
    a = _dot(h, wg_ref[...])
    act = (a * jax.nn.sigmoid(a) * _dot(h, wu_ref[...])).astype(BF16)
    return _dot(act, wd_ref[...])


def _ffn_kernel(n_lat, tm, x_ref, h_ref, mod_ref, wg_ref, wu_ref, wd_ref, o_ref):
    f = _swiglu(h_ref[...], wg_ref, wu_ref, wd_ref)
    o_ref[...] = x_ref[...] + _row_select(pl.program_id(0), tm, n_lat, mod_ref, 5) * f


def _ffn_call(x, h2, mods, wg, wu, wd, n_lat):
    L, d = x.shape
    tm = _pick(L, (640, 256, 128))
    row = lambda w: pl.BlockSpec((tm, w), lambda i: (i, 0))
    return pl.pallas_call(
        functools.partial(_ffn_kernel, n_lat, tm),
        grid=(L // tm,),
        in_specs=[row(d), row(d), _const_spec(mods.shape), _const_spec(wg.shape), _const_spec(wu.shape),
                  _const_spec(wd.shape)],
        out_specs=row(d),
        out_shape=jax.ShapeDtypeStruct((L, d), F32),
        compiler_params=_params(("arbitrary",)),
        name="ffn",
    )(x, h2, mods, wg, wu, wd)


MOE_TS = 512


def _row_copy(src, i, dst, j, sem):
    return pltpu.make_async_copy(src.at[pl.ds(i, 1)], dst.at[pl.ds(j, 1)], sem)


ROW_DMA_UNROLL = 8


def _dispatch_kernel(tb, n_tok, slot_ref, h_ref, xs_in, xs_hbm, sem):
    del xs_in
    base = pl.program_id(0) * tb

    def issue(t, carry):
        for k in range(TOP_K):
            _row_copy(h_ref, t, xs_hbm, slot_ref[k * n_tok + base + t], sem).start(priority=k % 2)
        return carry

    def drain(t, carry):
        for k in range(TOP_K):
            _row_copy(h_ref, 0, xs_hbm, 0, sem).wait()
        return carry

    lax.fori_loop(0, tb, issue, 0, unroll=ROW_DMA_UNROLL)
    lax.fori_loop(0, tb, drain, 0, unroll=ROW_DMA_UNROLL)


def _dispatch_call(slots, h2, n_slots):
    L, d = h2.shape
    tb = _pick(L, (640, 256, 128))
    xs0 = jnp.zeros((n_slots, d), h2.dtype)
    return pl.pallas_call(
        functools.partial(_dispatch_kernel, tb, L),
        grid_spec=pltpu.PrefetchScalarGridSpec(
            num_scalar_prefetch=1, grid=(L // tb,),
            in_specs=[pl.BlockSpec((tb, d), lambda i, s: (i, 0)), pl.BlockSpec(memory_space=pl.ANY)],
            out_specs=pl.BlockSpec(memory_space=pl.ANY),
            scratch_shapes=[pltpu.SemaphoreType.DMA(())]),
        out_shape=jax.ShapeDtypeStruct((n_slots, d), h2.dtype),
        input_output_aliases={2: 0},
        compiler_params=_params(("arbitrary",)),
        name="moe_dispatch",
    )(slots, h2, xs0)


def _expert_kernel(te_ref, nu_ref, x_ref, wg_ref, wu_ref, wd_ref, o_ref):
    used = pl.program_id(0) < nu_ref[0]

    @pl.when(used)
    def _():
        o_ref[...] = _swiglu(x_ref[...].astype(BF16), wg_ref, wu_ref, wd_ref)

    @pl.when(jnp.logical_not(used))
    def _():
        o_ref[...] = jnp.zeros_like(o_ref)


def _expert_call(tile_expert, n_used, xs, wg, wu, wd, layer):
    n_slots, d = xs.shape
    dff = wg.shape[-1]
    once = pl.Buffered(1)
    return pl.pallas_call(
        _expert_kernel,
        grid_spec=pltpu.PrefetchScalarGridSpec(
            num_scalar_prefetch=2, grid=(n_slots // MOE_TS,),
            in_specs=[pl.BlockSpec((MOE_TS, d), lambda j, te, nu: (j, 0)),
                      pl.BlockSpec((None, None, d, dff), lambda j, te, nu: (layer, te[j], 0, 0), pipeline_mode=once),
                      pl.BlockSpec((None, None, d, dff), lambda j, te, nu: (layer, te[j], 0, 0), pipeline_mode=once),
                      pl.BlockSpec((None, None, dff, d), lambda j, te, nu: (layer, te[j], 0, 0), pipeline_mode=once)],
            out_specs=pl.BlockSpec((MOE_TS, d), lambda j, te, nu: (j, 0))),
        out_shape=jax.ShapeDtypeStruct((n_slots, d), F32),
        compiler_params=_params(("arbitrary",)),
        name="moe_experts",
    )(tile_expert, n_used, xs, wg, wu, wd)


def _combine_kernel(n_lat, tb, n_tok, slot_ref, x_ref, route_ref, mod_ref, zs_hbm, o_ref, g_ref, sem):
    i = pl.program_id(0)
    base = i * tb

    def issue(t, carry):
        for k in range(TOP_K):
            _row_copy(zs_hbm, slot_ref[k * n_tok + base + t], g_ref.at[k], t, sem).start(priority=k % 2)
        return carry

    def drain(t, carry):
        for k in range(TOP_K):
            _row_copy(zs_hbm, 0, g_ref.at[k], 0, sem).wait()
        return carry

    lax.fori_loop(0, tb, issue, 0, unroll=ROW_DMA_UNROLL)
    lax.fori_loop(0, tb, drain, 0, unroll=ROW_DMA_UNROLL)
    route = route_ref[...]
    f = route[:, 4:5] * g_ref[0] + route[:, 5:6] * g_ref[1]
    o_ref[...] = x_ref[...] + _row_select(i, tb, n_lat, mod_ref, 5) * f


def _combine_call(slots, x, route, mods, zs, n_lat):
    L, d = x.shape
    tb = _pick(L, (320, 256, 128))
    row = lambda w: pl.BlockSpec((tb, w), lambda i, s: (i, 0))
    return pl.pallas_call(
        functools.partial(_combine_kernel, n_lat, tb, L),
        grid_spec=pltpu.PrefetchScalarGridSpec(
            num_scalar_prefetch=1, grid=(L // tb,),
            in_specs=[row(d), row(LANES), pl.BlockSpec(mods.shape, lambda i, s: (0, 0, 0)),
                      pl.BlockSpec(memory_space=pl.ANY)],
            out_specs=row(d),
            scratch_shapes=[pltpu.VMEM((TOP_K, tb, d), F32), pltpu.SemaphoreType.DMA(())]),
        out_shape=jax.ShapeDtypeStruct((L, d), F32),
        compiler_params=_params(("arbitrary",)),
        name="moe_combine",
    )(slots, x, route, mods, zs)


def _moe_call(x, h2, route, counts, mods, wg, wu, wd, layer, n_lat):
    L, d = x.shape
    ne = wg.shape[1]
    n_slots = (pl.cdiv(TOP_K * L, MOE_TS) + ne) * MOE_TS
    cnt = counts[0, :ne].astype(jnp.int32)
    size = (cnt + MOE_TS - 1) // MOE_TS * MOE_TS
    ends = jnp.cumsum(size)
    offs = ends - size
    eid = jnp.arange(ne, dtype=jnp.int32)
    slot = lambda e, r: jnp.sum(jnp.where(e[:, None] == eid[None, :], offs[None, :], 0), axis=1) + r
    r = route.astype(jnp.int32)
    slots = jnp.concatenate([slot(r[:, 0], r[:, 2]), slot(r[:, 1], r[:, 3])])
    starts = jnp.arange(n_slots // MOE_TS, dtype=jnp.int32) * MOE_TS
    tile_expert = jnp.minimum(jnp.sum(starts[:, None] >= ends[None, :], axis=1), ne - 1).astype(jnp.int32)
    n_used = (ends[-1:] // MOE_TS).astype(jnp.int32)
    xs = _dispatch_call(slots, h2, n_slots)
    zs = _expert_call(tile_expert, n_used, xs, wg, wu, wd, layer)
    return _combine_call(slots, x, route, mods, zs, n_lat)


def _rope_tables(n_lat, n_ctx):
    t = jnp.arange(n_lat, dtype=jnp.int32)
    n_freq = MLA_ROPE // 4
    inv = ROPE_THETA ** (-jnp.arange(n_freq, dtype=F32) / n_freq)
    ang = jnp.concatenate([(t // GRID_W).astype(F32)[:, None] * inv[None],
                           (t % GRID_W).astype(F32)[:, None] * inv[None]], axis=-1)
    cos, sin = jnp.cos(ang), jnp.sin(ang)
    ones = jnp.ones((n_lat, MLA_NOPE), F32)
    zn = jnp.zeros((n_lat, MLA_NOPE), F32)
    zh = jnp.zeros_like(sin)
    zp = jnp.zeros((n_lat, LANES - MLA_QK), F32)
    c = jnp.concatenate([ones, cos, cos, zp], axis=-1)
    s1 = jnp.concatenate([zn, -sin, zh, zp], axis=-1)
    s2 = jnp.concatenate([zn, zh, sin, zp], axis=-1)
    ctx = lambda a, fill: jnp.concatenate([a, jnp.full((n_ctx, LANES), fill, F32)], axis=0)
    return ctx(c, 1.0), ctx(s1, 0.0), ctx(s2, 0.0)


def _head_pad_cols(w, width):
    r = w.shape[0]
    w = w.reshape(r, MLA_HEADS, width)
    return jnp.pad(w, ((0, 0), (0, 0), (0, LANES - width))).reshape(r, HEAD_PAD)


def _layer_weights(i, p):
    d = D_MODEL
    w_in = p["w_in"][i]
    cuts = np.cumsum((0,) + IN_SPLITS)
    piece = lambda j: w_in[:, cuts[j]:cuts[j + 1]]
    bf = lambda w: w.astype(BF16)
    row = lambda v: v.reshape(1, -1).astype(F32)
    ukv = p["w_mla_ukv"][i].reshape(MLA_KV_RANK, MLA_HEADS, MLA_NOPE + MLA_V)
    wk_nope = _head_pad_cols(ukv[:, :, :MLA_NOPE].reshape(MLA_KV_RANK, -1), MLA_NOPE)
    kr_place = jnp.zeros((LANES, MLA_HEADS, LANES), F32)
    eye = jnp.eye(MLA_ROPE, dtype=F32)
    kr_place = kr_place.at[:MLA_ROPE, :, MLA_NOPE:MLA_QK].set(jnp.broadcast_to(eye[:, None, :], (MLA_ROPE, MLA_HEADS, MLA_ROPE)))
    vone = jnp.zeros((MLA_HEADS, LANES), F32).at[:, MLA_V].set(1.0).reshape(1, HEAD_PAD)
    head_gain = lambda g: jnp.tile(jnp.pad(g, (0, LANES - MLA_QK)), MLA_HEADS).reshape(1, HEAD_PAD)
    e64 = jnp.kron(jnp.eye(NA_HEADS, dtype=F32), jnp.ones((NA_DIM, NA_DIM), F32))
    wa = jnp.pad(p["w_br_mla"][i].reshape(MLA_HEADS, MLA_V, d), ((0, 0), (0, LANES - MLA_V), (0, 0)))
    return {
        "g1": row(p["g_norm1"][i]), "g2": row(p["g_norm2"][i]),
        "wq": bf(piece(0)), "wkv": bf(piece(1)),
        "wkr": bf(jnp.pad(piece(2), ((0, 0), (0, LANES - MLA_ROPE)))),
        "wnq": bf(piece(3)), "wnk": bf(piece(4)), "wnv": bf(piece(5)), "wu": bf(piece(6)), "wg": bf(piece(7)),
        "gq": row(p["g_mla_q"][i]), "gkv": row(p["g_mla_kv"][i]),
        "wuq": bf(_head_pad_cols(p["w_mla_uq"][i], MLA_QK)),
        "wk": bf(jnp.concatenate([wk_nope, kr_place.reshape(LANES, HEAD_PAD)], axis=0)),
        "wv": bf(_head_pad_cols(ukv[:, :, MLA_NOPE:].reshape(MLA_KV_RANK, -1), MLA_V)),
        "vone": vone,
        "gqn": head_gain(p["g_mla_qn"][i]), "gkn": head_gain(p["g_mla_kn"][i]),
        "e64": bf(e64),
        "gnq": jnp.tile(p["g_na_qn"][i], NA_HEADS).reshape(1, -1), "gnk": jnp.tile(p["g_na_kn"][i], NA_HEADS).reshape(1, -1),
        "wa": bf(wa.reshape(HEAD_PAD, d)), "wb": bf(p["w_br_na"][i]), "wc": bf(p["w_br_s5"][i]),
        "wglu": bf(p["w_glu"][i]), "bglu": row(p["b_glu"][i]), "d": row(p["s5_d"][i]), "wo": bf(p["w_out"][i]),
    }


def kernel(x, c, ctx, c_ctx, w_mod, b_mod, g_norm1, g_norm2, w_in, g_mla_q, g_mla_kv, w_mla_uq, w_mla_ukv,
           g_mla_qn, g_mla_kn, g_na_qn, g_na_kn, na_rpb, s5_a_re, s5_a_im, s5_log_dt, s5_b_re, s5_b_im,
           s5_c_re, s5_c_im, s5_d, w_glu, b_glu, w_br_mla, w_br_na, w_br_s5, w_out, w_ffn_gate, w_ffn_up,
           w_ffn_down, w_router, w_exp_gate, w_exp_up, w_exp_down):
    p = dict(w_in=w_in, g_norm1=g_norm1, g_norm2=g_norm2, g_mla_q=g_mla_q, g_mla_kv=g_mla_kv, w_mla_uq=w_mla_uq,
             w_mla_ukv=w_mla_ukv, g_mla_qn=g_mla_qn, g_mla_kn=g_mla_kn, g_na_qn=g_na_qn, g_na_kn=g_na_kn,
             s5_d=s5_d, w_glu=w_glu, b_glu=b_glu, w_br_mla=w_br_mla, w_br_na=w_br_na, w_br_s5=w_br_s5, w_out=w_out)
    assert x.shape[0] == 1 and x.shape[2] == D_MODEL
    n_lat, n_ctx = x.shape[1], ctx.shape[1]
    depth = w_mod.shape[0]
    xs = jnp.concatenate([x[0], ctx[0]], axis=0)

    cvec = jnp.zeros((8, D_MODEL), F32).at[0].set(c[0]).at[1].set(c_ctx)
    mods_all = _mod_call(cvec, w_mod, b_mod)[:, :2, None, :]
    rope = _rope_tables(n_lat, n_ctx)
    experts = (w_exp_gate.astype(BF16), w_exp_up.astype(BF16), w_exp_down.astype(BF16))

    for i in range(depth):
        lw = _layer_weights(i, p)
        mods = mods_all[i]
        q, k, v, nq, nk, nv, u, gate = _pre_call(xs, mods, n_lat, lw, rope)

        o_a = _mla_call(q, k, v, n_lat)
        bias = _na_bias(na_rpb[i], n_lat // GRID_W)
        o_b = _na_call(nq, nk, nv, bias, n_lat, n_ctx)
        o_a_c, o_b_c = _ctx_call(q, k, v, nq, nk, nv, n_lat, n_ctx)
        o_a = jnp.concatenate([o_a, o_a_c], axis=0)
        o_b = jnp.concatenate([o_b, o_b_c], axis=0)

        ops = _s5prep_call(s5_a_re[i], s5_a_im[i], s5_log_dt[i], s5_b_re[i], s5_b_im[i], s5_c_re[i], s5_c_im[i])
        ys = _s5_mix(u, ops, n_lat, n_ctx)

        moe = i % 2 == 1
        j = i // 2
        if moe:
            lw["wrouter"] = jnp.pad(w_router[j], ((0, 0), (0, LANES - N_EXPERTS)))
            xs, h2, route, counts = _post_call(xs, mods, o_a, o_b, ys, u, gate, lw, n_lat, True)
            xs = _moe_call(xs, h2, route, counts, mods, *experts, j, n_lat)
        else:
            xs, h2 = _post_call(xs, mods, o_a, o_b, ys, u, gate, lw, n_lat, False)
            xs = _ffn_call(xs, h2, mods, w_ffn_gate[j].astype(BF16), w_ffn_up[j].astype(BF16),
                           w_ffn_down[j].astype(BF16), n_lat)
    return xs[:n_lat][None]
```

```python
import functools
import math

import jax
import jax.numpy as jnp
import numpy as np
from jax import lax
from jax.experimental import pallas as pl
from jax.experimental.pallas import tpu as pltpu

F32 = jnp.float32
BF16 = jnp.bfloat16

D_MODEL = 1024
DEPTH = 4
GRID_W = 64
EPS = 1e-6
ROPE_THETA = 10000.0

MLA_HEADS = 8
MLA_NOPE = 64
MLA_ROPE = 32
MLA_V = 64
MLA_Q_RANK = 384
MLA_KV_RANK = 256
MLA_QK = MLA_NOPE + MLA_ROPE

NA_HEADS = 8
NA_DIM = 64
NA_KH = 8
NA_KW = 16
NA_WIDTH = NA_HEADS * NA_DIM

S5_GROUPS = 32
S5_GCH = 16
S5_STATE = 64
S5_WIDTH = S5_GROUPS * S5_GCH

N_BRANCH = 3
D_FF = 2816
N_EXPERTS = 8
TOP_K = 2

IN_SPLITS = (MLA_Q_RANK, MLA_KV_RANK, MLA_ROPE, NA_WIDTH, NA_WIDTH, NA_WIDTH, S5_WIDTH, N_BRANCH * D_MODEL)

LANES = 128
HEAD_PAD = MLA_HEADS * LANES
S5_T = 16
NA_QROWS = 4
NA_WROWS = NA_QROWS + NA_KH - 1
VMEM_LIMIT = 56 * 1024 * 1024
NEG = -1e30


def _pick(n, candidates):
    for c in candidates:
        if n % c == 0:
            return c
    raise ValueError(f"no tile in {candidates} divides {n}")


def _const_spec(shape):
    nd = len(shape)
    return pl.BlockSpec(shape, lambda *_: (0,) * nd, pipeline_mode=pl.Buffered(1))


def _params(sem):
    return pltpu.CompilerParams(dimension_semantics=sem, vmem_limit_bytes=VMEM_LIMIT)


def _dot(a, b):
    return jnp.dot(a, b, preferred_element_type=F32)


def _dot_t(a, b):
    return lax.dot_general(a, b, (((1,), (1,)), ((), ())), preferred_element_type=F32)


def _split(x):
    hi = x.astype(BF16)
    lo = (x - hi.astype(F32)).astype(BF16)
    return hi, lo


def _dot3(a, b):
    a_hi, a_lo = _split(a)
    b_hi, b_lo = _split(b)
    return _dot(a_hi, b_hi) + _dot(a_hi, b_lo) + _dot(a_lo, b_hi)


def _dot3_t(a, b):
    a_hi, a_lo = _split(a)
    b_hi, b_lo = _split(b)
    return _dot_t(a_hi, b_hi) + _dot_t(a_hi, b_lo) + _dot_t(a_lo, b_hi)


def _rms(x, g):
    ms = jnp.mean(x * x, axis=-1, keepdims=True)
    return x * lax.rsqrt(ms + EPS) * g


def _row_select(tile, tm, n_lat, mod_ref, idx):
    rows = tile * tm + lax.broadcasted_iota(jnp.int32, (tm, 1), 0)
    lat = mod_ref[0, :, idx * D_MODEL:(idx + 1) * D_MODEL]
    ctx = mod_ref[1, :, idx * D_MODEL:(idx + 1) * D_MODEL]
    return jnp.where(rows < n_lat, lat, ctx)


def _mod_kernel(c_ref, w_ref, b_ref, o_ref):
    c = c_ref[...]
    a = c * jax.nn.sigmoid(c)
    o_ref[...] = _dot3(a, w_ref[...]) + b_ref[...]


def _mod_call(cvec, w_mod, b_mod):
    depth, d, n6 = w_mod.shape
    tn = _pick(n6, (1536, 1024, 512, 128))
    return pl.pallas_call(
        _mod_kernel,
        grid=(depth, n6 // tn),
        in_specs=[
            pl.BlockSpec((8, d), lambda l, j: (0, 0)),
            pl.BlockSpec((None, d, tn), lambda l, j: (l, 0, j)),
            pl.BlockSpec((None, 1, tn), lambda l, j: (l, 0, j)),
        ],
        out_specs=pl.BlockSpec((None, 8, tn), lambda l, j: (l, 0, j)),
        out_shape=jax.ShapeDtypeStruct((depth, 8, n6), F32),
        compiler_params=_params(("arbitrary", "arbitrary")),
        name="mod",
    )(cvec, w_mod, b_mod.reshape(depth, 1, n6))


def _head_norm_rope(xp, gain, c_t, s1_t, s2_t, scale, o_ref, nmax_ref, first, carry_norm):
    lane = lax.broadcasted_iota(jnp.int32, (1, LANES), 1)
    for h in range(MLA_HEADS):
        sl = slice(h * LANES, (h + 1) * LANES)
        xh = xp[:, sl]
        ms = jnp.sum(xh * xh, axis=-1, keepdims=True) * (1.0 / MLA_QK)
        y = xh * lax.rsqrt(ms + EPS) * gain[:, sl]
        y = y * c_t + pltpu.roll(y, LANES - MLA_ROPE // 2, 1) * s1_t + pltpu.roll(y, MLA_ROPE // 2, 1) * s2_t
        if scale != 1.0:
            y = y * scale
        norm = jnp.sqrt(jnp.sum(y * y, axis=-1, keepdims=True))
        tile_max = jnp.broadcast_to(jnp.max(norm, axis=0, keepdims=True), (1, LANES))
        nmax_ref[h:h + 1, :] = jnp.maximum(jnp.where(first, 0.0, nmax_ref[h:h + 1, :]), tile_max)
        if carry_norm:
            y = jnp.where(lane == MLA_QK, -norm, y)
        o_ref[:, sl] = y.astype(o_ref.dtype)


def _pre_kernel(n_lat, tm, mla_scale, na_scale,
                x_ref, mod_ref, g1_ref, wq_ref, wkv_ref, wkr_ref, wnq_ref, wnk_ref, wnv_ref, wu_ref, wg_ref,
                gq_ref, gkv_ref, wuq_ref, wk_ref, wv_ref, vone_ref, gqn_ref, gkn_ref,
                rc_ref, rs1_ref, rs2_ref, e64_ref, gnq_ref, gnk_ref,
                q_ref, k_ref, v_ref, nq_ref, nk_ref, nv_ref, u_ref, gate_ref, qmax_ref, kmax_ref):
    i = pl.program_id(0)
    sh = _row_select(i, tm, n_lat, mod_ref, 0)
    sc = _row_select(i, tm, n_lat, mod_ref, 1)
    h = (_rms(x_ref[...], g1_ref[...]) * (1.0 + sc) + sh).astype(BF16)

    c_t, s1_t, s2_t = rc_ref[...], rs1_ref[...], rs2_ref[...]

    rq = _rms(_dot(h, wq_ref[...]), gq_ref[...]).astype(BF16)
    _head_norm_rope(_dot(rq, wuq_ref[...]), gqn_ref[...], c_t, s1_t, s2_t, mla_scale, q_ref, qmax_ref, i == 0, True)
    rkv = _rms(_dot(h, wkv_ref[...]), gkv_ref[...]).astype(BF16)
    pkr = _dot(h, wkr_ref[...]).astype(BF16)
    kin = jnp.concatenate([rkv, pkr], axis=-1)
    _head_norm_rope(_dot(kin, wk_ref[...]), gkn_ref[...], c_t, s1_t, s2_t, 1.0, k_ref, kmax_ref, i == 0, False)
    v_ref[...] = (_dot(rkv, wv_ref[...]) + vone_ref[...]).astype(BF16)

    def na_norm(w_ref, g_ref, scale):
        p = _dot(h, w_ref[...])
        sq_hi, sq_lo = _split(p * p)
        ss = (_dot(sq_hi, e64_ref[...]) + _dot(sq_lo, e64_ref[...])) * (1.0 / NA_DIM)
        return (p * lax.rsqrt(ss + EPS) * (g_ref[...] * scale)).astype(BF16)

    nq_ref[...] = na_norm(wnq_ref, gnq_ref, na_scale)
    nk_ref[...] = na_norm(wnk_ref, gnk_ref, 1.0)
    nv_ref[...] = _dot(h, wnv_ref[...]).astype(BF16)

    u_ref[...] = _dot(h, wu_ref[...])
    gate_ref[...] = jax.nn.sigmoid(_dot(h, wg_ref[...])).astype(BF16)


def _pre_call(x, mods, n_lat, lw, rope):
    L, d = x.shape
    tm = _pick(L, (320, 256, 128))
    row = lambda w: pl.BlockSpec((tm, w), lambda i: (i, 0))
    weights = [lw[k] for k in ("g1", "wq", "wkv", "wkr", "wnq", "wnk", "wnv", "wu", "wg", "gq", "gkv", "wuq",
                               "wk", "wv", "vone", "gqn", "gkn")]
    tail = [lw["e64"], lw["gnq"], lw["gnk"]]
    in_specs = ([row(d), _const_spec(mods.shape)] + [_const_spec(w.shape) for w in weights]
                + [row(LANES)] * 3 + [_const_spec(w.shape) for w in tail])
    outs = [(HEAD_PAD, BF16)] * 3 + [(NA_WIDTH, BF16)] * 3 + [(S5_WIDTH, F32), (N_BRANCH * d, BF16)]
    return pl.pallas_call(
        functools.partial(_pre_kernel, n_lat, tm, MLA_QK ** -0.5 * math.log2(math.e), NA_DIM ** -0.5),
        grid=(L // tm,),
        in_specs=in_specs,
        out_specs=[row(w) for w, _ in outs] + [pl.BlockSpec((MLA_HEADS, LANES), lambda i: (0, 0))] * 2,
        out_shape=([jax.ShapeDtypeStruct((L, w), t) for w, t in outs]
                   + [jax.ShapeDtypeStruct((MLA_HEADS, LANES), F32)] * 2),
        compiler_params=_params(("arbitrary",)),
        name="pre",
    )(x, mods, *weights, *rope, *tail)


def _mla_kernel(tq, tk, n_chunks, q_ref, k_ref, v_ref, o_ref, m_ref, acc_ref, sa_ref, sb_ref):
    m_ref[...] = jnp.full_like(m_ref, NEG)
    acc_ref[...] = jnp.zeros_like(acc_ref)
    q = q_ref[...]

    def scores(c):
        off = pl.multiple_of(c * tk, tk)
        return _dot_t(q, k_ref[pl.ds(off, tk), :])

    def absorb(s, c):
        off = pl.multiple_of(c * tk, tk)
        m_prev = m_ref[...]
        m_new = jnp.maximum(m_prev, jnp.max(s, axis=-1, keepdims=True))
        p = jnp.exp2(s - jnp.concatenate([m_new] * (tk // LANES), axis=1))
        acc_ref[...] = jnp.exp2(m_prev - m_new) * acc_ref[...] + _dot(p.astype(BF16), v_ref[pl.ds(off, tk), :])
        m_ref[...] = m_new

    sa_ref[...] = scores(0)

    def body(i, carry):
        c = 2 * i
        sb_ref[...] = scores(c + 1)
        absorb(sa_ref[...], c)
        sa_ref[...] = scores(c + 2)
        absorb(sb_ref[...], c + 1)
        return carry

    lax.fori_loop(0, (n_chunks - 1) // 2, body, 0)
    if n_chunks % 2 == 0:
        sb_ref[...] = scores(n_chunks - 1)
        absorb(sa_ref[...], n_chunks - 2)
        absorb(sb_ref[...], n_chunks - 1)
    else:
        absorb(sa_ref[...], n_chunks - 1)
    acc = acc_ref[...]
    o_ref[...] = (acc * pl.reciprocal(acc[:, MLA_V:MLA_V + 1], approx=False)).astype(o_ref.dtype)


MLA_SHIFT_MAX = 60.0


def _mla_shifted_kernel(tq, tk, n_chunks, kmax_ref, q_ref, k_ref, v_ref, o_ref, ks_ref, acc_ref, sa_ref, sb_ref):
    @pl.when(pl.program_id(1) == 0)
    def _():
        lane = lax.broadcasted_iota(jnp.int32, (1, LANES), 1)
        ks_ref[...] = jnp.where(lane == MLA_QK, kmax_ref[...].astype(BF16), k_ref[...])

    acc_ref[...] = jnp.zeros_like(acc_ref)
    q = q_ref[...]

    def scores(c):
        off = pl.multiple_of(c * tk, tk)
        return _dot_t(q, ks_ref[pl.ds(off, tk), :])

    def absorb(s, c):
        off = pl.multiple_of(c * tk, tk)
        acc_ref[...] += _dot(jnp.exp2(s).astype(BF16), v_ref[pl.ds(off, tk), :])

    sa_ref[...] = scores(0)

    def body(i, carry):
        c = 2 * i
        sb_ref[...] = scores(c + 1)
        absorb(sa_ref[...], c)
        sa_ref[...] = scores(c + 2)
        absorb(sb_ref[...], c + 1)
        return carry

    lax.fori_loop(0, (n_chunks - 1) // 2, body, 0)
    if n_chunks % 2 == 0:
        sb_ref[...] = scores(n_chunks - 1)
        absorb(sa_ref[...], n_chunks - 2)
        absorb(sb_ref[...], n_chunks - 1)
    else:
        absorb(sa_ref[...], n_chunks - 1)
    acc = acc_ref[...]
    o_ref[...] = (acc * pl.reciprocal(acc[:, MLA_V:MLA_V + 1], approx=False)).astype(o_ref.dtype)


def _mla_shifted_call(q, k, v, kmax, n_lat):
    L = q.shape[0]
    tq = _pick(n_lat, (1024, 512, 256))
    tk = _pick(L, (1280, 768, 512, 256))
    kv_spec = pl.BlockSpec((L, LANES), lambda h, i: (0, h))
    return pl.pallas_call(
        functools.partial(_mla_shifted_kernel, tq, tk, L // tk),
        grid=(MLA_HEADS, n_lat // tq),
        in_specs=[pl.BlockSpec((None, 1, LANES), lambda h, i: (h, 0, 0)),
                  pl.BlockSpec((tq, LANES), lambda h, i: (i, h)), kv_spec, kv_spec],
        out_specs=pl.BlockSpec((tq, LANES), lambda h, i: (i, h)),
        out_shape=jax.ShapeDtypeStruct((n_lat, HEAD_PAD), BF16),
        scratch_shapes=[pltpu.VMEM((L, LANES), BF16), pltpu.VMEM((tq, LANES), F32),
                        pltpu.VMEM((tq, tk), F32), pltpu.VMEM((tq, tk), F32)],
        compiler_params=_params(("arbitrary", "arbitrary")),
        name="mla_shifted",
    )(kmax.reshape(MLA_HEADS, 1, LANES), q, k, v)


def _mla_attend(q, k, v, qmax, kmax, n_lat):
    shift = jnp.max(qmax[:, 0] * kmax[:, 0])
    return lax.cond(shift <= MLA_SHIFT_MAX, lambda: _mla_shifted_call(q, k, v, kmax, n_lat),
                    lambda: _mla_call(q, k, v, n_lat))


def _mla_call(q, k, v, n_lat):
    L = q.shape[0]
    tq = _pick(n_lat, (1024, 512, 256))
    tk = _pick(L, (1280, 768, 512, 256))
    kv_spec = pl.BlockSpec((L, LANES), lambda h, i: (0, h))
    return pl.pallas_call(
        functools.partial(_mla_kernel, tq, tk, L // tk),
        grid=(MLA_HEADS, n_lat // tq),
        in_specs=[pl.BlockSpec((tq, LANES), lambda h, i: (i, h)), kv_spec, kv_spec],
        out_specs=pl.BlockSpec((tq, LANES), lambda h, i: (i, h)),
        out_shape=jax.ShapeDtypeStruct((n_lat, HEAD_PAD), BF16),
        scratch_shapes=[pltpu.VMEM((tq, LANES), F32), pltpu.VMEM((tq, LANES), F32),
                        pltpu.VMEM((tq, tk), F32), pltpu.VMEM((tq, tk), F32)],
        compiler_params=_params(("arbitrary", "arbitrary")),
        name="mla",
    )(q, k, v)


def _softmax_pv(parts):
    m = parts[0][0].max(axis=-1, keepdims=True)
    for s, _ in parts[1:]:
        m = jnp.maximum(m, s.max(axis=-1, keepdims=True))
    num, den = 0.0, 0.0
    for s, v in parts:
        p = jnp.exp(s - m)
        den = den + p.sum(axis=-1, keepdims=True)
        num = num + _dot(p.astype(BF16), v)
    return num * pl.reciprocal(den, approx=False)


def _na_kernel(n_lat, n_ctx, n_blocks, q_ref, k_ref, v_ref, bias_ref, o_ref):
    b = pl.program_id(1)
    wtok = NA_WROWS * GRID_W
    row0 = jnp.clip(b * NA_QROWS - NA_KH // 2, 0, n_lat // GRID_W - NA_WROWS)
    off = pl.multiple_of(row0 * GRID_W, GRID_W)
    q = q_ref[...]
    kw, vw = k_ref[pl.ds(off, wtok), :], v_ref[pl.ds(off, wtok), :]
    kc, vc = k_ref[n_lat:n_lat + n_ctx, :], v_ref[n_lat:n_lat + n_ctx, :]
    lane = lax.broadcasted_iota(jnp.int32, (1, LANES), 1)
    out = jnp.zeros(q.shape, F32)
    for j in range(LANES // NA_DIM):
        head = (lane >= j * NA_DIM) & (lane < (j + 1) * NA_DIM)
        qh = jnp.where(head, q, jnp.zeros_like(q))
        s_win = _dot_t(qh, kw) + bias_ref[j]
        s_ctx = _dot_t(qh, kc)
        out = jnp.where(head, _softmax_pv([(s_ctx, vc), (s_win, vw)]), out)
    o_ref[...] = out.astype(o_ref.dtype)


def _na_bias(rpb, n_rows):
    n_blocks = n_rows // NA_QROWS
    cols = np.arange(GRID_W)
    c0 = np.clip(cols - NA_KW // 2, 0, GRID_W - NA_KW)
    col_ok = (cols[None, :] >= c0[:, None]) & (cols[None, :] < c0[:, None] + NA_KW)
    dc = cols[None, :] - cols[:, None] + NA_KW - 1
    pick_c = (col_ok[:, :, None] & (dc[:, :, None] == np.arange(2 * NA_KW - 1))).astype(np.float32)
    pick_r = np.zeros((3, NA_QROWS, NA_WROWS, 2 * NA_KH - 1), np.float32)
    row_oks = []
    for v, b in enumerate((0, 1, n_blocks - 1)):
        row0 = min(max(b * NA_QROWS - NA_KH // 2, 0), n_rows - NA_WROWS)
        qr = b * NA_QROWS + np.arange(NA_QROWS)
        kr = row0 + np.arange(NA_WROWS)
        r0 = np.clip(qr - NA_KH // 2, 0, n_rows - NA_KH)
        row_ok = (kr[None, :] >= r0[:, None]) & (kr[None, :] < r0[:, None] + NA_KH)
        dr = kr[None, :] - qr[:, None] + NA_KH - 1
        pick_r[v] = row_ok[:, :, None] & (dr[:, :, None] == np.arange(2 * NA_KH - 1))
        row_oks.append(row_ok)
    hi = lax.Precision.HIGHEST
    by_col = jnp.einsum("hrd,qkd->hrqk", rpb.astype(F32), pick_c, precision=hi)
    vals = jnp.einsum("vabr,hrqk->vhaqbk", pick_r, by_col, precision=hi)
    ok = np.stack(row_oks)[:, None, :, None, :, None] & col_ok[None, None, None, :, None, :]
    vals = jnp.where(ok, vals, NEG)
    return vals.reshape(3, rpb.shape[0], NA_QROWS * GRID_W, NA_WROWS * GRID_W)


def _na_call(nq, nk, nv, bias, n_lat, n_ctx):
    L = nq.shape[0]
    tq = NA_QROWS * GRID_W
    n_blocks = n_lat // tq
    hp = LANES // NA_DIM
    kv_spec = pl.BlockSpec((L, LANES), lambda h, b: (0, h))

    def bias_map(h, b):
        return (jnp.where(b == 0, 0, jnp.where(b == n_blocks - 1, 2, 1)), h, 0, 0)

    return pl.pallas_call(
        functools.partial(_na_kernel, n_lat, n_ctx, n_blocks),
        grid=(NA_HEADS // hp, n_blocks),
        in_specs=[pl.BlockSpec((tq, LANES), lambda h, b: (b, h)), kv_spec, kv_spec,
                  pl.BlockSpec((None, hp, tq, NA_WROWS * GRID_W), bias_map)],
        out_specs=pl.BlockSpec((tq, LANES), lambda h, b: (b, h)),
        out_shape=jax.ShapeDtypeStruct((n_lat, NA_WIDTH), BF16),
        compiler_params=_params(("arbitrary", "arbitrary")),
        name="na",
    )(nq, nk, nv, bias)


def _ctx_kernel(q_ref, k_ref, v_ref, nq_ref, nk_ref, nv_ref, oa_ref, ob_ref):
    for h in range(MLA_HEADS):
        sl = slice(h * LANES, (h + 1) * LANES)
        s = _dot_t(q_ref[:, sl], k_ref[:, sl])
        p = jnp.exp2(s - s.max(axis=-1, keepdims=True))
        acc = _dot(p.astype(BF16), v_ref[:, sl])
        oa_ref[:, sl] = (acc * pl.reciprocal(acc[:, MLA_V:MLA_V + 1], approx=False)).astype(oa_ref.dtype)
    nq, nk, nv = nq_ref[...], nk_ref[...], nv_ref[...]
    outs = []
    for h in range(NA_HEADS):
        sl = slice(h * NA_DIM, (h + 1) * NA_DIM)
        outs.append(_softmax_pv([(_dot_t(nq[:, sl], nk[:, sl]), nv[:, sl])]))
    ob_ref[...] = jnp.concatenate(outs, axis=-1).astype(ob_ref.dtype)


def _ctx_call(q, k, v, nq, nk, nv, n_lat, n_ctx):
    blk = n_lat // n_ctx
    a_spec = pl.BlockSpec((n_ctx, HEAD_PAD), lambda i: (blk, 0))
    b_spec = pl.BlockSpec((n_ctx, NA_WIDTH), lambda i: (blk, 0))
    return pl.pallas_call(
        _ctx_kernel,
        grid=(1,),
        in_specs=[a_spec] * 3 + [b_spec] * 3,
        out_specs=[pl.BlockSpec((n_ctx, HEAD_PAD), lambda i: (0, 0)),
                   pl.BlockSpec((n_ctx, NA_WIDTH), lambda i: (0, 0))],
        out_shape=[jax.ShapeDtypeStruct((n_ctx, HEAD_PAD), BF16), jax.ShapeDtypeStruct((n_ctx, NA_WIDTH), BF16)],
        compiler_params=_params(("arbitrary",)),
        name="ctx_attn",
    )(q, k, v, nq, nk, nv)


S5_SG = LANES // S5_GCH
S5_SW = S5_SG * S5_STATE


def _s5prep_kernel(are_ref, aim_ref, ldt_ref, bre_ref, bim_ref, cre_ref, cim_ref,
                   arec_ref, aimc_ref, ldtc_ref, crec_ref, cimc_ref,
                   d_ref, bbr_ref, bbi_ref, wcr_ref, wci_ref, atr_ref, ati_ref):
    for ref in (d_ref, bbr_ref, bbi_ref, wcr_ref, wci_ref):
        ref[...] = jnp.zeros_like(ref)
    for g in range(S5_SG):
        rows, lanes = slice(g * S5_GCH, (g + 1) * S5_GCH), slice(g * S5_STATE, (g + 1) * S5_STATE)
        a_re, a_im = are_ref[g], aim_ref[g]
        dt = jnp.exp(ldt_ref[g])
        steps = lax.broadcasted_iota(jnp.int32, (S5_T + 1, S5_STATE), 0).astype(F32)
        mag = jnp.exp(a_re * dt * steps)
        p_re, p_im = mag * jnp.cos(a_im * dt * steps), mag * jnp.sin(a_im * dt * steps)
        nr, ni = p_re[1:2] - 1.0, p_im[1:2]
        den = 1.0 / (a_re * a_re + a_im * a_im)
        f_re, f_im = (nr * a_re + ni * a_im) * den, (ni * a_re - nr * a_im) * den
        b_re, b_im = bre_ref[g], bim_ref[g]
        bb_re, bb_im = f_re * b_re - f_im * b_im, f_re * b_im + f_im * b_re
        c_re, c_im = cre_ref[g], cim_ref[g]
        pw = [(p_re[e:e + 1], p_im[e:e + 1]) for e in range(S5_T + 1)]
        w_re = jnp.concatenate([c_re * r - c_im * i for r, i in pw[:S5_T]], axis=0)
        w_im = jnp.concatenate([c_re * i + c_im * r for r, i in pw[:S5_T]], axis=0)
        kt = _dot3_t(bb_re, w_re) - _dot3_t(bb_im, w_im)
        for e in range(S5_T):
            r, i = pw[e]
            d_ref[e, rows, rows] = kt[:, e * S5_GCH:(e + 1) * S5_GCH].astype(d_ref.dtype)
            bbr_ref[e, rows, lanes] = (r * bb_re - i * bb_im).astype(bbr_ref.dtype)
            bbi_ref[e, rows, lanes] = (r * bb_im + i * bb_re).astype(bbi_ref.dtype)
        atr_ref[:, lanes], ati_ref[:, lanes] = pw[S5_T]
        ac_re, ac_im = arec_ref[g], aimc_ref[g]
        dtc = jnp.exp(ldtc_ref[g])
        steps_c = lax.broadcasted_iota(jnp.int32, (S5_STATE, S5_T), 1).astype(F32) + 1.0
        mag_c = jnp.exp(ac_re * dtc * steps_c)
        q_re, q_im = mag_c * jnp.cos(ac_im * dtc * steps_c), mag_c * jnp.sin(ac_im * dtc * steps_c)
        ct_re, ct_im = crec_ref[g], cimc_ref[g]
        for e in range(S5_T):
            r, i = q_re[:, e:e + 1], q_im[:, e:e + 1]
            wcr_ref[e, lanes, rows] = (ct_re * r - ct_im * i).astype(wcr_ref.dtype)
            wci_ref[e, lanes, rows] = (-(ct_re * i + ct_im * r)).astype(wci_ref.dtype)


def _s5prep_call(a_re, a_im, log_dt, b_re, b_im, c_re, c_im):
    nd, g, p = a_re.shape
    cg = b_re.shape[-1]
    ns = g // S5_SG
    spec = lambda *s: pl.BlockSpec((None, S5_SG) + s, lambda d, j: (d, j) + (0,) * len(s))
    out = lambda *s: pl.BlockSpec((None, None) + s, lambda d, j: (d, j) + (0,) * len(s))
    sds = lambda t, *s: jax.ShapeDtypeStruct((nd, ns) + s, t)
    tr = lambda t: jnp.swapaxes(t, -1, -2)
    ldt = jnp.broadcast_to(log_dt[:, :, None, None], (nd, g, 1, p))
    return pl.pallas_call(
        _s5prep_kernel,
        grid=(nd, ns),
        in_specs=[spec(1, p)] * 3 + [spec(cg, p)] * 4 + [spec(p, 1)] * 3 + [spec(p, cg)] * 2,
        out_specs=[out(S5_T, LANES, LANES), out(S5_T, LANES, S5_SW), out(S5_T, LANES, S5_SW),
                   out(S5_T, S5_SW, LANES), out(S5_T, S5_SW, LANES), out(1, S5_SW), out(1, S5_SW)],
        out_shape=[sds(BF16, S5_T, LANES, LANES), sds(BF16, S5_T, LANES, S5_SW), sds(BF16, S5_T, LANES, S5_SW),
                   sds(BF16, S5_T, S5_SW, LANES), sds(BF16, S5_T, S5_SW, LANES), sds(F32, 1, S5_SW),
                   sds(F32, 1, S5_SW)],
        compiler_params=_params(("arbitrary", "arbitrary")),
        name="s5_prep",
    )(a_re[:, :, None, :], a_im[:, :, None, :], ldt, tr(b_re), tr(b_im), c_re, c_im,
      a_re[:, :, :, None], a_im[:, :, :, None], tr(ldt), tr(c_re), tr(c_im))


def _chunk_rows(u_ref, kb):
    return [u_ref[pl.ds(t, kb, stride=S5_T), :].astype(BF16) for t in range(S5_T)]


def _s5a_kernel(kb, u_ref, br_ref, bi_ref, ore_ref, oim_ref):
    x = _chunk_rows(u_ref, kb)
    x2 = [jnp.concatenate(x[t:t + 2], axis=1) for t in range(0, S5_T, 2)]
    for d in range(br_ref.shape[0]):
        ahead = lambda t: S5_T - 1 - t if d == 0 else t
        pair = lambda ref, t: jnp.concatenate([ref[d, ahead(t)], ref[d, ahead(t + 1)]], axis=0)
        ore_ref[d] = sum(_dot(x2[t // 2], pair(br_ref, t)) for t in range(0, S5_T, 2))
        oim_ref[d] = sum(_dot(x2[t // 2], pair(bi_ref, t)) for t in range(0, S5_T, 2))


def _s5_blocks(L):
    nch = L // S5_T
    return nch, _pick(nch, (208, 144, 80, 72, 40, 16, 8))


def _s5a_call(u, bcr, bci):
    L = u.shape[0]
    nd, ns, _, n, ws = bcr.shape
    nch, kb = _s5_blocks(L)
    wspec = pl.BlockSpec((nd, None, S5_T, n, ws), lambda s, r: (0, s, 0, 0, 0))
    ospec = pl.BlockSpec((nd, kb, ws), lambda s, r: (0, r, s))
    return pl.pallas_call(
        functools.partial(_s5a_kernel, kb),
        grid=(ns, nch // kb),
        in_specs=[pl.BlockSpec((kb * S5_T, LANES), lambda s, r: (r, s)), wspec, wspec],
        out_specs=[ospec, ospec],
        out_shape=[jax.ShapeDtypeStruct((nd, nch, ns * ws), F32)] * 2,
        compiler_params=_params(("arbitrary", "arbitrary")),
        name="s5_chunk_inputs",
    )(u, bcr, bci)


def _s5scan_kernel(n_lat_ch, n_ctx_ch, bre_ref, bim_ref, atr_ref, ati_ref, sre_ref, sim_ref):
    d = pl.program_id(0)
    a_re, a_im = atr_ref[...], ati_ref[...]

    def segment(base, count, carry):
        def step(i, st):
            s_re, s_im = st
            k = base + jnp.where(d == 0, i, count - 1 - i)
            sre_ref[k] = s_re
            sim_ref[k] = s_im
            return (a_re * s_re - a_im * s_im + bre_ref[k], a_re * s_im + a_im * s_re + bim_ref[k])
        return lax.fori_loop(0, count, step, carry)

    zero = jnp.zeros(a_re.shape, F32)
    carry = segment(n_lat_ch, n_ctx_ch, (zero, zero))
    segment(0, n_lat_ch, carry)


def _s5scan_call(b_re, b_im, atr, ati, n_lat_ch, n_ctx_ch):
    nd, nch, w = b_re.shape
    shp = (nd, nch, w // LANES, LANES)
    spec = pl.BlockSpec((None, nch, 8, LANES), lambda d, j: (d, 0, j, 0))
    aspec = pl.BlockSpec((None, 8, LANES), lambda d, j: (d, j, 0))
    return pl.pallas_call(
        functools.partial(_s5scan_kernel, n_lat_ch, n_ctx_ch),
        grid=(nd, w // LANES // 8),
        in_specs=[spec, spec, aspec, aspec],
        out_specs=[spec, spec],
        out_shape=[jax.ShapeDtypeStruct(shp, F32)] * 2,
        compiler_params=_params(("arbitrary", "arbitrary")),
        name="s5_scan",
    )(b_re.reshape(shp), b_im.reshape(shp), atr.reshape(nd, w // LANES, LANES), ati.reshape(nd, w // LANES, LANES))


def _s5c_kernel(kb, u_ref, sre_ref, sim_ref, d_ref, wr_ref, wi_ref, y_ref, acc_ref):
    nd = d_ref.shape[0]
    x = jnp.concatenate(_chunk_rows(u_ref, kb), axis=0)
    s_re = [sre_ref[d].astype(BF16) for d in range(nd)]
    s_im = [sim_ref[d].astype(BF16) for d in range(nd)]
    for t in range(0, S5_T, 2):
        later = lambda d, tt: tt if d == 0 else S5_T - 1 - tt
        both = lambda ref, d: jnp.concatenate([ref[d, later(d, t)], ref[d, later(d, t + 1)]], axis=1)
        r = sum(_dot(s_re[d], both(wr_ref, d)) + _dot(s_im[d], both(wi_ref, d)) for d in range(nd))
        acc_ref[t * kb:(t + 1) * kb, :] = r[:, :LANES]
        acc_ref[(t + 1) * kb:(t + 2) * kb, :] = r[:, LANES:]
    for e in range(0, S5_T, 2):
        n, n1 = (S5_T - e) * kb, (S5_T - e - 1) * kb
        both = lambda d: jnp.concatenate([d_ref[d, e], d_ref[d, e + 1]], axis=1)
        fwd = _dot(x[:n], both(0))
        acc_ref[e * kb:, :] += fwd[:, :LANES]
        acc_ref[(e + 1) * kb:, :] += fwd[:n1, LANES:]
        bwd = _dot(x[e * kb:], both(1))
        acc_ref[:n, :] += bwd[:, :LANES]
        acc_ref[:n1, :] += bwd[kb:, LANES:]
    for t in range(S5_T):
        y_ref[pl.ds(t, kb, stride=S5_T), :] = acc_ref[t * kb:(t + 1) * kb, :]


def _s5c_call(u, s_re, s_im, dm, wcr, wci):
    L = u.shape[0]
    nd, ns = dm.shape[:2]
    ws = wcr.shape[3]
    nch, kb = _s5_blocks(L)
    uspec = pl.BlockSpec((kb * S5_T, LANES), lambda s, r: (r, s))
    sspec = pl.BlockSpec((nd, kb, ws), lambda s, r: (0, r, s))
    wspec = pl.BlockSpec((nd, None, S5_T, ws, LANES), lambda s, r: (0, s, 0, 0, 0))
    return pl.pallas_call(
        functools.partial(_s5c_kernel, kb),
        grid=(ns, nch // kb),
        in_specs=[uspec, sspec, sspec,
                  pl.BlockSpec((nd, None, S5_T, LANES, LANES), lambda s, r: (0, s, 0, 0, 0)), wspec, wspec],
        out_specs=uspec,
        out_shape=jax.ShapeDtypeStruct(u.shape, F32),
        scratch_shapes=[pltpu.VMEM((kb * S5_T, LANES), F32)],
        compiler_params=_params(("arbitrary", "arbitrary")),
        name="s5_outputs",
    )(u, s_re, s_im, dm, wcr, wci)


def _s5_mix(u, ops, n_lat, n_ctx):
    dm, bcr, bci, wcr, wci, atr, ati = ops
    nd = dm.shape[0]
    b_re, b_im = _s5a_call(u, bcr, bci)
    s_re, s_im = _s5scan_call(b_re, b_im, atr.reshape(nd, -1), ati.reshape(nd, -1), n_lat // S5_T, n_ctx // S5_T)
    return _s5c_call(u, s_re.reshape(b_re.shape), s_im.reshape(b_im.shape), dm, wcr, wci)


def _gelu_tanh(x):
    return 0.5 * x * (1.0 + jnp.tanh(math.sqrt(2.0 / math.pi) * (x + 0.044715 * (x * x * x))))


def _post_kernel(n_lat, tm, moe,
                 x_ref, mod_ref, oa_ref, ob_ref, ys_ref, u_ref, gate_ref,
                 wa_ref, wb_ref, wc_ref, wglu_ref, bglu_ref, d_ref, wo_ref, g2_ref, *rest):
    i = pl.program_id(0)
    d = x_ref.shape[-1]
    g = _gelu_tanh(d_ref[...] * u_ref[...] + ys_ref[...])
    oc = (g * jax.nn.sigmoid(_dot(g.astype(BF16), wglu_ref[...]) + bglu_ref[...])).astype(BF16)
    gate = gate_ref[...].astype(F32)
    mix = (gate[:, :d] * _dot(oa_ref[...], wa_ref[...]) + gate[:, d:2 * d] * _dot(ob_ref[...], wb_ref[...])
           + gate[:, 2 * d:] * _dot(oc, wc_ref[...]))
    x_new = x_ref[...] + _row_select(i, tm, n_lat, mod_ref, 2) * _dot(mix.astype(BF16), wo_ref[...])
    h2 = (_rms(x_new, g2_ref[...]) * (1.0 + _row_select(i, tm, n_lat, mod_ref, 4))
          + _row_select(i, tm, n_lat, mod_ref, 3))
    if not moe:
        xo_ref, h2_ref = rest
    else:
        wr_ref, tri_ref, xo_ref, h2_ref, route_ref, count_ref, carry_ref = rest
        logits = _dot3(h2, wr_ref[...])
        lane = lax.broadcasted_iota(jnp.int32, logits.shape, 1)
        logits = jnp.where(lane < N_EXPERTS, logits, NEG)
        m1 = logits.max(axis=-1, keepdims=True)
        i1 = jnp.where(logits == m1, lane, LANES).min(axis=-1, keepdims=True)
        rest_l = jnp.where(lane == i1, NEG, logits)
        m2 = rest_l.max(axis=-1, keepdims=True)
        i2 = jnp.where(rest_l == m2, lane, LANES).min(axis=-1, keepdims=True)
        e2 = jnp.exp(m2 - m1)
        w1 = 1.0 / (1.0 + e2)

        @pl.when(i == 0)
        def _():
            carry_ref[...] = jnp.zeros_like(carry_ref)

        chosen = jnp.where((lane == i1) | (lane == i2), 1.0, 0.0)
        before = _dot(tri_ref[...], chosen.astype(BF16)) + carry_ref[...]
        r1 = jnp.sum(jnp.where(lane == i1, before, 0.0), axis=-1, keepdims=True)
        r2 = jnp.sum(jnp.where(lane == i2, before, 0.0), axis=-1, keepdims=True)
        total = carry_ref[...] + jnp.sum(chosen, axis=0, keepdims=True)
        carry_ref[...] = total
        count_ref[...] = jnp.broadcast_to(total, count_ref.shape)
        cols = [i1.astype(F32), i2.astype(F32), r1, r2, w1, e2 * w1]
        route = jnp.zeros(logits.shape, F32)
        for j, col in enumerate(cols):
            route = jnp.where(lane == j, col, route)
        route_ref[...] = route
    xo_ref[...] = x_new
    h2_ref[...] = h2.astype(h2_ref.dtype)


def _post_call(x, mods, o_a, o_b, ys, u, gate, lw, n_lat, moe):
    L, d = x.shape
    tm = _pick(L, (320, 256, 128))
    row = lambda w: pl.BlockSpec((tm, w), lambda i: (i, 0))
    weights = [lw[k] for k in ("wa", "wb", "wc", "wglu", "bglu", "d", "wo", "g2")]
    if moe:
        tri = jnp.asarray(np.tril(np.ones((tm, tm), np.float32), -1), BF16)
        weights += [lw["wrouter"], tri]
        outs = [(L, d, F32), (L, d, F32), (L, LANES, F32), (8, LANES, F32)]
        out_specs = [row(d), row(d), row(LANES), pl.BlockSpec((8, LANES), lambda i: (0, 0))]
        scratch = [pltpu.VMEM((1, LANES), F32)]
    else:
        outs = [(L, d, F32), (L, d, BF16)]
        out_specs = [row(d), row(d)]
        scratch = []
    return pl.pallas_call(
        functools.partial(_post_kernel, n_lat, tm, moe),
        grid=(L // tm,),
        in_specs=([row(d), _const_spec(mods.shape), row(HEAD_PAD), row(NA_WIDTH), row(S5_WIDTH), row(S5_WIDTH),
                   row(N_BRANCH * d)] + [_const_spec(w.shape) for w in weights]),
        out_specs=out_specs,
        out_shape=[jax.ShapeDtypeStruct((r, w), t) for r, w, t in outs],
        scratch_shapes=scratch,
        compiler_params=_params(("arbitrary",)),
        name="post_moe" if moe else "post",
    )(x, mods, o_a, o_b, ys, u, gate, *weights)


def _swiglu(h, wg_ref, wu_ref, wd_ref):
    a = _dot(h, wg_ref[...])
    act = (a * jax.nn.sigmoid(a) * _dot(h, wu_ref[...])).astype(BF16)
    return _dot(act, wd_ref[...])


def _ffn_kernel(n_lat, tm, x_ref, h_ref, mod_ref, wg_ref, wu_ref, wd_ref, o_ref):
    f = _swiglu(h_ref[...], wg_ref, wu_ref, wd_ref)
    o_ref[...] = x_ref[...] + _row_select(pl.program_id(0), tm, n_lat, mod_ref, 5) * f


def _ffn_call(x, h2, mods, wg, wu, wd, n_lat):
    L, d = x.shape
    tm = _pick(L, (640, 256, 128))
    row = lambda w: pl.BlockSpec((tm, w), lambda i: (i, 0))
    return pl.pallas_call(
        functools.partial(_ffn_kernel, n_lat, tm),
        grid=(L // tm,),
        in_specs=[row(d), row(d), _const_spec(mods.shape), _const_spec(wg.shape), _const_spec(wu.shape),
                  _const_spec(wd.shape)],
        out_specs=row(d),
        out_shape=jax.ShapeDtypeStruct((L, d), F32),
        compiler_params=_params(("arbitrary",)),
        name="ffn",
    )(x, h2, mods, wg, wu, wd)


MOE_TS = 512


def _row_copy(src, i, dst, j, sem):
    return pltpu.make_async_copy(src.at[pl.ds(i, 1)], dst.at[pl.ds(j, 1)], sem)


ROW_DMA_UNROLL = 8


def _dispatch_kernel(tb, n_tok, slot_ref, h_ref, xs_in, xs_hbm, sem):
    del xs_in
    base = pl.program_id(0) * tb

    def issue(t, carry):
        for k in range(TOP_K):
            _row_copy(h_ref, t, xs_hbm, slot_ref[k * n_tok + base + t], sem).start(priority=k % 2)
        return carry

    def drain(t, carry):
        for k in range(TOP_K):
            _row_copy(h_ref, 0, xs_hbm, 0, sem).wait()
        return carry

    lax.fori_loop(0, tb, issue, 0, unroll=ROW_DMA_UNROLL)
    lax.fori_loop(0, tb, drain, 0, unroll=ROW_DMA_UNROLL)


def _dispatch_call(slots, h2, n_slots):
    L, d = h2.shape
    tb = _pick(L, (640, 256, 128))
    xs0 = jnp.zeros((n_slots, d), h2.dtype)
    return pl.pallas_call(
        functools.partial(_dispatch_kernel, tb, L),
        grid_spec=pltpu.PrefetchScalarGridSpec(
            num_scalar_prefetch=1, grid=(L // tb,),
            in_specs=[pl.BlockSpec((tb, d), lambda i, s: (i, 0)), pl.BlockSpec(memory_space=pl.ANY)],
            out_specs=pl.BlockSpec(memory_space=pl.ANY),
            scratch_shapes=[pltpu.SemaphoreType.DMA(())]),
        out_shape=jax.ShapeDtypeStruct((n_slots, d), h2.dtype),
        input_output_aliases={2: 0},
        compiler_params=_params(("arbitrary",)),
        name="moe_dispatch",
    )(slots, h2, xs0)


def _expert_kernel(te_ref, nu_ref, x_ref, wg_ref, wu_ref, wd_ref, o_ref):
    used = pl.program_id(0) < nu_ref[0]

    @pl.when(used)
    def _():
        o_ref[...] = _swiglu(x_ref[...].astype(BF16), wg_ref, wu_ref, wd_ref)

    @pl.when(jnp.logical_not(used))
    def _():
        o_ref[...] = jnp.zeros_like(o_ref)


def _expert_call(tile_expert, n_used, xs, wg, wu, wd, layer):
    n_slots, d = xs.shape
    dff = wg.shape[-1]
    once = pl.Buffered(1)
    return pl.pallas_call(
        _expert_kernel,
        grid_spec=pltpu.PrefetchScalarGridSpec(
            num_scalar_prefetch=2, grid=(n_slots // MOE_TS,),
            in_specs=[pl.BlockSpec((MOE_TS, d), lambda j, te, nu: (j, 0)),
                      pl.BlockSpec((None, None, d, dff), lambda j, te, nu: (layer, te[j], 0, 0), pipeline_mode=once),
                      pl.BlockSpec((None, None, d, dff), lambda j, te, nu: (layer, te[j], 0, 0), pipeline_mode=once),
                      pl.BlockSpec((None, None, dff, d), lambda j, te, nu: (layer, te[j], 0, 0), pipeline_mode=once)],
            out_specs=pl.BlockSpec((MOE_TS, d), lambda j, te, nu: (j, 0))),
        out_shape=jax.ShapeDtypeStruct((n_slots, d), F32),
        compiler_params=_params(("arbitrary",)),
        name="moe_experts",
    )(tile_expert, n_used, xs, wg, wu, wd)


def _combine_kernel(n_lat, tb, n_tok, slot_ref, x_ref, route_ref, mod_ref, zs_hbm, o_ref, g_ref, sem):
    i = pl.program_id(0)
    base = i * tb

    def issue(t, carry):
        for k in range(TOP_K):
            _row_copy(zs_hbm, slot_ref[k * n_tok + base + t], g_ref.at[k], t, sem).start(priority=k % 2)
        return carry

    def drain(t, carry):
        for k in range(TOP_K):
            _row_copy(zs_hbm, 0, g_ref.at[k], 0, sem).wait()
        return carry

    lax.fori_loop(0, tb, issue, 0, unroll=ROW_DMA_UNROLL)
    lax.fori_loop(0, tb, drain, 0, unroll=ROW_DMA_UNROLL)
    route = route_ref[...]
    f = route[:, 4:5] * g_ref[0] + route[:, 5:6] * g_ref[1]
    o_ref[...] = x_ref[...] + _row_select(i, tb, n_lat, mod_ref, 5) * f


def _combine_call(slots, x, route, mods, zs, n_lat):
    L, d = x.shape
    tb = _pick(L, (320, 256, 128))
    row = lambda w: pl.BlockSpec((tb, w), lambda i, s: (i, 0))
    return pl.pallas_call(
        functools.partial(_combine_kernel, n_lat, tb, L),
        grid_spec=pltpu.PrefetchScalarGridSpec(
            num_scalar_prefetch=1, grid=(L // tb,),
            in_specs=[row(d), row(LANES), pl.BlockSpec(mods.shape, lambda i, s: (0, 0, 0)),
                      pl.BlockSpec(memory_space=pl.ANY)],
            out_specs=row(d),
            scratch_shapes=[pltpu.VMEM((TOP_K, tb, d), F32), pltpu.SemaphoreType.DMA(())]),
        out_shape=jax.ShapeDtypeStruct((L, d), F32),
        compiler_params=_params(("arbitrary",)),
        name="moe_combine",
    )(slots, x, route, mods, zs)


def _moe_call(x, h2, route, counts, mods, wg, wu, wd, layer, n_lat):
    L, d = x.shape
    ne = wg.shape[1]
    n_slots = (pl.cdiv(TOP_K * L, MOE_TS) + ne) * MOE_TS
    cnt = counts[0, :ne].astype(jnp.int32)
    size = (cnt + MOE_TS - 1) // MOE_TS * MOE_TS
    ends = jnp.cumsum(size)
    offs = ends - size
    eid = jnp.arange(ne, dtype=jnp.int32)
    slot = lambda e, r: jnp.sum(jnp.where(e[:, None] == eid[None, :], offs[None, :], 0), axis=1) + r
    r = route.astype(jnp.int32)
    slots = jnp.concatenate([slot(r[:, 0], r[:, 2]), slot(r[:, 1], r[:, 3])])
    starts = jnp.arange(n_slots // MOE_TS, dtype=jnp.int32) * MOE_TS
    tile_expert = jnp.minimum(jnp.sum(starts[:, None] >= ends[None, :], axis=1), ne - 1).astype(jnp.int32)
    n_used = (ends[-1:] // MOE_TS).astype(jnp.int32)
    xs = _dispatch_call(slots, h2, n_slots)
    zs = _expert_call(tile_expert, n_used, xs, wg, wu, wd, layer)
    return _combine_call(slots, x, route, mods, zs, n_lat)


def _rope_tables(n_lat, n_ctx):
    t = jnp.arange(n_lat, dtype=jnp.int32)
    n_freq = MLA_ROPE // 4
    inv = ROPE_THETA ** (-jnp.arange(n_freq, dtype=F32) / n_freq)
    ang = jnp.concatenate([(t // GRID_W).astype(F32)[:, None] * inv[None],
                           (t % GRID_W).astype(F32)[:, None] * inv[None]], axis=-1)
    cos, sin = jnp.cos(ang), jnp.sin(ang)
    ones = jnp.ones((n_lat, MLA_NOPE), F32)
    zn = jnp.zeros((n_lat, MLA_NOPE), F32)
    zh = jnp.zeros_like(sin)
    zp = jnp.zeros((n_lat, LANES - MLA_QK), F32)
    c = jnp.concatenate([ones, cos, cos, zp], axis=-1)
    s1 = jnp.concatenate([zn, -sin, zh, zp], axis=-1)
    s2 = jnp.concatenate([zn, zh, sin, zp], axis=-1)
    ctx = lambda a, fill: jnp.concatenate([a, jnp.full((n_ctx, LANES), fill, F32)], axis=0)
    return ctx(c, 1.0), ctx(s1, 0.0), ctx(s2, 0.0)


def _head_pad_cols(w, width):
    r = w.shape[0]
    w = w.reshape(r, MLA_HEADS, width)
    return jnp.pad(w, ((0, 0), (0, 0), (0, LANES - width))).reshape(r, HEAD_PAD)


def _layer_weights(i, p):
    d = D_MODEL
    w_in = p["w_in"][i]
    cuts = np.cumsum((0,) + IN_SPLITS)
    piece = lambda j: w_in[:, cuts[j]:cuts[j + 1]]
    bf = lambda w: w.astype(BF16)
    row = lambda v: v.reshape(1, -1).astype(F32)
    ukv = p["w_mla_ukv"][i].reshape(MLA_KV_RANK, MLA_HEADS, MLA_NOPE + MLA_V)
    wk_nope = _head_pad_cols(ukv[:, :, :MLA_NOPE].reshape(MLA_KV_RANK, -1), MLA_NOPE)
    kr_place = jnp.zeros((LANES, MLA_HEADS, LANES), F32)
    eye = jnp.eye(MLA_ROPE, dtype=F32)
    kr_place = kr_place.at[:MLA_ROPE, :, MLA_NOPE:MLA_QK].set(jnp.broadcast_to(eye[:, None, :], (MLA_ROPE, MLA_HEADS, MLA_ROPE)))
    vone = jnp.zeros((MLA_HEADS, LANES), F32).at[:, MLA_V].set(1.0).reshape(1, HEAD_PAD)
    head_gain = lambda g: jnp.tile(jnp.pad(g, (0, LANES - MLA_QK)), MLA_HEADS).reshape(1, HEAD_PAD)
    e64 = jnp.kron(jnp.eye(NA_HEADS, dtype=F32), jnp.ones((NA_DIM, NA_DIM), F32))
    wa = jnp.pad(p["w_br_mla"][i].reshape(MLA_HEADS, MLA_V, d), ((0, 0), (0, LANES - MLA_V), (0, 0)))
    return {
        "g1": row(p["g_norm1"][i]), "g2": row(p["g_norm2"][i]),
        "wq": bf(piece(0)), "wkv": bf(piece(1)),
        "wkr": bf(jnp.pad(piece(2), ((0, 0), (0, LANES - MLA_ROPE)))),
        "wnq": bf(piece(3)), "wnk": bf(piece(4)), "wnv": bf(piece(5)), "wu": bf(piece(6)), "wg": bf(piece(7)),
        "gq": row(p["g_mla_q"][i]), "gkv": row(p["g_mla_kv"][i]),
        "wuq": bf(_head_pad_cols(p["w_mla_uq"][i], MLA_QK)),
        "wk": bf(jnp.concatenate([wk_nope, kr_place.reshape(LANES, HEAD_PAD)], axis=0)),
        "wv": bf(_head_pad_cols(ukv[:, :, MLA_NOPE:].reshape(MLA_KV_RANK, -1), MLA_V)),
        "vone": vone,
        "gqn": head_gain(p["g_mla_qn"][i]), "gkn": head_gain(p["g_mla_kn"][i]),
        "e64": bf(e64),
        "gnq": jnp.tile(p["g_na_qn"][i], NA_HEADS).reshape(1, -1), "gnk": jnp.tile(p["g_na_kn"][i], NA_HEADS).reshape(1, -1),
        "wa": bf(wa.reshape(HEAD_PAD, d)), "wb": bf(p["w_br_na"][i]), "wc": bf(p["w_br_s5"][i]),
        "wglu": bf(p["w_glu"][i]), "bglu": row(p["b_glu"][i]), "d": row(p["s5_d"][i]), "wo": bf(p["w_out"][i]),
    }


def kernel(x, c, ctx, c_ctx, w_mod, b_mod, g_norm1, g_norm2, w_in, g_mla_q, g_mla_kv, w_mla_uq, w_mla_ukv,
           g_mla_qn, g_mla_kn, g_na_qn, g_na_kn, na_rpb, s5_a_re, s5_a_im, s5_log_dt, s5_b_re, s5_b_im,
           s5_c_re, s5_c_im, s5_d, w_glu, b_glu, w_br_mla, w_br_na, w_br_s5, w_out, w_ffn_gate, w_ffn_up,
           w_ffn_down, w_router, w_exp_gate, w_exp_up, w_exp_down):
    p = dict(w_in=w_in, g_norm1=g_norm1, g_norm2=g_norm2, g_mla_q=g_mla_q, g_mla_kv=g_mla_kv, w_mla_uq=w_mla_uq,
             w_mla_ukv=w_mla_ukv, g_mla_qn=g_mla_qn, g_mla_kn=g_mla_kn, g_na_qn=g_na_qn, g_na_kn=g_na_kn,
             s5_d=s5_d, w_glu=w_glu, b_glu=b_glu, w_br_mla=w_br_mla, w_br_na=w_br_na, w_br_s5=w_br_s5, w_out=w_out)
    assert x.shape[0] == 1 and x.shape[2] == D_MODEL
    n_lat, n_ctx = x.shape[1], ctx.shape[1]
    depth = w_mod.shape[0]
    xs = jnp.concatenate([x[0], ctx[0]], axis=0)

    cvec = jnp.zeros((8, D_MODEL), F32).at[0].set(c[0]).at[1].set(c_ctx)
    mods_all = _mod_call(cvec, w_mod, b_mod)[:, :2, None, :]
    rope = _rope_tables(n_lat, n_ctx)
    experts = (w_exp_gate.astype(BF16), w_exp_up.astype(BF16), w_exp_down.astype(BF16))

    for i in range(depth):
        lw = _layer_weights(i, p)
        mods = mods_all[i]
        q, k, v, nq, nk, nv, u, gate, qmax, kmax = _pre_call(xs, mods, n_lat, lw, rope)

        o_a = _mla_attend(q, k, v, qmax, kmax, n_lat)
        bias = _na_bias(na_rpb[i], n_lat // GRID_W)
        o_b = _na_call(nq, nk, nv, bias, n_lat, n_ctx)
        o_a_c, o_b_c = _ctx_call(q, k, v, nq, nk, nv, n_lat, n_ctx)
        o_a = jnp.concatenate([o_a, o_a_c], axis=0)
        o_b = jnp.concatenate([o_b, o_b_c], axis=0)

        ops = _s5prep_call(s5_a_re[i], s5_a_im[i], s5_log_dt[i], s5_b_re[i], s5_b_im[i], s5_c_re[i], s5_c_im[i])
        ys = _s5_mix(u, ops, n_lat, n_ctx)

        moe = i % 2 == 1
        j = i // 2
        if moe:
            lw["wrouter"] = jnp.pad(w_router[j], ((0, 0), (0, LANES - N_EXPERTS)))
            xs, h2, route, counts = _post_call(xs, mods, o_a, o_b, ys, u, gate, lw, n_lat, True)
            xs = _moe_call(xs, h2, route, counts, mods, *experts, j, n_lat)
        else:
            xs, h2 = _post_call(xs, mods, o_a, o_b, ys, u, gate, lw, n_lat, False)
            xs = _ffn_call(xs, h2, mods, w_ffn_gate[j].astype(BF16), w_ffn_up[j].astype(BF16),
                           w_ffn_down[j].astype(BF16), n_lat)
    return xs[:n_lat][None]
```

```python
import functools
import math

import jax
import jax.numpy as jnp
import numpy as np
from jax import lax
from jax.experimental import pallas as pl
from jax.experimental.pallas import tpu as pltpu

F32 = jnp.float32
BF16 = jnp.bfloat16

D_MODEL = 1024
DEPTH = 4
GRID_W = 64
EPS = 1e-6
ROPE_THETA = 10000.0

MLA_HEADS = 8
MLA_NOPE = 64
MLA_ROPE = 32
MLA_V = 64
MLA_Q_RANK = 384
MLA_KV_RANK = 256
MLA_QK = MLA_NOPE + MLA_ROPE

NA_HEADS = 8
NA_DIM = 64
NA_KH = 8
NA_KW = 16
NA_WIDTH = NA_HEADS * NA_DIM

S5_GROUPS = 32
S5_GCH = 16
S5_STATE = 64
S5_WIDTH = S5_GROUPS * S5_GCH

N_BRANCH = 3
D_FF = 2816
N_EXPERTS = 8
TOP_K = 2

IN_SPLITS = (MLA_Q_RANK, MLA_KV_RANK, MLA_ROPE, NA_WIDTH, NA_WIDTH, NA_WIDTH, S5_WIDTH, N_BRANCH * D_MODEL)

LANES = 128
HEAD_PAD = MLA_HEADS * LANES
S5_T = 16
NA_QROWS = 4
NA_WROWS = NA_QROWS + NA_KH - 1
VMEM_LIMIT = 56 * 1024 * 1024
NEG = -1e30


def _pick(n, candidates):
    for c in candidates:
        if n % c == 0:
            return c
    raise ValueError(f"no tile in {candidates} divides {n}")


def _const_spec(shape):
    nd = len(shape)
    return pl.BlockSpec(shape, lambda *_: (0,) * nd, pipeline_mode=pl.Buffered(1))


def _params(sem):
    return pltpu.CompilerParams(dimension_semantics=sem, vmem_limit_bytes=VMEM_LIMIT)


def _dot(a, b):
    return jnp.dot(a, b, preferred_element_type=F32)


def _dot_t(a, b):
    return lax.dot_general(a, b, (((1,), (1,)), ((), ())), preferred_element_type=F32)


def _split(x):
    hi = x.astype(BF16)
    lo = (x - hi.astype(F32)).astype(BF16)
    return hi, lo


def _dot3(a, b):
    a_hi, a_lo = _split(a)
    b_hi, b_lo = _split(b)
    return _dot(a_hi, b_hi) + _dot(a_hi, b_lo) + _dot(a_lo, b_hi)


def _dot3_t(a, b):
    a_hi, a_lo = _split(a)
    b_hi, b_lo = _split(b)
    return _dot_t(a_hi, b_hi) + _dot_t(a_hi, b_lo) + _dot_t(a_lo, b_hi)


def _rms(x, g):
    ms = jnp.mean(x * x, axis=-1, keepdims=True)
    return x * lax.rsqrt(ms + EPS) * g


def _row_select(tile, tm, n_lat, mod_ref, idx):
    rows = tile * tm + lax.broadcasted_iota(jnp.int32, (tm, 1), 0)
    lat = mod_ref[0, :, idx * D_MODEL:(idx + 1) * D_MODEL]
    ctx = mod_ref[1, :, idx * D_MODEL:(idx + 1) * D_MODEL]
    return jnp.where(rows < n_lat, lat, ctx)


def _mod_kernel(c_ref, w_ref, b_ref, o_ref):
    c = c_ref[...]
    a = c * jax.nn.sigmoid(c)
    o_ref[...] = _dot3(a, w_ref[...]) + b_ref[...]


def _mod_call(cvec, w_mod, b_mod):
    depth, d, n6 = w_mod.shape
    tn = _pick(n6, (1536, 1024, 512, 128))
    return pl.pallas_call(
        _mod_kernel,
        grid=(depth, n6 // tn),
        in_specs=[
            pl.BlockSpec((8, d), lambda l, j: (0, 0)),
            pl.BlockSpec((None, d, tn), lambda l, j: (l, 0, j)),
            pl.BlockSpec((None, 1, tn), lambda l, j: (l, 0, j)),
        ],
        out_specs=pl.BlockSpec((None, 8, tn), lambda l, j: (l, 0, j)),
        out_shape=jax.ShapeDtypeStruct((depth, 8, n6), F32),
        compiler_params=_params(("arbitrary", "arbitrary")),
        name="mod",
    )(cvec, w_mod, b_mod.reshape(depth, 1, n6))


def _head_norm_rope(xp, gain, c_t, s1_t, s2_t, scale, o_ref, nmax_ref, first, carry_norm):
    lane = lax.broadcasted_iota(jnp.int32, (1, LANES), 1)
    for h in range(MLA_HEADS):
        sl = slice(h * LANES, (h + 1) * LANES)
        xh = xp[:, sl]
        ms = jnp.sum(xh * xh, axis=-1, keepdims=True) * (1.0 / MLA_QK)
        y = xh * lax.rsqrt(ms + EPS) * gain[:, sl]
        y = y * c_t + pltpu.roll(y, LANES - MLA_ROPE // 2, 1) * s1_t + pltpu.roll(y, MLA_ROPE // 2, 1) * s2_t
        if scale != 1.0:
            y = y * scale
        norm = jnp.sqrt(jnp.sum(y * y, axis=-1, keepdims=True))
        tile_max = jnp.broadcast_to(jnp.max(norm, axis=0, keepdims=True), (1, LANES))
        nmax_ref[h:h + 1, :] = jnp.maximum(jnp.where(first, 0.0, nmax_ref[h:h + 1, :]), tile_max)
        if carry_norm:
            y = jnp.where(lane == MLA_QK, -norm, y)
        o_ref[:, sl] = y.astype(o_ref.dtype)


def _pre_kernel(n_lat, tm, mla_scale, na_scale,
                x_ref, mod_ref, g1_ref, wq_ref, wkv_ref, wkr_ref, wnq_ref, wnk_ref, wnv_ref, wu_ref, wg_ref,
                gq_ref, gkv_ref, wuq_ref, wk_ref, wv_ref, vone_ref, gqn_ref, gkn_ref,
                rc_ref, rs1_ref, rs2_ref, e64_ref, gnq_ref, gnk_ref,
                q_ref, k_ref, v_ref, nq_ref, nk_ref, nv_ref, u_ref, gate_ref, qmax_ref, kmax_ref):
    i = pl.program_id(0)
    sh = _row_select(i, tm, n_lat, mod_ref, 0)
    sc = _row_select(i, tm, n_lat, mod_ref, 1)
    h = (_rms(x_ref[...], g1_ref[...]) * (1.0 + sc) + sh).astype(BF16)

    c_t, s1_t, s2_t = rc_ref[...], rs1_ref[...], rs2_ref[...]

    rq = _rms(_dot(h, wq_ref[...]), gq_ref[...]).astype(BF16)
    _head_norm_rope(_dot(rq, wuq_ref[...]), gqn_ref[...], c_t, s1_t, s2_t, mla_scale, q_ref, qmax_ref, i == 0, True)
    rkv = _rms(_dot(h, wkv_ref[...]), gkv_ref[...]).astype(BF16)
    pkr = _dot(h, wkr_ref[...]).astype(BF16)
    kin = jnp.concatenate([rkv, pkr], axis=-1)
    _head_norm_rope(_dot(kin, wk_ref[...]), gkn_ref[...], c_t, s1_t, s2_t, 1.0, k_ref, kmax_ref, i == 0, False)
    v_ref[...] = (_dot(rkv, wv_ref[...]) + vone_ref[...]).astype(BF16)

    def na_norm(w_ref, g_ref, scale):
        p = _dot(h, w_ref[...])
        sq_hi, sq_lo = _split(p * p)
        ss = (_dot(sq_hi, e64_ref[...]) + _dot(sq_lo, e64_ref[...])) * (1.0 / NA_DIM)
        return (p * lax.rsqrt(ss + EPS) * (g_ref[...] * scale)).astype(BF16)

    nq_ref[...] = na_norm(wnq_ref, gnq_ref, na_scale)
    nk_ref[...] = na_norm(wnk_ref, gnk_ref, 1.0)
    nv_ref[...] = _dot(h, wnv_ref[...]).astype(BF16)

    u_ref[...] = _dot(h, wu_ref[...])
    gate_ref[...] = jax.nn.sigmoid(_dot(h, wg_ref[...])).astype(BF16)


def _pre_call(x, mods, n_lat, lw, rope):
    L, d = x.shape
    tm = _pick(L, (320, 256, 128))
    row = lambda w: pl.BlockSpec((tm, w), lambda i: (i, 0))
    weights = [lw[k] for k in ("g1", "wq", "wkv", "wkr", "wnq", "wnk", "wnv", "wu", "wg", "gq", "gkv", "wuq",
                               "wk", "wv", "vone", "gqn", "gkn")]
    tail = [lw["e64"], lw["gnq"], lw["gnk"]]
    in_specs = ([row(d), _const_spec(mods.shape)] + [_const_spec(w.shape) for w in weights]
                + [row(LANES)] * 3 + [_const_spec(w.shape) for w in tail])
    outs = [(HEAD_PAD, BF16)] * 3 + [(NA_WIDTH, BF16)] * 3 + [(S5_WIDTH, F32), (N_BRANCH * d, BF16)]
    return pl.pallas_call(
        functools.partial(_pre_kernel, n_lat, tm, MLA_QK ** -0.5 * math.log2(math.e), NA_DIM ** -0.5),
        grid=(L // tm,),
        in_specs=in_specs,
        out_specs=[row(w) for w, _ in outs] + [pl.BlockSpec((MLA_HEADS, LANES), lambda i: (0, 0))] * 2,
        out_shape=([jax.ShapeDtypeStruct((L, w), t) for w, t in outs]
                   + [jax.ShapeDtypeStruct((MLA_HEADS, LANES), F32)] * 2),
        compiler_params=_params(("arbitrary",)),
        name="pre",
    )(x, mods, *weights, *rope, *tail)


def _mla_kernel(tq, tk, n_chunks, q_ref, k_ref, v_ref, o_ref, m_ref, acc_ref, sa_ref, sb_ref):
    m_ref[...] = jnp.full_like(m_ref, NEG)
    acc_ref[...] = jnp.zeros_like(acc_ref)
    q = q_ref[...]

    def scores(c):
        off = pl.multiple_of(c * tk, tk)
        return _dot_t(q, k_ref[pl.ds(off, tk), :])

    def absorb(s, c):
        off = pl.multiple_of(c * tk, tk)
        m_prev = m_ref[...]
        m_new = jnp.maximum(m_prev, jnp.max(s, axis=-1, keepdims=True))
        p = jnp.exp2(s - jnp.concatenate([m_new] * (tk // LANES), axis=1))
        acc_ref[...] = jnp.exp2(m_prev - m_new) * acc_ref[...] + _dot(p.astype(BF16), v_ref[pl.ds(off, tk), :])
        m_ref[...] = m_new

    sa_ref[...] = scores(0)

    def body(i, carry):
        c = 2 * i
        sb_ref[...] = scores(c + 1)
        absorb(sa_ref[...], c)
        sa_ref[...] = scores(c + 2)
        absorb(sb_ref[...], c + 1)
        return carry

    lax.fori_loop(0, (n_chunks - 1) // 2, body, 0)
    if n_chunks % 2 == 0:
        sb_ref[...] = scores(n_chunks - 1)
        absorb(sa_ref[...], n_chunks - 2)
        absorb(sb_ref[...], n_chunks - 1)
    else:
        absorb(sa_ref[...], n_chunks - 1)
    acc = acc_ref[...]
    o_ref[...] = (acc * pl.reciprocal(acc[:, MLA_V:MLA_V + 1], approx=False)).astype(o_ref.dtype)


MLA_SHIFT_MAX = 60.0


def _mla_shifted_kernel(tq, tk, n_chunks, kmax_ref, q_ref, k_ref, v_ref, o_ref, ks_ref, acc_ref, sa_ref, sb_ref):
    @pl.when(pl.program_id(1) == 0)
    def _():
        lane = lax.broadcasted_iota(jnp.int32, (1, LANES), 1)
        ks_ref[...] = jnp.where(lane == MLA_QK, kmax_ref[...].astype(BF16), k_ref[...])

    acc_ref[...] = jnp.zeros_like(acc_ref)
    q = q_ref[...]

    def scores(c):
        off = pl.multiple_of(c * tk, tk)
        return _dot_t(q, ks_ref[pl.ds(off, tk), :])

    def absorb(s, c):
        off = pl.multiple_of(c * tk, tk)
        acc_ref[...] += _dot(jnp.exp2(s).astype(BF16), v_ref[pl.ds(off, tk), :])

    sa_ref[...] = scores(0)

    def body(i, carry):
        c = 2 * i
        sb_ref[...] = scores(c + 1)
        absorb(sa_ref[...], c)
        sa_ref[...] = scores(c + 2)
        absorb(sb_ref[...], c + 1)
        return carry

    lax.fori_loop(0, (n_chunks - 1) // 2, body, 0)
    if n_chunks % 2 == 0:
        sb_ref[...] = scores(n_chunks - 1)
        absorb(sa_ref[...], n_chunks - 2)
        absorb(sb_ref[...], n_chunks - 1)
    else:
        absorb(sa_ref[...], n_chunks - 1)
    acc = acc_ref[...]
    o_ref[...] = (acc * pl.reciprocal(acc[:, MLA_V:MLA_V + 1], approx=False)).astype(o_ref.dtype)


def _mla_shifted_call(q, k, v, kmax, n_lat):
    L = q.shape[0]
    tq = _pick(n_lat, (1024, 512, 256))
    tk = _pick(L, (1280, 768, 512, 256))
    kv_spec = pl.BlockSpec((L, LANES), lambda h, i: (0, h))
    return pl.pallas_call(
        functools.partial(_mla_shifted_kernel, tq, tk, L // tk),
        grid=(MLA_HEADS, n_lat // tq),
        in_specs=[pl.BlockSpec((None, 1, LANES), lambda h, i: (h, 0, 0)),
                  pl.BlockSpec((tq, LANES), lambda h, i: (i, h)), kv_spec, kv_spec],
        out_specs=pl.BlockSpec((tq, LANES), lambda h, i: (i, h)),
        out_shape=jax.ShapeDtypeStruct((n_lat, HEAD_PAD), BF16),
        scratch_shapes=[pltpu.VMEM((L, LANES), BF16), pltpu.VMEM((tq, LANES), F32),
                        pltpu.VMEM((tq, tk), F32), pltpu.VMEM((tq, tk), F32)],
        compiler_params=_params(("arbitrary", "arbitrary")),
        name="mla_shifted",
    )(kmax.reshape(MLA_HEADS, 1, LANES), q, k, v)


def _mla_attend(q, k, v, qmax, kmax, n_lat):
    shift = jnp.max(qmax[:, 0] * kmax[:, 0])
    return lax.cond(shift <= MLA_SHIFT_MAX, lambda: _mla_shifted_call(q, k, v, kmax, n_lat),
                    lambda: _mla_call(q, k, v, n_lat))


def _mla_call(q, k, v, n_lat):
    L = q.shape[0]
    tq = _pick(n_lat, (1024, 512, 256))
    tk = _pick(L, (1280, 768, 512, 256))
    kv_spec = pl.BlockSpec((L, LANES), lambda h, i: (0, h))
    return pl.pallas_call(
        functools.partial(_mla_kernel, tq, tk, L // tk),
        grid=(MLA_HEADS, n_lat // tq),
        in_specs=[pl.BlockSpec((tq, LANES), lambda h, i: (i, h)), kv_spec, kv_spec],
        out_specs=pl.BlockSpec((tq, LANES), lambda h, i: (i, h)),
        out_shape=jax.ShapeDtypeStruct((n_lat, HEAD_PAD), BF16),
        scratch_shapes=[pltpu.VMEM((tq, LANES), F32), pltpu.VMEM((tq, LANES), F32),
                        pltpu.VMEM((tq, tk), F32), pltpu.VMEM((tq, tk), F32)],
        compiler_params=_params(("arbitrary", "arbitrary")),
        name="mla",
    )(q, k, v)


def _softmax_pv(parts):
    m = parts[0][0].max(axis=-1, keepdims=True)
    for s, _ in parts[1:]:
        m = jnp.maximum(m, s.max(axis=-1, keepdims=True))
    num, den = 0.0, 0.0
    for s, v in parts:
        p = jnp.exp(s - m)
        den = den + p.sum(axis=-1, keepdims=True)
        num = num + _dot(p.astype(BF16), v)
    return num * pl.reciprocal(den, approx=False)


NA_STEP_BLOCKS = 2


def _na_kernel(n_lat, n_ctx, n_blocks, q_ref, k_ref, v_ref, *rest):
    bias_refs, o_ref = rest[:NA_STEP_BLOCKS], rest[NA_STEP_BLOCKS]
    wtok, tq = NA_WROWS * GRID_W, NA_QROWS * GRID_W
    kc, vc = k_ref[n_lat:n_lat + n_ctx, :], v_ref[n_lat:n_lat + n_ctx, :]
    lane = lax.broadcasted_iota(jnp.int32, (1, LANES), 1)
    for sub in range(NA_STEP_BLOCKS):
        b = pl.program_id(1) * NA_STEP_BLOCKS + sub
        row0 = jnp.clip(b * NA_QROWS - NA_KH // 2, 0, n_lat // GRID_W - NA_WROWS)
        off = pl.multiple_of(row0 * GRID_W, GRID_W)
        q = q_ref[sub * tq:(sub + 1) * tq, :]
        kw, vw = k_ref[pl.ds(off, wtok), :], v_ref[pl.ds(off, wtok), :]
        out = jnp.zeros(q.shape, F32)
        for j in range(LANES // NA_DIM):
            head = (lane >= j * NA_DIM) & (lane < (j + 1) * NA_DIM)
            qh = jnp.where(head, q, jnp.zeros_like(q))
            s_win = _dot_t(qh, kw) + bias_refs[sub][j]
            s_ctx = _dot_t(qh, kc)
            out = jnp.where(head, _softmax_pv([(s_ctx, vc), (s_win, vw)]), out)
        o_ref[sub * tq:(sub + 1) * tq, :] = out.astype(o_ref.dtype)


def _na_bias(rpb, n_rows):
    n_blocks = n_rows // NA_QROWS
    cols = np.arange(GRID_W)
    c0 = np.clip(cols - NA_KW // 2, 0, GRID_W - NA_KW)
    col_ok = (cols[None, :] >= c0[:, None]) & (cols[None, :] < c0[:, None] + NA_KW)
    dc = cols[None, :] - cols[:, None] + NA_KW - 1
    pick_c = (col_ok[:, :, None] & (dc[:, :, None] == np.arange(2 * NA_KW - 1))).astype(np.float32)
    pick_r = np.zeros((3, NA_QROWS, NA_WROWS, 2 * NA_KH - 1), np.float32)
    row_oks = []
    for v, b in enumerate((0, 1, n_blocks - 1)):
        row0 = min(max(b * NA_QROWS - NA_KH // 2, 0), n_rows - NA_WROWS)
        qr = b * NA_QROWS + np.arange(NA_QROWS)
        kr = row0 + np.arange(NA_WROWS)
        r0 = np.clip(qr - NA_KH // 2, 0, n_rows - NA_KH)
        row_ok = (kr[None, :] >= r0[:, None]) & (kr[None, :] < r0[:, None] + NA_KH)
        dr = kr[None, :] - qr[:, None] + NA_KH - 1
        pick_r[v] = row_ok[:, :, None] & (dr[:, :, None] == np.arange(2 * NA_KH - 1))
        row_oks.append(row_ok)
    hi = lax.Precision.HIGHEST
    by_col = jnp.einsum("hrd,qkd->hrqk", rpb.astype(F32), pick_c, precision=hi)
    vals = jnp.einsum("vabr,hrqk->vhaqbk", pick_r, by_col, precision=hi)
    ok = np.stack(row_oks)[:, None, :, None, :, None] & col_ok[None, None, None, :, None, :]
    vals = jnp.where(ok, vals, NEG)
    return vals.reshape(3, rpb.shape[0], NA_QROWS * GRID_W, NA_WROWS * GRID_W)


def _na_call(nq, nk, nv, bias, n_lat, n_ctx):
    L = nq.shape[0]
    tq = NA_QROWS * GRID_W
    n_blocks = n_lat // tq
    hp = LANES // NA_DIM
    kv_spec = pl.BlockSpec((L, LANES), lambda h, b: (0, h))

    def bias_spec(sub):
        def bias_map(h, s):
            b = s * NA_STEP_BLOCKS + sub
            return (jnp.where(b == 0, 0, jnp.where(b == n_blocks - 1, 2, 1)), h, 0, 0)
        return pl.BlockSpec((None, hp, tq, NA_WROWS * GRID_W), bias_map)

    qspec = pl.BlockSpec((NA_STEP_BLOCKS * tq, LANES), lambda h, s: (s, h))
    return pl.pallas_call(
        functools.partial(_na_kernel, n_lat, n_ctx, n_blocks),
        grid=(NA_HEADS // hp, n_blocks // NA_STEP_BLOCKS),
        in_specs=[qspec, kv_spec, kv_spec] + [bias_spec(sub) for sub in range(NA_STEP_BLOCKS)],
        out_specs=qspec,
        out_shape=jax.ShapeDtypeStruct((n_lat, NA_WIDTH), BF16),
        compiler_params=_params(("arbitrary", "arbitrary")),
        name="na",
    )(nq, nk, nv, *([bias] * NA_STEP_BLOCKS))


def _ctx_kernel(q_ref, k_ref, v_ref, nq_ref, nk_ref, nv_ref, oa_ref, ob_ref):
    for h in range(MLA_HEADS):
        sl = slice(h * LANES, (h + 1) * LANES)
        s = _dot_t(q_ref[:, sl], k_ref[:, sl])
        p = jnp.exp2(s - s.max(axis=-1, keepdims=True))
        acc = _dot(p.astype(BF16), v_ref[:, sl])
        oa_ref[:, sl] = (acc * pl.reciprocal(acc[:, MLA_V:MLA_V + 1], approx=False)).astype(oa_ref.dtype)
    nq, nk, nv = nq_ref[...], nk_ref[...], nv_ref[...]
    outs = []
    for h in range(NA_HEADS):
        sl = slice(h * NA_DIM, (h + 1) * NA_DIM)
        outs.append(_softmax_pv([(_dot_t(nq[:, sl], nk[:, sl]), nv[:, sl])]))
    ob_ref[...] = jnp.concatenate(outs, axis=-1).astype(ob_ref.dtype)


def _ctx_call(q, k, v, nq, nk, nv, n_lat, n_ctx):
    blk = n_lat // n_ctx
    a_spec = pl.BlockSpec((n_ctx, HEAD_PAD), lambda i: (blk, 0))
    b_spec = pl.BlockSpec((n_ctx, NA_WIDTH), lambda i: (blk, 0))
    return pl.pallas_call(
        _ctx_kernel,
        grid=(1,),
        in_specs=[a_spec] * 3 + [b_spec] * 3,
        out_specs=[pl.BlockSpec((n_ctx, HEAD_PAD), lambda i: (0, 0)),
                   pl.BlockSpec((n_ctx, NA_WIDTH), lambda i: (0, 0))],
        out_shape=[jax.ShapeDtypeStruct((n_ctx, HEAD_PAD), BF16), jax.ShapeDtypeStruct((n_ctx, NA_WIDTH), BF16)],
        compiler_params=_params(("arbitrary",)),
        name="ctx_attn",
    )(q, k, v, nq, nk, nv)


S5_SG = LANES // S5_GCH
S5_SW = S5_SG * S5_STATE


def _s5prep_kernel(are_ref, aim_ref, ldt_ref, bre_ref, bim_ref, cre_ref, cim_ref,
                   arec_ref, aimc_ref, ldtc_ref, crec_ref, cimc_ref,
                   d_ref, bbr_ref, bbi_ref, wcr_ref, wci_ref, atr_ref, ati_ref):
    for ref in (d_ref, bbr_ref, bbi_ref, wcr_ref, wci_ref):
        ref[...] = jnp.zeros_like(ref)
    for g in range(S5_SG):
        rows, lanes = slice(g * S5_GCH, (g + 1) * S5_GCH), slice(g * S5_STATE, (g + 1) * S5_STATE)
        a_re, a_im = are_ref[g], aim_ref[g]
        dt = jnp.exp(ldt_ref[g])
        steps = lax.broadcasted_iota(jnp.int32, (S5_T + 1, S5_STATE), 0).astype(F32)
        mag = jnp.exp(a_re * dt * steps)
        p_re, p_im = mag * jnp.cos(a_im * dt * steps), mag * jnp.sin(a_im * dt * steps)
        nr, ni = p_re[1:2] - 1.0, p_im[1:2]
        den = 1.0 / (a_re * a_re + a_im * a_im)
        f_re, f_im = (nr * a_re + ni * a_im) * den, (ni * a_re - nr * a_im) * den
        b_re, b_im = bre_ref[g], bim_ref[g]
        bb_re, bb_im = f_re * b_re - f_im * b_im, f_re * b_im + f_im * b_re
        c_re, c_im = cre_ref[g], cim_ref[g]
        pw = [(p_re[e:e + 1], p_im[e:e + 1]) for e in range(S5_T + 1)]
        w_re = jnp.concatenate([c_re * r - c_im * i for r, i in pw[:S5_T]], axis=0)
        w_im = jnp.concatenate([c_re * i + c_im * r for r, i in pw[:S5_T]], axis=0)
        kt = _dot3_t(bb_re, w_re) - _dot3_t(bb_im, w_im)
        for e in range(S5_T):
            r, i = pw[e]
            d_ref[e, rows, rows] = kt[:, e * S5_GCH:(e + 1) * S5_GCH].astype(d_ref.dtype)
            bbr_ref[e, rows, lanes] = (r * bb_re - i * bb_im).astype(bbr_ref.dtype)
            bbi_ref[e, rows, lanes] = (r * bb_im + i * bb_re).astype(bbi_ref.dtype)
        atr_ref[:, lanes], ati_ref[:, lanes] = pw[S5_T]
        ac_re, ac_im = arec_ref[g], aimc_ref[g]
        dtc = jnp.exp(ldtc_ref[g])
        steps_c = lax.broadcasted_iota(jnp.int32, (S5_STATE, S5_T), 1).astype(F32) + 1.0
        mag_c = jnp.exp(ac_re * dtc * steps_c)
        q_re, q_im = mag_c * jnp.cos(ac_im * dtc * steps_c), mag_c * jnp.sin(ac_im * dtc * steps_c)
        ct_re, ct_im = crec_ref[g], cimc_ref[g]
        for e in range(S5_T):
            r, i = q_re[:, e:e + 1], q_im[:, e:e + 1]
            wcr_ref[e, lanes, rows] = (ct_re * r - ct_im * i).astype(wcr_ref.dtype)
            wci_ref[e, lanes, rows] = (-(ct_re * i + ct_im * r)).astype(wci_ref.dtype)


def _s5prep_call(a_re, a_im, log_dt, b_re, b_im, c_re, c_im):
    nd, g, p = a_re.shape
    cg = b_re.shape[-1]
    ns = g // S5_SG
    spec = lambda *s: pl.BlockSpec((None, S5_SG) + s, lambda d, j: (d, j) + (0,) * len(s))
    out = lambda *s: pl.BlockSpec((None, None) + s, lambda d, j: (d, j) + (0,) * len(s))
    sds = lambda t, *s: jax.ShapeDtypeStruct((nd, ns) + s, t)
    tr = lambda t: jnp.swapaxes(t, -1, -2)
    ldt = jnp.broadcast_to(log_dt[:, :, None, None], (nd, g, 1, p))
    return pl.pallas_call(
        _s5prep_kernel,
        grid=(nd, ns),
        in_specs=[spec(1, p)] * 3 + [spec(cg, p)] * 4 + [spec(p, 1)] * 3 + [spec(p, cg)] * 2,
        out_specs=[out(S5_T, LANES, LANES), out(S5_T, LANES, S5_SW), out(S5_T, LANES, S5_SW),
                   out(S5_T, S5_SW, LANES), out(S5_T, S5_SW, LANES), out(1, S5_SW), out(1, S5_SW)],
        out_shape=[sds(BF16, S5_T, LANES, LANES), sds(BF16, S5_T, LANES, S5_SW), sds(BF16, S5_T, LANES, S5_SW),
                   sds(BF16, S5_T, S5_SW, LANES), sds(BF16, S5_T, S5_SW, LANES), sds(F32, 1, S5_SW),
                   sds(F32, 1, S5_SW)],
        compiler_params=_params(("arbitrary", "arbitrary")),
        name="s5_prep",
    )(a_re[:, :, None, :], a_im[:, :, None, :], ldt, tr(b_re), tr(b_im), c_re, c_im,
      a_re[:, :, :, None], a_im[:, :, :, None], tr(ldt), tr(c_re), tr(c_im))


def _chunk_rows(u_ref, kb):
    return [u_ref[pl.ds(t, kb, stride=S5_T), :].astype(BF16) for t in range(S5_T)]


def _s5a_kernel(kb, u_ref, br_ref, bi_ref, ore_ref, oim_ref):
    x = _chunk_rows(u_ref, kb)
    x2 = [jnp.concatenate(x[t:t + 2], axis=1) for t in range(0, S5_T, 2)]
    for d in range(br_ref.shape[0]):
        ahead = lambda t: S5_T - 1 - t if d == 0 else t
        pair = lambda ref, t: jnp.concatenate([ref[d, ahead(t)], ref[d, ahead(t + 1)]], axis=0)
        ore_ref[d] = sum(_dot(x2[t // 2], pair(br_ref, t)) for t in range(0, S5_T, 2))
        oim_ref[d] = sum(_dot(x2[t // 2], pair(bi_ref, t)) for t in range(0, S5_T, 2))


def _s5_blocks(L):
    nch = L // S5_T
    return nch, _pick(nch, (208, 144, 80, 72, 40, 16, 8))


def _s5a_call(u, bcr, bci):
    L = u.shape[0]
    nd, ns, _, n, ws = bcr.shape
    nch, kb = _s5_blocks(L)
    wspec = pl.BlockSpec((nd, None, S5_T, n, ws), lambda s, r: (0, s, 0, 0, 0))
    ospec = pl.BlockSpec((nd, kb, ws), lambda s, r: (0, r, s))
    return pl.pallas_call(
        functools.partial(_s5a_kernel, kb),
        grid=(ns, nch // kb),
        in_specs=[pl.BlockSpec((kb * S5_T, LANES), lambda s, r: (r, s)), wspec, wspec],
        out_specs=[ospec, ospec],
        out_shape=[jax.ShapeDtypeStruct((nd, nch, ns * ws), F32)] * 2,
        compiler_params=_params(("arbitrary", "arbitrary")),
        name="s5_chunk_inputs",
    )(u, bcr, bci)


def _s5scan_kernel(n_lat_ch, n_ctx_ch, bre_ref, bim_ref, atr_ref, ati_ref, sre_ref, sim_ref):
    d = pl.program_id(0)
    a_re, a_im = atr_ref[...], ati_ref[...]

    def segment(base, count, carry):
        def step(i, st):
            s_re, s_im = st
            k = base + jnp.where(d == 0, i, count - 1 - i)
            sre_ref[k] = s_re
            sim_ref[k] = s_im
            return (a_re * s_re - a_im * s_im + bre_ref[k], a_re * s_im + a_im * s_re + bim_ref[k])
        return lax.fori_loop(0, count, step, carry)

    zero = jnp.zeros(a_re.shape, F32)
    carry = segment(n_lat_ch, n_ctx_ch, (zero, zero))
    segment(0, n_lat_ch, carry)


def _s5scan_call(b_re, b_im, atr, ati, n_lat_ch, n_ctx_ch):
    nd, nch, w = b_re.shape
    shp = (nd, nch, w // LANES, LANES)
    spec = pl.BlockSpec((None, nch, 8, LANES), lambda d, j: (d, 0, j, 0))
    aspec = pl.BlockSpec((None, 8, LANES), lambda d, j: (d, j, 0))
    return pl.pallas_call(
        functools.partial(_s5scan_kernel, n_lat_ch, n_ctx_ch),
        grid=(nd, w // LANES // 8),
        in_specs=[spec, spec, aspec, aspec],
        out_specs=[spec, spec],
        out_shape=[jax.ShapeDtypeStruct(shp, F32)] * 2,
        compiler_params=_params(("arbitrary", "arbitrary")),
        name="s5_scan",
    )(b_re.reshape(shp), b_im.reshape(shp), atr.reshape(nd, w // LANES, LANES), ati.reshape(nd, w // LANES, LANES))


def _s5c_kernel(kb, u_ref, sre_ref, sim_ref, d_ref, wr_ref, wi_ref, y_ref, acc_ref):
    nd = d_ref.shape[0]
    x = jnp.concatenate(_chunk_rows(u_ref, kb), axis=0)
    s_re = [sre_ref[d].astype(BF16) for d in range(nd)]
    s_im = [sim_ref[d].astype(BF16) for d in range(nd)]
    for t in range(0, S5_T, 2):
        later = lambda d, tt: tt if d == 0 else S5_T - 1 - tt
        both = lambda ref, d: jnp.concatenate([ref[d, later(d, t)], ref[d, later(d, t + 1)]], axis=1)
        r = sum(_dot(s_re[d], both(wr_ref, d)) + _dot(s_im[d], both(wi_ref, d)) for d in range(nd))
        acc_ref[t * kb:(t + 1) * kb, :] = r[:, :LANES]
        acc_ref[(t + 1) * kb:(t + 2) * kb, :] = r[:, LANES:]
    for e in range(0, S5_T, 2):
        n, n1 = (S5_T - e) * kb, (S5_T - e - 1) * kb
        both = lambda d: jnp.concatenate([d_ref[d, e], d_ref[d, e + 1]], axis=1)
        fwd = _dot(x[:n], both(0))
        acc_ref[e * kb:, :] += fwd[:, :LANES]
        acc_ref[(e + 1) * kb:, :] += fwd[:n1, LANES:]
        bwd = _dot(x[e * kb:], both(1))
        acc_ref[:n, :] += bwd[:, :LANES]
        acc_ref[:n1, :] += bwd[kb:, LANES:]
    for t in range(S5_T):
        y_ref[pl.ds(t, kb, stride=S5_T), :] = acc_ref[t * kb:(t + 1) * kb, :]


def _s5c_call(u, s_re, s_im, dm, wcr, wci):
    L = u.shape[0]
    nd, ns = dm.shape[:2]
    ws = wcr.shape[3]
    nch, kb = _s5_blocks(L)
    uspec = pl.BlockSpec((kb * S5_T, LANES), lambda s, r: (r, s))
    sspec = pl.BlockSpec((nd, kb, ws), lambda s, r: (0, r, s))
    wspec = pl.BlockSpec((nd, None, S5_T, ws, LANES), lambda s, r: (0, s, 0, 0, 0))
    return pl.pallas_call(
        functools.partial(_s5c_kernel, kb),
        grid=(ns, nch // kb),
        in_specs=[uspec, sspec, sspec,
                  pl.BlockSpec((nd, None, S5_T, LANES, LANES), lambda s, r: (0, s, 0, 0, 0)), wspec, wspec],
        out_specs=uspec,
        out_shape=jax.ShapeDtypeStruct(u.shape, F32),
        scratch_shapes=[pltpu.VMEM((kb * S5_T, LANES), F32)],
        compiler_params=_params(("arbitrary", "arbitrary")),
        name="s5_outputs",
    )(u, s_re, s_im, dm, wcr, wci)


def _s5_mix(u, ops, n_lat, n_ctx):
    dm, bcr, bci, wcr, wci, atr, ati = ops
    nd = dm.shape[0]
    b_re, b_im = _s5a_call(u, bcr, bci)
    s_re, s_im = _s5scan_call(b_re, b_im, atr.reshape(nd, -1), ati.reshape(nd, -1), n_lat // S5_T, n_ctx // S5_T)
    return _s5c_call(u, s_re.reshape(b_re.shape), s_im.reshape(b_im.shape), dm, wcr, wci)


def _gelu_tanh(x):
    return 0.5 * x * (1.0 + jnp.tanh(math.sqrt(2.0 / math.pi) * (x + 0.044715 * (x * x * x))))


def _post_kernel(n_lat, tm, moe,
                 x_ref, mod_ref, oa_ref, oac_ref, ob_ref, obc_ref, ys_ref, u_ref, gate_ref,
                 wa_ref, wb_ref, wc_ref, wglu_ref, bglu_ref, d_ref, wo_ref, g2_ref, *rest):
    i = pl.program_id(0)
    d = x_ref.shape[-1]
    g = _gelu_tanh(d_ref[...] * u_ref[...] + ys_ref[...])
    oc = (g * jax.nn.sigmoid(_dot(g.astype(BF16), wglu_ref[...]) + bglu_ref[...])).astype(BF16)
    gate = gate_ref[...].astype(F32)
    lat_tile = i < n_lat // tm
    o_a = jnp.where(lat_tile, oa_ref[...], oac_ref[...])
    o_b = jnp.where(lat_tile, ob_ref[...], obc_ref[...])
    mix = (gate[:, :d] * _dot(o_a, wa_ref[...]) + gate[:, d:2 * d] * _dot(o_b, wb_ref[...])
           + gate[:, 2 * d:] * _dot(oc, wc_ref[...]))
    x_new = x_ref[...] + _row_select(i, tm, n_lat, mod_ref, 2) * _dot(mix.astype(BF16), wo_ref[...])
    h2 = (_rms(x_new, g2_ref[...]) * (1.0 + _row_select(i, tm, n_lat, mod_ref, 4))
          + _row_select(i, tm, n_lat, mod_ref, 3))
    if not moe:
        xo_ref, h2_ref = rest
    else:
        wr_ref, tri_ref, xo_ref, h2_ref, route_ref, count_ref, carry_ref = rest
        logits = _dot3(h2, wr_ref[...])
        lane = lax.broadcasted_iota(jnp.int32, logits.shape, 1)
        logits = jnp.where(lane < N_EXPERTS, logits, NEG)
        m1 = logits.max(axis=-1, keepdims=True)
        i1 = jnp.where(logits == m1, lane, LANES).min(axis=-1, keepdims=True)
        rest_l = jnp.where(lane == i1, NEG, logits)
        m2 = rest_l.max(axis=-1, keepdims=True)
        i2 = jnp.where(rest_l == m2, lane, LANES).min(axis=-1, keepdims=True)
        e2 = jnp.exp(m2 - m1)
        w1 = 1.0 / (1.0 + e2)

        @pl.when(i == 0)
        def _():
            carry_ref[...] = jnp.zeros_like(carry_ref)

        chosen = jnp.where((lane == i1) | (lane == i2), 1.0, 0.0)
        before = _dot(tri_ref[...], chosen.astype(BF16)) + carry_ref[...]
        r1 = jnp.sum(jnp.where(lane == i1, before, 0.0), axis=-1, keepdims=True)
        r2 = jnp.sum(jnp.where(lane == i2, before, 0.0), axis=-1, keepdims=True)
        total = carry_ref[...] + jnp.sum(chosen, axis=0, keepdims=True)
        carry_ref[...] = total
        count_ref[...] = jnp.broadcast_to(total, count_ref.shape)
        cols = [i1.astype(F32), i2.astype(F32), r1, r2, w1, e2 * w1]
        route = jnp.zeros(logits.shape, F32)
        for j, col in enumerate(cols):
            route = jnp.where(lane == j, col, route)
        route_ref[...] = route
    xo_ref[...] = x_new
    h2_ref[...] = h2.astype(h2_ref.dtype)


def _post_call(x, mods, o_a, o_a_c, o_b, o_b_c, ys, u, gate, lw, n_lat, moe):
    L, d = x.shape
    n_ctx = L - n_lat
    tm = _pick(math.gcd(n_lat, n_ctx), (256, 128))
    n_lat_tiles = n_lat // tm
    row = lambda w: pl.BlockSpec((tm, w), lambda i: (i, 0))
    lat = lambda w: pl.BlockSpec((tm, w), lambda i: (jnp.minimum(i, n_lat_tiles - 1), 0))
    ctx = lambda w: pl.BlockSpec((tm, w), lambda i: (jnp.maximum(i - n_lat_tiles, 0), 0))
    weights = [lw[k] for k in ("wa", "wb", "wc", "wglu", "bglu", "d", "wo", "g2")]
    if moe:
        tri = jnp.asarray(np.tril(np.ones((tm, tm), np.float32), -1), BF16)
        weights += [lw["wrouter"], tri]
        outs = [(L, d, F32), (L, d, F32), (L, LANES, F32), (8, LANES, F32)]
        out_specs = [row(d), row(d), row(LANES), pl.BlockSpec((8, LANES), lambda i: (0, 0))]
        scratch = [pltpu.VMEM((1, LANES), F32)]
    else:
        outs = [(L, d, F32), (L, d, BF16)]
        out_specs = [row(d), row(d)]
        scratch = []
    return pl.pallas_call(
        functools.partial(_post_kernel, n_lat, tm, moe),
        grid=(L // tm,),
        in_specs=([row(d), _const_spec(mods.shape), lat(HEAD_PAD), ctx(HEAD_PAD), lat(NA_WIDTH), ctx(NA_WIDTH),
                   row(S5_WIDTH), row(S5_WIDTH), row(N_BRANCH * d)] + [_const_spec(w.shape) for w in weights]),
        out_specs=out_specs,
        out_shape=[jax.ShapeDtypeStruct((r, w), t) for r, w, t in outs],
        scratch_shapes=scratch,
        compiler_params=_params(("arbitrary",)),
        name="post_moe" if moe else "post",
    )(x, mods, o_a, o_a_c, o_b, o_b_c, ys, u, gate, *weights)


def _swiglu(h, wg_ref, wu_ref, wd_ref):
    a = _dot(h, wg_ref[...])
    act = (a * jax.nn.sigmoid(a) * _dot(h, wu_ref[...])).astype(BF16)
    return _dot(act, wd_ref[...])


def _ffn_kernel(n_lat, tm, x_ref, h_ref, mod_ref, wg_ref, wu_ref, wd_ref, o_ref):
    f = _swiglu(h_ref[...], wg_ref, wu_ref, wd_ref)
    o_ref[...] = x_ref[...] + _row_select(pl.program_id(0), tm, n_lat, mod_ref, 5) * f


def _ffn_call(x, h2, mods, wg, wu, wd, n_lat):
    L, d = x.shape
    tm = _pick(L, (640, 256, 128))
    row = lambda w: pl.BlockSpec((tm, w), lambda i: (i, 0))
    return pl.pallas_call(
        functools.partial(_ffn_kernel, n_lat, tm),
        grid=(L // tm,),
        in_specs=[row(d), row(d), _const_spec(mods.shape), _const_spec(wg.shape), _const_spec(wu.shape),
                  _const_spec(wd.shape)],
        out_specs=row(d),
        out_shape=jax.ShapeDtypeStruct((L, d), F32),
        compiler_params=_params(("arbitrary",)),
        name="ffn",
    )(x, h2, mods, wg, wu, wd)


MOE_TS = 512


def _row_copy(src, i, dst, j, sem):
    return pltpu.make_async_copy(src.at[pl.ds(i, 1)], dst.at[pl.ds(j, 1)], sem)


ROW_DMA_UNROLL = 8


def _dispatch_kernel(tb, n_tok, slot_ref, h_ref, xs_in, xs_hbm, sem):
    del xs_in
    base = pl.program_id(0) * tb

    def issue(t, carry):
        for k in range(TOP_K):
            _row_copy(h_ref, t, xs_hbm, slot_ref[k * n_tok + base + t], sem).start(priority=k % 2)
        return carry

    def drain(t, carry):
        for k in range(TOP_K):
            _row_copy(h_ref, 0, xs_hbm, 0, sem).wait()
        return carry

    lax.fori_loop(0, tb, issue, 0, unroll=ROW_DMA_UNROLL)
    lax.fori_loop(0, tb, drain, 0, unroll=ROW_DMA_UNROLL)


def _dispatch_call(slots, h2, n_slots):
    L, d = h2.shape
    tb = _pick(L, (640, 256, 128))
    xs0 = jnp.zeros((n_slots, d), h2.dtype)
    return pl.pallas_call(
        functools.partial(_dispatch_kernel, tb, L),
        grid_spec=pltpu.PrefetchScalarGridSpec(
            num_scalar_prefetch=1, grid=(L // tb,),
            in_specs=[pl.BlockSpec((tb, d), lambda i, s: (i, 0)), pl.BlockSpec(memory_space=pl.ANY)],
            out_specs=pl.BlockSpec(memory_space=pl.ANY),
            scratch_shapes=[pltpu.SemaphoreType.DMA(())]),
        out_shape=jax.ShapeDtypeStruct((n_slots, d), h2.dtype),
        input_output_aliases={2: 0},
        compiler_params=_params(("arbitrary",)),
        name="moe_dispatch",
    )(slots, h2, xs0)


def _expert_kernel(te_ref, nu_ref, x_ref, wg_ref, wu_ref, wd_ref, o_ref):
    used = pl.program_id(0) < nu_ref[0]

    @pl.when(used)
    def _():
        o_ref[...] = _swiglu(x_ref[...].astype(BF16), wg_ref, wu_ref, wd_ref)

    @pl.when(jnp.logical_not(used))
    def _():
        o_ref[...] = jnp.zeros_like(o_ref)


def _expert_call(tile_expert, n_used, xs, wg, wu, wd, layer):
    n_slots, d = xs.shape
    dff = wg.shape[-1]
    once = pl.Buffered(1)
    return pl.pallas_call(
        _expert_kernel,
        grid_spec=pltpu.PrefetchScalarGridSpec(
            num_scalar_prefetch=2, grid=(n_slots // MOE_TS,),
            in_specs=[pl.BlockSpec((MOE_TS, d), lambda j, te, nu: (j, 0)),
                      pl.BlockSpec((None, None, d, dff), lambda j, te, nu: (layer, te[j], 0, 0), pipeline_mode=once),
                      pl.BlockSpec((None, None, d, dff), lambda j, te, nu: (layer, te[j], 0, 0), pipeline_mode=once),
                      pl.BlockSpec((None, None, dff, d), lambda j, te, nu: (layer, te[j], 0, 0), pipeline_mode=once)],
            out_specs=pl.BlockSpec((MOE_TS, d), lambda j, te, nu: (j, 0))),
        out_shape=jax.ShapeDtypeStruct((n_slots, d), F32),
        compiler_params=_params(("arbitrary",)),
        name="moe_experts",
    )(tile_expert, n_used, xs, wg, wu, wd)


def _combine_kernel(n_lat, tb, n_tok, slot_ref, x_ref, route_ref, mod_ref, zs_hbm, o_ref, g_ref, sem):
    i = pl.program_id(0)
    base = i * tb

    def issue(t, carry):
        for k in range(TOP_K):
            _row_copy(zs_hbm, slot_ref[k * n_tok + base + t], g_ref.at[k], t, sem).start(priority=k % 2)
        return carry

    def drain(t, carry):
        for k in range(TOP_K):
            _row_copy(zs_hbm, 0, g_ref.at[k], 0, sem).wait()
        return carry

    lax.fori_loop(0, tb, issue, 0, unroll=ROW_DMA_UNROLL)
    lax.fori_loop(0, tb, drain, 0, unroll=ROW_DMA_UNROLL)
    route = route_ref[...]
    f = route[:, 4:5] * g_ref[0] + route[:, 5:6] * g_ref[1]
    o_ref[...] = x_ref[...] + _row_select(i, tb, n_lat, mod_ref, 5) * f


def _combine_call(slots, x, route, mods, zs, n_lat):
    L, d = x.shape
    tb = _pick(L, (320, 256, 128))
    row = lambda w: pl.BlockSpec((tb, w), lambda i, s: (i, 0))
    return pl.pallas_call(
        functools.partial(_combine_kernel, n_lat, tb, L),
        grid_spec=pltpu.PrefetchScalarGridSpec(
            num_scalar_prefetch=1, grid=(L // tb,),
            in_specs=[row(d), row(LANES), pl.BlockSpec(mods.shape, lambda i, s: (0, 0, 0)),
                      pl.BlockSpec(memory_space=pl.ANY)],
            out_specs=row(d),
            scratch_shapes=[pltpu.VMEM((TOP_K, tb, d), F32), pltpu.SemaphoreType.DMA(())]),
        out_shape=jax.ShapeDtypeStruct((L, d), F32),
        compiler_params=_params(("arbitrary",)),
        name="moe_combine",
    )(slots, x, route, mods, zs)


def _moe_call(x, h2, route, counts, mods, wg, wu, wd, layer, n_lat):
    L, d = x.shape
    ne = wg.shape[1]
    n_slots = (pl.cdiv(TOP_K * L, MOE_TS) + ne) * MOE_TS
    cnt = counts[0, :ne].astype(jnp.int32)
    size = (cnt + MOE_TS - 1) // MOE_TS * MOE_TS
    ends = jnp.cumsum(size)
    offs = ends - size
    eid = jnp.arange(ne, dtype=jnp.int32)
    slot = lambda e, r: jnp.sum(jnp.where(e[:, None] == eid[None, :], offs[None, :], 0), axis=1) + r
    r = route.astype(jnp.int32)
    slots = jnp.concatenate([slot(r[:, 0], r[:, 2]), slot(r[:, 1], r[:, 3])])
    starts = jnp.arange(n_slots // MOE_TS, dtype=jnp.int32) * MOE_TS
    tile_expert = jnp.minimum(jnp.sum(starts[:, None] >= ends[None, :], axis=1), ne - 1).astype(jnp.int32)
    n_used = (ends[-1:] // MOE_TS).astype(jnp.int32)
    xs = _dispatch_call(slots, h2, n_slots)
    zs = _expert_call(tile_expert, n_used, xs, wg, wu, wd, layer)
    return _combine_call(slots, x, route, mods, zs, n_lat)


def _rope_tables(n_lat, n_ctx):
    t = jnp.arange(n_lat, dtype=jnp.int32)
    n_freq = MLA_ROPE // 4
    inv = ROPE_THETA ** (-jnp.arange(n_freq, dtype=F32) / n_freq)
    ang = jnp.concatenate([(t // GRID_W).astype(F32)[:, None] * inv[None],
                           (t % GRID_W).astype(F32)[:, None] * inv[None]], axis=-1)
    cos, sin = jnp.cos(ang), jnp.sin(ang)
    ones = jnp.ones((n_lat, MLA_NOPE), F32)
    zn = jnp.zeros((n_lat, MLA_NOPE), F32)
    zh = jnp.zeros_like(sin)
    zp = jnp.zeros((n_lat, LANES - MLA_QK), F32)
    c = jnp.concatenate([ones, cos, cos, zp], axis=-1)
    s1 = jnp.concatenate([zn, -sin, zh, zp], axis=-1)
    s2 = jnp.concatenate([zn, zh, sin, zp], axis=-1)
    ctx = lambda a, fill: jnp.concatenate([a, jnp.full((n_ctx, LANES), fill, F32)], axis=0)
    return ctx(c, 1.0), ctx(s1, 0.0), ctx(s2, 0.0)


def _head_pad_cols(w, width):
    r = w.shape[0]
    w = w.reshape(r, MLA_HEADS, width)
    return jnp.pad(w, ((0, 0), (0, 0), (0, LANES - width))).reshape(r, HEAD_PAD)


def _layer_weights(i, p):
    d = D_MODEL
    w_in = p["w_in"][i]
    cuts = np.cumsum((0,) + IN_SPLITS)
    piece = lambda j: w_in[:, cuts[j]:cuts[j + 1]]
    bf = lambda w: w.astype(BF16)
    row = lambda v: v.reshape(1, -1).astype(F32)
    ukv = p["w_mla_ukv"][i].reshape(MLA_KV_RANK, MLA_HEADS, MLA_NOPE + MLA_V)
    wk_nope = _head_pad_cols(ukv[:, :, :MLA_NOPE].reshape(MLA_KV_RANK, -1), MLA_NOPE)
    kr_place = jnp.zeros((LANES, MLA_HEADS, LANES), F32)
    eye = jnp.eye(MLA_ROPE, dtype=F32)
    kr_place = kr_place.at[:MLA_ROPE, :, MLA_NOPE:MLA_QK].set(jnp.broadcast_to(eye[:, None, :], (MLA_ROPE, MLA_HEADS, MLA_ROPE)))
    vone = jnp.zeros((MLA_HEADS, LANES), F32).at[:, MLA_V].set(1.0).reshape(1, HEAD_PAD)
    head_gain = lambda g: jnp.tile(jnp.pad(g, (0, LANES - MLA_QK)), MLA_HEADS).reshape(1, HEAD_PAD)
    e64 = jnp.kron(jnp.eye(NA_HEADS, dtype=F32), jnp.ones((NA_DIM, NA_DIM), F32))
    wa = jnp.pad(p["w_br_mla"][i].reshape(MLA_HEADS, MLA_V, d), ((0, 0), (0, LANES - MLA_V), (0, 0)))
    return {
        "g1": row(p["g_norm1"][i]), "g2": row(p["g_norm2"][i]),
        "wq": bf(piece(0)), "wkv": bf(piece(1)),
        "wkr": bf(jnp.pad(piece(2), ((0, 0), (0, LANES - MLA_ROPE)))),
        "wnq": bf(piece(3)), "wnk": bf(piece(4)), "wnv": bf(piece(5)), "wu": bf(piece(6)), "wg": bf(piece(7)),
        "gq": row(p["g_mla_q"][i]), "gkv": row(p["g_mla_kv"][i]),
        "wuq": bf(_head_pad_cols(p["w_mla_uq"][i], MLA_QK)),
        "wk": bf(jnp.concatenate([wk_nope, kr_place.reshape(LANES, HEAD_PAD)], axis=0)),
        "wv": bf(_head_pad_cols(ukv[:, :, MLA_NOPE:].reshape(MLA_KV_RANK, -1), MLA_V)),
        "vone": vone,
        "gqn": head_gain(p["g_mla_qn"][i]), "gkn": head_gain(p["g_mla_kn"][i]),
        "e64": bf(e64),
        "gnq": jnp.tile(p["g_na_qn"][i], NA_HEADS).reshape(1, -1), "gnk": jnp.tile(p["g_na_kn"][i], NA_HEADS).reshape(1, -1),
        "wa": bf(wa.reshape(HEAD_PAD, d)), "wb": bf(p["w_br_na"][i]), "wc": bf(p["w_br_s5"][i]),
        "wglu": bf(p["w_glu"][i]), "bglu": row(p["b_glu"][i]), "d": row(p["s5_d"][i]), "wo": bf(p["w_out"][i]),
    }


def kernel(x, c, ctx, c_ctx, w_mod, b_mod, g_norm1, g_norm2, w_in, g_mla_q, g_mla_kv, w_mla_uq, w_mla_ukv,
           g_mla_qn, g_mla_kn, g_na_qn, g_na_kn, na_rpb, s5_a_re, s5_a_im, s5_log_dt, s5_b_re, s5_b_im,
           s5_c_re, s5_c_im, s5_d, w_glu, b_glu, w_br_mla, w_br_na, w_br_s5, w_out, w_ffn_gate, w_ffn_up,
           w_ffn_down, w_router, w_exp_gate, w_exp_up, w_exp_down):
    p = dict(w_in=w_in, g_norm1=g_norm1, g_norm2=g_norm2, g_mla_q=g_mla_q, g_mla_kv=g_mla_kv, w_mla_uq=w_mla_uq,
             w_mla_ukv=w_mla_ukv, g_mla_qn=g_mla_qn, g_mla_kn=g_mla_kn, g_na_qn=g_na_qn, g_na_kn=g_na_kn,
             s5_d=s5_d, w_glu=w_glu, b_glu=b_glu, w_br_mla=w_br_mla, w_br_na=w_br_na, w_br_s5=w_br_s5, w_out=w_out)
    assert x.shape[0] == 1 and x.shape[2] == D_MODEL
    n_lat, n_ctx = x.shape[1], ctx.shape[1]
    depth = w_mod.shape[0]
    xs = jnp.concatenate([x[0], ctx[0]], axis=0)

    cvec = jnp.zeros((8, D_MODEL), F32).at[0].set(c[0]).at[1].set(c_ctx)
    mods_all = _mod_call(cvec, w_mod, b_mod)[:, :2, None, :]
    rope = _rope_tables(n_lat, n_ctx)
    experts = (w_exp_gate.astype(BF16), w_exp_up.astype(BF16), w_exp_down.astype(BF16))

    for i in range(depth):
        lw = _layer_weights(i, p)
        mods = mods_all[i]
        q, k, v, nq, nk, nv, u, gate, qmax, kmax = _pre_call(xs, mods, n_lat, lw, rope)

        o_a = _mla_attend(q, k, v, qmax, kmax, n_lat)
        bias = _na_bias(na_rpb[i], n_lat // GRID_W)
        o_b = _na_call(nq, nk, nv, bias, n_lat, n_ctx)
        o_a_c, o_b_c = _ctx_call(q, k, v, nq, nk, nv, n_lat, n_ctx)
        attn = (o_a, o_a_c, o_b, o_b_c)

        ops = _s5prep_call(s5_a_re[i], s5_a_im[i], s5_log_dt[i], s5_b_re[i], s5_b_im[i], s5_c_re[i], s5_c_im[i])
        ys = _s5_mix(u, ops, n_lat, n_ctx)

        moe = i % 2 == 1
        j = i // 2
        if moe:
            lw["wrouter"] = jnp.pad(w_router[j], ((0, 0), (0, LANES - N_EXPERTS)))
            xs, h2, route, counts = _post_call(xs, mods, *attn, ys, u, gate, lw, n_lat, True)
            xs = _moe_call(xs, h2, route, counts, mods, *experts, j, n_lat)
        else:
            xs, h2 = _post_call(xs, mods, *attn, ys, u, gate, lw, n_lat, False)
            xs = _ffn_call(xs, h2, mods, w_ffn_gate[j].astype(BF16), w_ffn_up[j].astype(BF16),
                           w_ffn_down[j].astype(BF16), n_lat)
    return xs[:n_lat][None]
```

```python
import functools
import math

import jax
import jax.numpy as jnp
import numpy as np
from jax import lax
from jax.experimental import pallas as pl
from jax.experimental.pallas import tpu as pltpu

F32 = jnp.float32
BF16 = jnp.bfloat16

D_MODEL = 1024
DEPTH = 4
GRID_W = 64
EPS = 1e-6
ROPE_THETA = 10000.0

MLA_HEADS = 8
MLA_NOPE = 64
MLA_ROPE = 32
MLA_V = 64
MLA_Q_RANK = 384
MLA_KV_RANK = 256
MLA_QK = MLA_NOPE + MLA_ROPE

NA_HEADS = 8
NA_DIM = 64
NA_KH = 8
NA_KW = 16
NA_WIDTH = NA_HEADS * NA_DIM

S5_GROUPS = 32
S5_GCH = 16
S5_STATE = 64
S5_WIDTH = S5_GROUPS * S5_GCH

N_BRANCH = 3
D_FF = 2816
N_EXPERTS = 8
TOP_K = 2

IN_SPLITS = (MLA_Q_RANK, MLA_KV_RANK, MLA_ROPE, NA_WIDTH, NA_WIDTH, NA_WIDTH, S5_WIDTH, N_BRANCH * D_MODEL)

LANES = 128
HEAD_PAD = MLA_HEADS * LANES
S5_T = 16
NA_QROWS = 4
NA_WROWS = NA_QROWS + NA_KH - 1
VMEM_LIMIT = 56 * 1024 * 1024
NEG = -1e30


def _pick(n, candidates):
    for c in candidates:
        if n % c == 0:
            return c
    raise ValueError(f"no tile in {candidates} divides {n}")


def _const_spec(shape):
    nd = len(shape)
    return pl.BlockSpec(shape, lambda *_: (0,) * nd, pipeline_mode=pl.Buffered(1))


def _params(sem):
    return pltpu.CompilerParams(dimension_semantics=sem, vmem_limit_bytes=VMEM_LIMIT)


def _dot(a, b):
    return jnp.dot(a, b, preferred_element_type=F32)


def _dot_t(a, b):
    return lax.dot_general(a, b, (((1,), (1,)), ((), ())), preferred_element_type=F32)


def _split(x):
    hi = x.astype(BF16)
    lo = (x - hi.astype(F32)).astype(BF16)
    return hi, lo


def _dot3(a, b):
    a_hi, a_lo = _split(a)
    b_hi, b_lo = _split(b)
    return _dot(a_hi, b_hi) + _dot(a_hi, b_lo) + _dot(a_lo, b_hi)


def _dot3_t(a, b):
    a_hi, a_lo = _split(a)
    b_hi, b_lo = _split(b)
    return _dot_t(a_hi, b_hi) + _dot_t(a_hi, b_lo) + _dot_t(a_lo, b_hi)


def _rms(x, g):
    ms = jnp.mean(x * x, axis=-1, keepdims=True)
    return x * lax.rsqrt(ms + EPS) * g


def _row_select(tile, tm, n_lat, mod_ref, idx):
    rows = tile * tm + lax.broadcasted_iota(jnp.int32, (tm, 1), 0)
    lat = mod_ref[0, :, idx * D_MODEL:(idx + 1) * D_MODEL]
    ctx = mod_ref[1, :, idx * D_MODEL:(idx + 1) * D_MODEL]
    return jnp.where(rows < n_lat, lat, ctx)


def _mod_kernel(c_ref, w_ref, b_ref, o_ref):
    c = c_ref[...]
    a = c * jax.nn.sigmoid(c)
    o_ref[...] = _dot3(a, w_ref[...]) + b_ref[...]


def _mod_call(cvec, w_mod, b_mod):
    depth, d, n6 = w_mod.shape
    tn = _pick(n6, (1536, 1024, 512, 128))
    return pl.pallas_call(
        _mod_kernel,
        grid=(depth, n6 // tn),
        in_specs=[
            pl.BlockSpec((8, d), lambda l, j: (0, 0)),
            pl.BlockSpec((None, d, tn), lambda l, j: (l, 0, j)),
            pl.BlockSpec((None, 1, tn), lambda l, j: (l, 0, j)),
        ],
        out_specs=pl.BlockSpec((None, 8, tn), lambda l, j: (l, 0, j)),
        out_shape=jax.ShapeDtypeStruct((depth, 8, n6), F32),
        compiler_params=_params(("arbitrary", "arbitrary")),
        name="mod",
    )(cvec, w_mod, b_mod.reshape(depth, 1, n6))


def _head_norm_rope(xp, gain, c_t, s1_t, s2_t, scale, o_ref, nmax_ref, first, carry_norm):
    lane = lax.broadcasted_iota(jnp.int32, (1, LANES), 1)
    for h in range(MLA_HEADS):
        sl = slice(h * LANES, (h + 1) * LANES)
        xh = xp[:, sl]
        ms = jnp.sum(xh * xh, axis=-1, keepdims=True) * (1.0 / MLA_QK)
        y = xh * lax.rsqrt(ms + EPS) * gain[:, sl]
        y = y * c_t + pltpu.roll(y, LANES - MLA_ROPE // 2, 1) * s1_t + pltpu.roll(y, MLA_ROPE // 2, 1) * s2_t
        if scale != 1.0:
            y = y * scale
        norm = jnp.sqrt(jnp.sum(y * y, axis=-1, keepdims=True))
        tile_max = jnp.broadcast_to(jnp.max(norm, axis=0, keepdims=True), (1, LANES))
        nmax_ref[h:h + 1, :] = jnp.maximum(jnp.where(first, 0.0, nmax_ref[h:h + 1, :]), tile_max)
        if carry_norm:
            y = jnp.where(lane == MLA_QK, -norm, y)
        o_ref[:, sl] = y.astype(o_ref.dtype)


def _pre_kernel(n_lat, tm, mla_scale, na_scale,
                x_ref, mod_ref, g1_ref, wq_ref, wkv_ref, wkr_ref, wnq_ref, wnk_ref, wnv_ref, wu_ref, wg_ref,
                gq_ref, gkv_ref, wuq_ref, wk_ref, wv_ref, vone_ref, gqn_ref, gkn_ref,
                rc_ref, rs1_ref, rs2_ref, e64_ref, gnq_ref, gnk_ref,
                q_ref, k_ref, v_ref, nq_ref, nk_ref, nv_ref, u_ref, gate_ref, qmax_ref, kmax_ref):
    i = pl.program_id(0)
    sh = _row_select(i, tm, n_lat, mod_ref, 0)
    sc = _row_select(i, tm, n_lat, mod_ref, 1)
    h = (_rms(x_ref[...], g1_ref[...]) * (1.0 + sc) + sh).astype(BF16)

    c_t, s1_t, s2_t = rc_ref[...], rs1_ref[...], rs2_ref[...]

    rq = _rms(_dot(h, wq_ref[...]), gq_ref[...]).astype(BF16)
    _head_norm_rope(_dot(rq, wuq_ref[...]), gqn_ref[...], c_t, s1_t, s2_t, mla_scale, q_ref, qmax_ref, i == 0, True)
    rkv = _rms(_dot(h, wkv_ref[...]), gkv_ref[...]).astype(BF16)
    pkr = _dot(h, wkr_ref[...]).astype(BF16)
    kin = jnp.concatenate([rkv, pkr], axis=-1)
    _head_norm_rope(_dot(kin, wk_ref[...]), gkn_ref[...], c_t, s1_t, s2_t, 1.0, k_ref, kmax_ref, i == 0, False)
    v_ref[...] = (_dot(rkv, wv_ref[...]) + vone_ref[...]).astype(BF16)

    def na_norm(w_ref, g_ref, scale):
        p = _dot(h, w_ref[...])
        sq_hi, sq_lo = _split(p * p)
        ss = (_dot(sq_hi, e64_ref[...]) + _dot(sq_lo, e64_ref[...])) * (1.0 / NA_DIM)
        return (p * lax.rsqrt(ss + EPS) * (g_ref[...] * scale)).astype(BF16)

    nq_ref[...] = na_norm(wnq_ref, gnq_ref, na_scale)
    nk_ref[...] = na_norm(wnk_ref, gnk_ref, 1.0)
    nv_ref[...] = _dot(h, wnv_ref[...]).astype(BF16)

    u_ref[...] = _dot(h, wu_ref[...])
    gate_ref[...] = jax.nn.sigmoid(_dot(h, wg_ref[...])).astype(BF16)


def _pre_call(x, mods, n_lat, lw, rope):
    L, d = x.shape
    tm = _pick(L, (320, 256, 128))
    row = lambda w: pl.BlockSpec((tm, w), lambda i: (i, 0))
    weights = [lw[k] for k in ("g1", "wq", "wkv", "wkr", "wnq", "wnk", "wnv", "wu", "wg", "gq", "gkv", "wuq",
                               "wk", "wv", "vone", "gqn", "gkn")]
    tail = [lw["e64"], lw["gnq"], lw["gnk"]]
    in_specs = ([row(d), _const_spec(mods.shape)] + [_const_spec(w.shape) for w in weights]
                + [row(LANES)] * 3 + [_const_spec(w.shape) for w in tail])
    outs = [(HEAD_PAD, BF16)] * 3 + [(NA_WIDTH, BF16)] * 3 + [(S5_WIDTH, F32), (N_BRANCH * d, BF16)]
    return pl.pallas_call(
        functools.partial(_pre_kernel, n_lat, tm, MLA_QK ** -0.5 * math.log2(math.e), NA_DIM ** -0.5),
        grid=(L // tm,),
        in_specs=in_specs,
        out_specs=[row(w) for w, _ in outs] + [pl.BlockSpec((MLA_HEADS, LANES), lambda i: (0, 0))] * 2,
        out_shape=([jax.ShapeDtypeStruct((L, w), t) for w, t in outs]
                   + [jax.ShapeDtypeStruct((MLA_HEADS, LANES), F32)] * 2),
        compiler_params=_params(("arbitrary",)),
        name="pre",
    )(x, mods, *weights, *rope, *tail)


def _mla_kernel(tq, tk, n_chunks, q_ref, k_ref, v_ref, o_ref, m_ref, acc_ref, sa_ref, sb_ref):
    m_ref[...] = jnp.full_like(m_ref, NEG)
    acc_ref[...] = jnp.zeros_like(acc_ref)
    q = q_ref[...]

    def scores(c):
        off = pl.multiple_of(c * tk, tk)
        return _dot_t(q, k_ref[pl.ds(off, tk), :])

    def absorb(s, c):
        off = pl.multiple_of(c * tk, tk)
        m_prev = m_ref[...]
        m_new = jnp.maximum(m_prev, jnp.max(s, axis=-1, keepdims=True))
        p = jnp.exp2(s - jnp.concatenate([m_new] * (tk // LANES), axis=1))
        acc_ref[...] = jnp.exp2(m_prev - m_new) * acc_ref[...] + _dot(p.astype(BF16), v_ref[pl.ds(off, tk), :])
        m_ref[...] = m_new

    sa_ref[...] = scores(0)

    def body(i, carry):
        c = 2 * i
        sb_ref[...] = scores(c + 1)
        absorb(sa_ref[...], c)
        sa_ref[...] = scores(c + 2)
        absorb(sb_ref[...], c + 1)
        return carry

    lax.fori_loop(0, (n_chunks - 1) // 2, body, 0)
    if n_chunks % 2 == 0:
        sb_ref[...] = scores(n_chunks - 1)
        absorb(sa_ref[...], n_chunks - 2)
        absorb(sb_ref[...], n_chunks - 1)
    else:
        absorb(sa_ref[...], n_chunks - 1)
    acc = acc_ref[...]
    o_ref[...] = (acc * pl.reciprocal(acc[:, MLA_V:MLA_V + 1], approx=False)).astype(o_ref.dtype)


MLA_SHIFT_MAX = 60.0


def _mla_shifted_kernel(tq, tk, n_chunks, kmax_ref, q_ref, k_ref, v_ref, o_ref, ks_ref, acc_ref, sa_ref, sb_ref):
    @pl.when(pl.program_id(1) == 0)
    def _():
        lane = lax.broadcasted_iota(jnp.int32, (1, LANES), 1)
        ks_ref[...] = jnp.where(lane == MLA_QK, kmax_ref[...].astype(BF16), k_ref[...])

    acc_ref[...] = jnp.zeros_like(acc_ref)
    q = q_ref[...]

    def scores(c):
        off = pl.multiple_of(c * tk, tk)
        return _dot_t(q, ks_ref[pl.ds(off, tk), :])

    def absorb(s, c):
        off = pl.multiple_of(c * tk, tk)
        acc_ref[...] += _dot(jnp.exp2(s).astype(BF16), v_ref[pl.ds(off, tk), :])

    sa_ref[...] = scores(0)

    def body(i, carry):
        c = 2 * i
        sb_ref[...] = scores(c + 1)
        absorb(sa_ref[...], c)
        sa_ref[...] = scores(c + 2)
        absorb(sb_ref[...], c + 1)
        return carry

    lax.fori_loop(0, (n_chunks - 1) // 2, body, 0)
    if n_chunks % 2 == 0:
        sb_ref[...] = scores(n_chunks - 1)
        absorb(sa_ref[...], n_chunks - 2)
        absorb(sb_ref[...], n_chunks - 1)
    else:
        absorb(sa_ref[...], n_chunks - 1)
    acc = acc_ref[...]
    o_ref[...] = (acc * pl.reciprocal(acc[:, MLA_V:MLA_V + 1], approx=False)).astype(o_ref.dtype)


def _mla_shifted_call(q, k, v, kmax, n_lat):
    L = q.shape[0]
    tq = _pick(n_lat, (2048, 1024, 512, 256))
    tk = _pick(L, (1280, 768, 512, 256))
    kv_spec = pl.BlockSpec((L, LANES), lambda h, i: (0, h))
    return pl.pallas_call(
        functools.partial(_mla_shifted_kernel, tq, tk, L // tk),
        grid=(MLA_HEADS, n_lat // tq),
        in_specs=[pl.BlockSpec((None, 1, LANES), lambda h, i: (h, 0, 0)),
                  pl.BlockSpec((tq, LANES), lambda h, i: (i, h)), kv_spec, kv_spec],
        out_specs=pl.BlockSpec((tq, LANES), lambda h, i: (i, h)),
        out_shape=jax.ShapeDtypeStruct((n_lat, HEAD_PAD), BF16),
        scratch_shapes=[pltpu.VMEM((L, LANES), BF16), pltpu.VMEM((tq, LANES), F32),
                        pltpu.VMEM((tq, tk), F32), pltpu.VMEM((tq, tk), F32)],
        compiler_params=_params(("arbitrary", "arbitrary")),
        name="mla_shifted",
    )(kmax.reshape(MLA_HEADS, 1, LANES), q, k, v)


def _mla_attend(q, k, v, qmax, kmax, n_lat):
    shift = jnp.max(qmax[:, 0] * kmax[:, 0])
    return lax.cond(shift <= MLA_SHIFT_MAX, lambda: _mla_shifted_call(q, k, v, kmax, n_lat),
                    lambda: _mla_call(q, k, v, n_lat))


def _mla_call(q, k, v, n_lat):
    L = q.shape[0]
    tq = _pick(n_lat, (1024, 512, 256))
    tk = _pick(L, (1280, 768, 512, 256))
    kv_spec = pl.BlockSpec((L, LANES), lambda h, i: (0, h))
    return pl.pallas_call(
        functools.partial(_mla_kernel, tq, tk, L // tk),
        grid=(MLA_HEADS, n_lat // tq),
        in_specs=[pl.BlockSpec((tq, LANES), lambda h, i: (i, h)), kv_spec, kv_spec],
        out_specs=pl.BlockSpec((tq, LANES), lambda h, i: (i, h)),
        out_shape=jax.ShapeDtypeStruct((n_lat, HEAD_PAD), BF16),
        scratch_shapes=[pltpu.VMEM((tq, LANES), F32), pltpu.VMEM((tq, LANES), F32),
                        pltpu.VMEM((tq, tk), F32), pltpu.VMEM((tq, tk), F32)],
        compiler_params=_params(("arbitrary", "arbitrary")),
        name="mla",
    )(q, k, v)


def _softmax_pv(parts):
    m = parts[0][0].max(axis=-1, keepdims=True)
    for s, _ in parts[1:]:
        m = jnp.maximum(m, s.max(axis=-1, keepdims=True))
    num, den = 0.0, 0.0
    for s, v in parts:
        p = jnp.exp(s - m)
        den = den + p.sum(axis=-1, keepdims=True)
        num = num + _dot(p.astype(BF16), v)
    return num * pl.reciprocal(den, approx=False)


NA_STEP_BLOCKS = 4


def _na_kernel(n_lat, n_ctx, n_blocks, q_ref, k_ref, v_ref, *rest):
    bias_refs, o_ref = rest[:NA_STEP_BLOCKS], rest[NA_STEP_BLOCKS]
    wtok, tq = NA_WROWS * GRID_W, NA_QROWS * GRID_W
    kc, vc = k_ref[n_lat:n_lat + n_ctx, :], v_ref[n_lat:n_lat + n_ctx, :]
    lane = lax.broadcasted_iota(jnp.int32, (1, LANES), 1)
    for sub in range(NA_STEP_BLOCKS):
        b = pl.program_id(1) * NA_STEP_BLOCKS + sub
        row0 = jnp.clip(b * NA_QROWS - NA_KH // 2, 0, n_lat // GRID_W - NA_WROWS)
        off = pl.multiple_of(row0 * GRID_W, GRID_W)
        q = q_ref[sub * tq:(sub + 1) * tq, :]
        kw, vw = k_ref[pl.ds(off, wtok), :], v_ref[pl.ds(off, wtok), :]
        out = jnp.zeros(q.shape, F32)
        for j in range(LANES // NA_DIM):
            head = (lane >= j * NA_DIM) & (lane < (j + 1) * NA_DIM)
            qh = jnp.where(head, q, jnp.zeros_like(q))
            s_win = _dot_t(qh, kw) + bias_refs[sub][j]
            s_ctx = _dot_t(qh, kc)
            out = jnp.where(head, _softmax_pv([(s_ctx, vc), (s_win, vw)]), out)
        o_ref[sub * tq:(sub + 1) * tq, :] = out.astype(o_ref.dtype)


def _na_bias(rpb, n_rows):
    n_blocks = n_rows // NA_QROWS
    cols = np.arange(GRID_W)
    c0 = np.clip(cols - NA_KW // 2, 0, GRID_W - NA_KW)
    col_ok = (cols[None, :] >= c0[:, None]) & (cols[None, :] < c0[:, None] + NA_KW)
    dc = cols[None, :] - cols[:, None] + NA_KW - 1
    pick_c = (col_ok[:, :, None] & (dc[:, :, None] == np.arange(2 * NA_KW - 1))).astype(np.float32)
    pick_r = np.zeros((3, NA_QROWS, NA_WROWS, 2 * NA_KH - 1), np.float32)
    row_oks = []
    for v, b in enumerate((0, 1, n_blocks - 1)):
        row0 = min(max(b * NA_QROWS - NA_KH // 2, 0), n_rows - NA_WROWS)
        qr = b * NA_QROWS + np.arange(NA_QROWS)
        kr = row0 + np.arange(NA_WROWS)
        r0 = np.clip(qr - NA_KH // 2, 0, n_rows - NA_KH)
        row_ok = (kr[None, :] >= r0[:, None]) & (kr[None, :] < r0[:, None] + NA_KH)
        dr = kr[None, :] - qr[:, None] + NA_KH - 1
        pick_r[v] = row_ok[:, :, None] & (dr[:, :, None] == np.arange(2 * NA_KH - 1))
        row_oks.append(row_ok)
    hi = lax.Precision.HIGHEST
    by_col = jnp.einsum("hrd,qkd->hrqk", rpb.astype(F32), pick_c, precision=hi)
    vals = jnp.einsum("vabr,hrqk->vhaqbk", pick_r, by_col, precision=hi)
    ok = np.stack(row_oks)[:, None, :, None, :, None] & col_ok[None, None, None, :, None, :]
    vals = jnp.where(ok, vals, NEG)
    return vals.reshape(3, rpb.shape[0], NA_QROWS * GRID_W, NA_WROWS * GRID_W)


def _na_call(nq, nk, nv, bias, n_lat, n_ctx):
    L = nq.shape[0]
    tq = NA_QROWS * GRID_W
    n_blocks = n_lat // tq
    hp = LANES // NA_DIM
    kv_spec = pl.BlockSpec((L, LANES), lambda h, b: (0, h))

    def bias_spec(sub):
        def bias_map(h, s):
            b = s * NA_STEP_BLOCKS + sub
            return (jnp.where(b == 0, 0, jnp.where(b == n_blocks - 1, 2, 1)), h, 0, 0)
        return pl.BlockSpec((None, hp, tq, NA_WROWS * GRID_W), bias_map)

    qspec = pl.BlockSpec((NA_STEP_BLOCKS * tq, LANES), lambda h, s: (s, h))
    return pl.pallas_call(
        functools.partial(_na_kernel, n_lat, n_ctx, n_blocks),
        grid=(NA_HEADS // hp, n_blocks // NA_STEP_BLOCKS),
        in_specs=[qspec, kv_spec, kv_spec] + [bias_spec(sub) for sub in range(NA_STEP_BLOCKS)],
        out_specs=qspec,
        out_shape=jax.ShapeDtypeStruct((n_lat, NA_WIDTH), BF16),
        compiler_params=_params(("arbitrary", "arbitrary")),
        name="na",
    )(nq, nk, nv, *([bias] * NA_STEP_BLOCKS))


def _ctx_kernel(q_ref, k_ref, v_ref, nq_ref, nk_ref, nv_ref, oa_ref, ob_ref):
    for h in range(MLA_HEADS):
        sl = slice(h * LANES, (h + 1) * LANES)
        s = _dot_t(q_ref[:, sl], k_ref[:, sl])
        p = jnp.exp2(s - s.max(axis=-1, keepdims=True))
        acc = _dot(p.astype(BF16), v_ref[:, sl])
        oa_ref[:, sl] = (acc * pl.reciprocal(acc[:, MLA_V:MLA_V + 1], approx=False)).astype(oa_ref.dtype)
    nq, nk, nv = nq_ref[...], nk_ref[...], nv_ref[...]
    outs = []
    for h in range(NA_HEADS):
        sl = slice(h * NA_DIM, (h + 1) * NA_DIM)
        outs.append(_softmax_pv([(_dot_t(nq[:, sl], nk[:, sl]), nv[:, sl])]))
    ob_ref[...] = jnp.concatenate(outs, axis=-1).astype(ob_ref.dtype)


def _ctx_call(q, k, v, nq, nk, nv, n_lat, n_ctx):
    blk = n_lat // n_ctx
    a_spec = pl.BlockSpec((n_ctx, HEAD_PAD), lambda i: (blk, 0))
    b_spec = pl.BlockSpec((n_ctx, NA_WIDTH), lambda i: (blk, 0))
    return pl.pallas_call(
        _ctx_kernel,
        grid=(1,),
        in_specs=[a_spec] * 3 + [b_spec] * 3,
        out_specs=[pl.BlockSpec((n_ctx, HEAD_PAD), lambda i: (0, 0)),
                   pl.BlockSpec((n_ctx, NA_WIDTH), lambda i: (0, 0))],
        out_shape=[jax.ShapeDtypeStruct((n_ctx, HEAD_PAD), BF16), jax.ShapeDtypeStruct((n_ctx, NA_WIDTH), BF16)],
        compiler_params=_params(("arbitrary",)),
        name="ctx_attn",
    )(q, k, v, nq, nk, nv)


S5_SG = LANES // S5_GCH
S5_SW = S5_SG * S5_STATE


def _s5prep_kernel(are_ref, aim_ref, ldt_ref, bre_ref, bim_ref, cre_ref, cim_ref,
                   arec_ref, aimc_ref, ldtc_ref, crec_ref, cimc_ref,
                   d_ref, bbr_ref, bbi_ref, wcr_ref, wci_ref, atr_ref, ati_ref):
    for ref in (d_ref, bbr_ref, bbi_ref, wcr_ref, wci_ref):
        ref[...] = jnp.zeros_like(ref)
    for g in range(S5_SG):
        rows, lanes = slice(g * S5_GCH, (g + 1) * S5_GCH), slice(g * S5_STATE, (g + 1) * S5_STATE)
        a_re, a_im = are_ref[g], aim_ref[g]
        dt = jnp.exp(ldt_ref[g])
        steps = lax.broadcasted_iota(jnp.int32, (S5_T + 1, S5_STATE), 0).astype(F32)
        mag = jnp.exp(a_re * dt * steps)
        p_re, p_im = mag * jnp.cos(a_im * dt * steps), mag * jnp.sin(a_im * dt * steps)
        nr, ni = p_re[1:2] - 1.0, p_im[1:2]
        den = 1.0 / (a_re * a_re + a_im * a_im)
        f_re, f_im = (nr * a_re + ni * a_im) * den, (ni * a_re - nr * a_im) * den
        b_re, b_im = bre_ref[g], bim_ref[g]
        bb_re, bb_im = f_re * b_re - f_im * b_im, f_re * b_im + f_im * b_re
        c_re, c_im = cre_ref[g], cim_ref[g]
        pw = [(p_re[e:e + 1], p_im[e:e + 1]) for e in range(S5_T + 1)]
        w_re = jnp.concatenate([c_re * r - c_im * i for r, i in pw[:S5_T]], axis=0)
        w_im = jnp.concatenate([c_re * i + c_im * r for r, i in pw[:S5_T]], axis=0)
        kt = _dot3_t(bb_re, w_re) - _dot3_t(bb_im, w_im)
        for e in range(S5_T):
            r, i = pw[e]
            d_ref[e, rows, rows] = kt[:, e * S5_GCH:(e + 1) * S5_GCH].astype(d_ref.dtype)
            bbr_ref[e, rows, lanes] = (r * bb_re - i * bb_im).astype(bbr_ref.dtype)
            bbi_ref[e, rows, lanes] = (r * bb_im + i * bb_re).astype(bbi_ref.dtype)
        atr_ref[:, lanes], ati_ref[:, lanes] = pw[S5_T]
        ac_re, ac_im = arec_ref[g], aimc_ref[g]
        dtc = jnp.exp(ldtc_ref[g])
        steps_c = lax.broadcasted_iota(jnp.int32, (S5_STATE, S5_T), 1).astype(F32) + 1.0
        mag_c = jnp.exp(ac_re * dtc * steps_c)
        q_re, q_im = mag_c * jnp.cos(ac_im * dtc * steps_c), mag_c * jnp.sin(ac_im * dtc * steps_c)
        ct_re, ct_im = crec_ref[g], cimc_ref[g]
        for e in range(S5_T):
            r, i = q_re[:, e:e + 1], q_im[:, e:e + 1]
            wcr_ref[e, lanes, rows] = (ct_re * r - ct_im * i).astype(wcr_ref.dtype)
            wci_ref[e, lanes, rows] = (-(ct_re * i + ct_im * r)).astype(wci_ref.dtype)


def _s5prep_call(a_re, a_im, log_dt, b_re, b_im, c_re, c_im):
    nd, g, p = a_re.shape
    cg = b_re.shape[-1]
    ns = g // S5_SG
    spec = lambda *s: pl.BlockSpec((None, S5_SG) + s, lambda d, j: (d, j) + (0,) * len(s))
    out = lambda *s: pl.BlockSpec((None, None) + s, lambda d, j: (d, j) + (0,) * len(s))
    sds = lambda t, *s: jax.ShapeDtypeStruct((nd, ns) + s, t)
    tr = lambda t: jnp.swapaxes(t, -1, -2)
    ldt = jnp.broadcast_to(log_dt[:, :, None, None], (nd, g, 1, p))
    return pl.pallas_call(
        _s5prep_kernel,
        grid=(nd, ns),
        in_specs=[spec(1, p)] * 3 + [spec(cg, p)] * 4 + [spec(p, 1)] * 3 + [spec(p, cg)] * 2,
        out_specs=[out(S5_T, LANES, LANES), out(S5_T, LANES, S5_SW), out(S5_T, LANES, S5_SW),
                   out(S5_T, S5_SW, LANES), out(S5_T, S5_SW, LANES), out(1, S5_SW), out(1, S5_SW)],
        out_shape=[sds(BF16, S5_T, LANES, LANES), sds(BF16, S5_T, LANES, S5_SW), sds(BF16, S5_T, LANES, S5_SW),
                   sds(BF16, S5_T, S5_SW, LANES), sds(BF16, S5_T, S5_SW, LANES), sds(F32, 1, S5_SW),
                   sds(F32, 1, S5_SW)],
        compiler_params=_params(("arbitrary", "arbitrary")),
        name="s5_prep",
    )(a_re[:, :, None, :], a_im[:, :, None, :], ldt, tr(b_re), tr(b_im), c_re, c_im,
      a_re[:, :, :, None], a_im[:, :, :, None], tr(ldt), tr(c_re), tr(c_im))


def _chunk_rows(u_ref, kb):
    return [u_ref[pl.ds(t, kb, stride=S5_T), :].astype(BF16) for t in range(S5_T)]


def _s5a_kernel(kb, u_ref, br_ref, bi_ref, ore_ref, oim_ref):
    x = _chunk_rows(u_ref, kb)
    x2 = [jnp.concatenate(x[t:t + 2], axis=1) for t in range(0, S5_T, 2)]
    for d in range(br_ref.shape[0]):
        ahead = lambda t: S5_T - 1 - t if d == 0 else t
        pair = lambda ref, t: jnp.concatenate([ref[d, ahead(t)], ref[d, ahead(t + 1)]], axis=0)
        ore_ref[d] = sum(_dot(x2[t // 2], pair(br_ref, t)) for t in range(0, S5_T, 2))
        oim_ref[d] = sum(_dot(x2[t // 2], pair(bi_ref, t)) for t in range(0, S5_T, 2))


def _s5_blocks(L):
    nch = L // S5_T
    return nch, _pick(nch, (208, 144, 80, 72, 40, 16, 8))


def _s5a_call(u, bcr, bci):
    L = u.shape[0]
    nd, ns, _, n, ws = bcr.shape
    nch, kb = _s5_blocks(L)
    wspec = pl.BlockSpec((nd, None, S5_T, n, ws), lambda s, r: (0, s, 0, 0, 0))
    ospec = pl.BlockSpec((nd, kb, ws), lambda s, r: (0, r, s))
    return pl.pallas_call(
        functools.partial(_s5a_kernel, kb),
        grid=(ns, nch // kb),
        in_specs=[pl.BlockSpec((kb * S5_T, LANES), lambda s, r: (r, s)), wspec, wspec],
        out_specs=[ospec, ospec],
        out_shape=[jax.ShapeDtypeStruct((nd, nch, ns * ws), F32)] * 2,
        compiler_params=_params(("arbitrary", "arbitrary")),
        name="s5_chunk_inputs",
    )(u, bcr, bci)


def _s5scan_kernel(n_lat_ch, n_ctx_ch, bre_ref, bim_ref, atr_ref, ati_ref, sre_ref, sim_ref):
    d = pl.program_id(0)
    a_re, a_im = atr_ref[...], ati_ref[...]

    def segment(base, count, carry):
        def step(i, st):
            s_re, s_im = st
            k = base + jnp.where(d == 0, i, count - 1 - i)
            sre_ref[k] = s_re
            sim_ref[k] = s_im
            return (a_re * s_re - a_im * s_im + bre_ref[k], a_re * s_im + a_im * s_re + bim_ref[k])
        return lax.fori_loop(0, count, step, carry)

    zero = jnp.zeros(a_re.shape, F32)
    carry = segment(n_lat_ch, n_ctx_ch, (zero, zero))
    segment(0, n_lat_ch, carry)


def _s5scan_call(b_re, b_im, atr, ati, n_lat_ch, n_ctx_ch):
    nd, nch, w = b_re.shape
    shp = (nd, nch, w // LANES, LANES)
    spec = pl.BlockSpec((None, nch, 8, LANES), lambda d, j: (d, 0, j, 0))
    aspec = pl.BlockSpec((None, 8, LANES), lambda d, j: (d, j, 0))
    return pl.pallas_call(
        functools.partial(_s5scan_kernel, n_lat_ch, n_ctx_ch),
        grid=(nd, w // LANES // 8),
        in_specs=[spec, spec, aspec, aspec],
        out_specs=[spec, spec],
        out_shape=[jax.ShapeDtypeStruct(shp, F32)] * 2,
        compiler_params=_params(("arbitrary", "arbitrary")),
        name="s5_scan",
    )(b_re.reshape(shp), b_im.reshape(shp), atr.reshape(nd, w // LANES, LANES), ati.reshape(nd, w // LANES, LANES))


def _s5c_kernel(kb, u_ref, sre_ref, sim_ref, d_ref, wr_ref, wi_ref, y_ref, acc_ref):
    nd = d_ref.shape[0]
    x = jnp.concatenate(_chunk_rows(u_ref, kb), axis=0)
    s_re = [sre_ref[d].astype(BF16) for d in range(nd)]
    s_im = [sim_ref[d].astype(BF16) for d in range(nd)]
    for t in range(0, S5_T, 2):
        later = lambda d, tt: tt if d == 0 else S5_T - 1 - tt
        both = lambda ref, d: jnp.concatenate([ref[d, later(d, t)], ref[d, later(d, t + 1)]], axis=1)
        r = sum(_dot(s_re[d], both(wr_ref, d)) + _dot(s_im[d], both(wi_ref, d)) for d in range(nd))
        acc_ref[t * kb:(t + 1) * kb, :] = r[:, :LANES]
        acc_ref[(t + 1) * kb:(t + 2) * kb, :] = r[:, LANES:]
    for e in range(0, S5_T, 2):
        n, n1 = (S5_T - e) * kb, (S5_T - e - 1) * kb
        both = lambda d: jnp.concatenate([d_ref[d, e], d_ref[d, e + 1]], axis=1)
        fwd = _dot(x[:n], both(0))
        acc_ref[e * kb:, :] += fwd[:, :LANES]
        acc_ref[(e + 1) * kb:, :] += fwd[:n1, LANES:]
        bwd = _dot(x[e * kb:], both(1))
        acc_ref[:n, :] += bwd[:, :LANES]
        acc_ref[:n1, :] += bwd[kb:, LANES:]
    for t in range(S5_T):
        y_ref[pl.ds(t, kb, stride=S5_T), :] = acc_ref[t * kb:(t + 1) * kb, :]


def _s5c_call(u, s_re, s_im, dm, wcr, wci):
    L = u.shape[0]
    nd, ns = dm.shape[:2]
    ws = wcr.shape[3]
    nch, kb = _s5_blocks(L)
    uspec = pl.BlockSpec((kb * S5_T, LANES), lambda s, r: (r, s))
    sspec = pl.BlockSpec((nd, kb, ws), lambda s, r: (0, r, s))
    wspec = pl.BlockSpec((nd, None, S5_T, ws, LANES), lambda s, r: (0, s, 0, 0, 0))
    return pl.pallas_call(
        functools.partial(_s5c_kernel, kb),
        grid=(ns, nch // kb),
        in_specs=[uspec, sspec, sspec,
                  pl.BlockSpec((nd, None, S5_T, LANES, LANES), lambda s, r: (0, s, 0, 0, 0)), wspec, wspec],
        out_specs=uspec,
        out_shape=jax.ShapeDtypeStruct(u.shape, F32),
        scratch_shapes=[pltpu.VMEM((kb * S5_T, LANES), F32)],
        compiler_params=_params(("arbitrary", "arbitrary")),
        name="s5_outputs",
    )(u, s_re, s_im, dm, wcr, wci)


def _s5_mix(u, ops, n_lat, n_ctx):
    dm, bcr, bci, wcr, wci, atr, ati = ops
    nd = dm.shape[0]
    b_re, b_im = _s5a_call(u, bcr, bci)
    s_re, s_im = _s5scan_call(b_re, b_im, atr.reshape(nd, -1), ati.reshape(nd, -1), n_lat // S5_T, n_ctx // S5_T)
    return _s5c_call(u, s_re.reshape(b_re.shape), s_im.reshape(b_im.shape), dm, wcr, wci)


def _gelu_tanh(x):
    return 0.5 * x * (1.0 + jnp.tanh(math.sqrt(2.0 / math.pi) * (x + 0.044715 * (x * x * x))))


def _post_kernel(n_lat, tm, moe,
                 x_ref, mod_ref, oa_ref, oac_ref, ob_ref, obc_ref, ys_ref, u_ref, gate_ref,
                 wa_ref, wb_ref, wc_ref, wglu_ref, bglu_ref, d_ref, wo_ref, g2_ref, *rest):
    i = pl.program_id(0)
    d = x_ref.shape[-1]
    g = _gelu_tanh(d_ref[...] * u_ref[...] + ys_ref[...])
    oc = (g * jax.nn.sigmoid(_dot(g.astype(BF16), wglu_ref[...]) + bglu_ref[...])).astype(BF16)
    gate = gate_ref[...].astype(F32)
    lat_tile = i < n_lat // tm
    o_a = jnp.where(lat_tile, oa_ref[...], oac_ref[...])
    o_b = jnp.where(lat_tile, ob_ref[...], obc_ref[...])
    mix = (gate[:, :d] * _dot(o_a, wa_ref[...]) + gate[:, d:2 * d] * _dot(o_b, wb_ref[...])
           + gate[:, 2 * d:] * _dot(oc, wc_ref[...]))
    x_new = x_ref[...] + _row_select(i, tm, n_lat, mod_ref, 2) * _dot(mix.astype(BF16), wo_ref[...])
    h2 = (_rms(x_new, g2_ref[...]) * (1.0 + _row_select(i, tm, n_lat, mod_ref, 4))
          + _row_select(i, tm, n_lat, mod_ref, 3))
    if not moe:
        xo_ref, h2_ref = rest
    else:
        wr_ref, tri_ref, xo_ref, h2_ref, route_ref, count_ref, carry_ref = rest
        logits = _dot3(h2, wr_ref[...])
        lane = lax.broadcasted_iota(jnp.int32, logits.shape, 1)
        logits = jnp.where(lane < N_EXPERTS, logits, NEG)
        m1 = logits.max(axis=-1, keepdims=True)
        i1 = jnp.where(logits == m1, lane, LANES).min(axis=-1, keepdims=True)
        rest_l = jnp.where(lane == i1, NEG, logits)
        m2 = rest_l.max(axis=-1, keepdims=True)
        i2 = jnp.where(rest_l == m2, lane, LANES).min(axis=-1, keepdims=True)
        e2 = jnp.exp(m2 - m1)
        w1 = 1.0 / (1.0 + e2)

        @pl.when(i == 0)
        def _():
            carry_ref[...] = jnp.zeros_like(carry_ref)

        chosen = jnp.where((lane == i1) | (lane == i2), 1.0, 0.0)
        before = _dot(tri_ref[...], chosen.astype(BF16)) + carry_ref[...]
        r1 = jnp.sum(jnp.where(lane == i1, before, 0.0), axis=-1, keepdims=True)
        r2 = jnp.sum(jnp.where(lane == i2, before, 0.0), axis=-1, keepdims=True)
        total = carry_ref[...] + jnp.sum(chosen, axis=0, keepdims=True)
        carry_ref[...] = total
        count_ref[...] = jnp.broadcast_to(total, count_ref.shape)
        cols = [i1.astype(F32), i2.astype(F32), r1, r2, w1, e2 * w1]
        route = jnp.zeros(logits.shape, F32)
        for j, col in enumerate(cols):
            route = jnp.where(lane == j, col, route)
        route_ref[...] = route
    xo_ref[...] = x_new
    h2_ref[...] = h2.astype(h2_ref.dtype)


def _post_call(x, mods, o_a, o_a_c, o_b, o_b_c, ys, u, gate, lw, n_lat, moe):
    L, d = x.shape
    n_ctx = L - n_lat
    tm = _pick(math.gcd(n_lat, n_ctx), (256, 128))
    n_lat_tiles = n_lat // tm
    row = lambda w: pl.BlockSpec((tm, w), lambda i: (i, 0))
    lat = lambda w: pl.BlockSpec((tm, w), lambda i: (jnp.minimum(i, n_lat_tiles - 1), 0))
    ctx = lambda w: pl.BlockSpec((tm, w), lambda i: (jnp.maximum(i - n_lat_tiles, 0), 0))
    weights = [lw[k] for k in ("wa", "wb", "wc", "wglu", "bglu", "d", "wo", "g2")]
    if moe:
        tri = jnp.asarray(np.tril(np.ones((tm, tm), np.float32), -1), BF16)
        weights += [lw["wrouter"], tri]
        outs = [(L, d, F32), (L, d, F32), (L, LANES, F32), (8, LANES, F32)]
        out_specs = [row(d), row(d), row(LANES), pl.BlockSpec((8, LANES), lambda i: (0, 0))]
        scratch = [pltpu.VMEM((1, LANES), F32)]
    else:
        outs = [(L, d, F32), (L, d, BF16)]
        out_specs = [row(d), row(d)]
        scratch = []
    return pl.pallas_call(
        functools.partial(_post_kernel, n_lat, tm, moe),
        grid=(L // tm,),
        in_specs=([row(d), _const_spec(mods.shape), lat(HEAD_PAD), ctx(HEAD_PAD), lat(NA_WIDTH), ctx(NA_WIDTH),
                   row(S5_WIDTH), row(S5_WIDTH), row(N_BRANCH * d)] + [_const_spec(w.shape) for w in weights]),
        out_specs=out_specs,
        out_shape=[jax.ShapeDtypeStruct((r, w), t) for r, w, t in outs],
        scratch_shapes=scratch,
        compiler_params=_params(("arbitrary",)),
        name="post_moe" if moe else "post",
    )(x, mods, o_a, o_a_c, o_b, o_b_c, ys, u, gate, *weights)


def _swiglu(h, wg_ref, wu_ref, wd_ref):
    a = _dot(h, wg_ref[...])
    act = (a * jax.nn.sigmoid(a) * _dot(h, wu_ref[...])).astype(BF16)
    return _dot(act, wd_ref[...])


def _ffn_kernel(n_lat, tm, x_ref, h_ref, mod_ref, wg_ref, wu_ref, wd_ref, o_ref):
    f = _swiglu(h_ref[...], wg_ref, wu_ref, wd_ref)
    o_ref[...] = x_ref[...] + _row_select(pl.program_id(0), tm, n_lat, mod_ref, 5) * f


def _ffn_call(x, h2, mods, wg, wu, wd, n_lat):
    L, d = x.shape
    tm = _pick(L, (640, 256, 128))
    row = lambda w: pl.BlockSpec((tm, w), lambda i: (i, 0))
    return pl.pallas_call(
        functools.partial(_ffn_kernel, n_lat, tm),
        grid=(L // tm,),
        in_specs=[row(d), row(d), _const_spec(mods.shape), _const_spec(wg.shape), _const_spec(wu.shape),
                  _const_spec(wd.shape)],
        out_specs=row(d),
        out_shape=jax.ShapeDtypeStruct((L, d), F32),
        compiler_params=_params(("arbitrary",)),
        name="ffn",
    )(x, h2, mods, wg, wu, wd)


MOE_TS = 512


def _row_copy(src, i, dst, j, sem):
    return pltpu.make_async_copy(src.at[pl.ds(i, 1)], dst.at[pl.ds(j, 1)], sem)


ROW_DMA_UNROLL = 8


def _dispatch_kernel(tb, n_tok, slot_ref, h_ref, xs_in, xs_hbm, sem):
    del xs_in
    base = pl.program_id(0) * tb

    def issue(t, carry):
        for k in range(TOP_K):
            _row_copy(h_ref, t, xs_hbm, slot_ref[k * n_tok + base + t], sem).start(priority=k % 2)
        return carry

    def drain(t, carry):
        for k in range(TOP_K):
            _row_copy(h_ref, 0, xs_hbm, 0, sem).wait()
        return carry

    lax.fori_loop(0, tb, issue, 0, unroll=ROW_DMA_UNROLL)
    lax.fori_loop(0, tb, drain, 0, unroll=ROW_DMA_UNROLL)


def _dispatch_call(slots, h2, n_slots):
    L, d = h2.shape
    tb = _pick(L, (640, 256, 128))
    xs0 = jnp.zeros((n_slots, d), h2.dtype)
    return pl.pallas_call(
        functools.partial(_dispatch_kernel, tb, L),
        grid_spec=pltpu.PrefetchScalarGridSpec(
            num_scalar_prefetch=1, grid=(L // tb,),
            in_specs=[pl.BlockSpec((tb, d), lambda i, s: (i, 0)), pl.BlockSpec(memory_space=pl.ANY)],
            out_specs=pl.BlockSpec(memory_space=pl.ANY),
            scratch_shapes=[pltpu.SemaphoreType.DMA(())]),
        out_shape=jax.ShapeDtypeStruct((n_slots, d), h2.dtype),
        input_output_aliases={2: 0},
        compiler_params=_params(("arbitrary",)),
        name="moe_dispatch",
    )(slots, h2, xs0)


def _expert_kernel(te_ref, nu_ref, x_ref, wg_ref, wu_ref, wd_ref, o_ref):
    used = pl.program_id(0) < nu_ref[0]

    @pl.when(used)
    def _():
        o_ref[...] = _swiglu(x_ref[...].astype(BF16), wg_ref, wu_ref, wd_ref)

    @pl.when(jnp.logical_not(used))
    def _():
        o_ref[...] = jnp.zeros_like(o_ref)


def _expert_call(tile_expert, n_used, xs, wg, wu, wd, layer):
    n_slots, d = xs.shape
    dff = wg.shape[-1]
    once = pl.Buffered(1)
    return pl.pallas_call(
        _expert_kernel,
        grid_spec=pltpu.PrefetchScalarGridSpec(
            num_scalar_prefetch=2, grid=(n_slots // MOE_TS,),
            in_specs=[pl.BlockSpec((MOE_TS, d), lambda j, te, nu: (j, 0)),
                      pl.BlockSpec((None, None, d, dff), lambda j, te, nu: (layer, te[j], 0, 0), pipeline_mode=once),
                      pl.BlockSpec((None, None, d, dff), lambda j, te, nu: (layer, te[j], 0, 0), pipeline_mode=once),
                      pl.BlockSpec((None, None, dff, d), lambda j, te, nu: (layer, te[j], 0, 0), pipeline_mode=once)],
            out_specs=pl.BlockSpec((MOE_TS, d), lambda j, te, nu: (j, 0))),
        out_shape=jax.ShapeDtypeStruct((n_slots, d), F32),
        compiler_params=_params(("arbitrary",)),
        name="moe_experts",
    )(tile_expert, n_used, xs, wg, wu, wd)


def _combine_kernel(n_lat, tb, n_tok, slot_ref, x_ref, route_ref, mod_ref, zs_hbm, o_ref, g_ref, sem):
    i = pl.program_id(0)
    base = i * tb

    def issue(t, carry):
        for k in range(TOP_K):
            _row_copy(zs_hbm, slot_ref[k * n_tok + base + t], g_ref.at[k], t, sem).start(priority=k % 2)
        return carry

    def drain(t, carry):
        for k in range(TOP_K):
            _row_copy(zs_hbm, 0, g_ref.at[k], 0, sem).wait()
        return carry

    lax.fori_loop(0, tb, issue, 0, unroll=ROW_DMA_UNROLL)
    lax.fori_loop(0, tb, drain, 0, unroll=ROW_DMA_UNROLL)
    route = route_ref[...]
    f = route[:, 4:5] * g_ref[0] + route[:, 5:6] * g_ref[1]
    o_ref[...] = x_ref[...] + _row_select(i, tb, n_lat, mod_ref, 5) * f


def _combine_call(slots, x, route, mods, zs, n_lat):
    L, d = x.shape
    tb = _pick(L, (320, 256, 128))
    row = lambda w: pl.BlockSpec((tb, w), lambda i, s: (i, 0))
    return pl.pallas_call(
        functools.partial(_combine_kernel, n_lat, tb, L),
        grid_spec=pltpu.PrefetchScalarGridSpec(
            num_scalar_prefetch=1, grid=(L // tb,),
            in_specs=[row(d), row(LANES), pl.BlockSpec(mods.shape, lambda i, s: (0, 0, 0)),
                      pl.BlockSpec(memory_space=pl.ANY)],
            out_specs=row(d),
            scratch_shapes=[pltpu.VMEM((TOP_K, tb, d), F32), pltpu.SemaphoreType.DMA(())]),
        out_shape=jax.ShapeDtypeStruct((L, d), F32),
        compiler_params=_params(("arbitrary",)),
        name="moe_combine",
    )(slots, x, route, mods, zs)


def _moe_call(x, h2, route, counts, mods, wg, wu, wd, layer, n_lat):
    L, d = x.shape
    ne = wg.shape[1]
    n_slots = (pl.cdiv(TOP_K * L, MOE_TS) + ne) * MOE_TS
    cnt = counts[0, :ne].astype(jnp.int32)
    size = (cnt + MOE_TS - 1) // MOE_TS * MOE_TS
    ends = jnp.cumsum(size)
    offs = ends - size
    eid = jnp.arange(ne, dtype=jnp.int32)
    slot = lambda e, r: jnp.sum(jnp.where(e[:, None] == eid[None, :], offs[None, :], 0), axis=1) + r
    r = route.astype(jnp.int32)
    slots = jnp.concatenate([slot(r[:, 0], r[:, 2]), slot(r[:, 1], r[:, 3])])
    starts = jnp.arange(n_slots // MOE_TS, dtype=jnp.int32) * MOE_TS
    tile_expert = jnp.minimum(jnp.sum(starts[:, None] >= ends[None, :], axis=1), ne - 1).astype(jnp.int32)
    n_used = (ends[-1:] // MOE_TS).astype(jnp.int32)
    xs = _dispatch_call(slots, h2, n_slots)
    zs = _expert_call(tile_expert, n_used, xs, wg, wu, wd, layer)
    return _combine_call(slots, x, route, mods, zs, n_lat)


def _rope_tables(n_lat, n_ctx):
    t = jnp.arange(n_lat, dtype=jnp.int32)
    n_freq = MLA_ROPE // 4
    inv = ROPE_THETA ** (-jnp.arange(n_freq, dtype=F32) / n_freq)
    ang = jnp.concatenate([(t // GRID_W).astype(F32)[:, None] * inv[None],
                           (t % GRID_W).astype(F32)[:, None] * inv[None]], axis=-1)
    cos, sin = jnp.cos(ang), jnp.sin(ang)
    ones = jnp.ones((n_lat, MLA_NOPE), F32)
    zn = jnp.zeros((n_lat, MLA_NOPE), F32)
    zh = jnp.zeros_like(sin)
    zp = jnp.zeros((n_lat, LANES - MLA_QK), F32)
    c = jnp.concatenate([ones, cos, cos, zp], axis=-1)
    s1 = jnp.concatenate([zn, -sin, zh, zp], axis=-1)
    s2 = jnp.concatenate([zn, zh, sin, zp], axis=-1)
    ctx = lambda a, fill: jnp.concatenate([a, jnp.full((n_ctx, LANES), fill, F32)], axis=0)
    return ctx(c, 1.0), ctx(s1, 0.0), ctx(s2, 0.0)


def _head_pad_cols(w, width):
    r = w.shape[0]
    w = w.reshape(r, MLA_HEADS, width)
    return jnp.pad(w, ((0, 0), (0, 0), (0, LANES - width))).reshape(r, HEAD_PAD)


def _layer_weights(i, p):
    d = D_MODEL
    w_in = p["w_in"][i]
    cuts = np.cumsum((0,) + IN_SPLITS)
    piece = lambda j: w_in[:, cuts[j]:cuts[j + 1]]
    bf = lambda w: w.astype(BF16)
    row = lambda v: v.reshape(1, -1).astype(F32)
    ukv = p["w_mla_ukv"][i].reshape(MLA_KV_RANK, MLA_HEADS, MLA_NOPE + MLA_V)
    wk_nope = _head_pad_cols(ukv[:, :, :MLA_NOPE].reshape(MLA_KV_RANK, -1), MLA_NOPE)
    kr_place = jnp.zeros((LANES, MLA_HEADS, LANES), F32)
    eye = jnp.eye(MLA_ROPE, dtype=F32)
    kr_place = kr_place.at[:MLA_ROPE, :, MLA_NOPE:MLA_QK].set(jnp.broadcast_to(eye[:, None, :], (MLA_ROPE, MLA_HEADS, MLA_ROPE)))
    vone = jnp.zeros((MLA_HEADS, LANES), F32).at[:, MLA_V].set(1.0).reshape(1, HEAD_PAD)
    head_gain = lambda g: jnp.tile(jnp.pad(g, (0, LANES - MLA_QK)), MLA_HEADS).reshape(1, HEAD_PAD)
    e64 = jnp.kron(jnp.eye(NA_HEADS, dtype=F32), jnp.ones((NA_DIM, NA_DIM), F32))
    wa = jnp.pad(p["w_br_mla"][i].reshape(MLA_HEADS, MLA_V, d), ((0, 0), (0, LANES - MLA_V), (0, 0)))
    return {
        "g1": row(p["g_norm1"][i]), "g2": row(p["g_norm2"][i]),
        "wq": bf(piece(0)), "wkv": bf(piece(1)),
        "wkr": bf(jnp.pad(piece(2), ((0, 0), (0, LANES - MLA_ROPE)))),
        "wnq": bf(piece(3)), "wnk": bf(piece(4)), "wnv": bf(piece(5)), "wu": bf(piece(6)), "wg": bf(piece(7)),
        "gq": row(p["g_mla_q"][i]), "gkv": row(p["g_mla_kv"][i]),
        "wuq": bf(_head_pad_cols(p["w_mla_uq"][i], MLA_QK)),
        "wk": bf(jnp.concatenate([wk_nope, kr_place.reshape(LANES, HEAD_PAD)], axis=0)),
        "wv": bf(_head_pad_cols(ukv[:, :, MLA_NOPE:].reshape(MLA_KV_RANK, -1), MLA_V)),
        "vone": vone,
        "gqn": head_gain(p["g_mla_qn"][i]), "gkn": head_gain(p["g_mla_kn"][i]),
        "e64": bf(e64),
        "gnq": jnp.tile(p["g_na_qn"][i], NA_HEADS).reshape(1, -1), "gnk": jnp.tile(p["g_na_kn"][i], NA_HEADS).reshape(1, -1),
        "wa": bf(wa.reshape(HEAD_PAD, d)), "wb": bf(p["w_br_na"][i]), "wc": bf(p["w_br_s5"][i]),
        "wglu": bf(p["w_glu"][i]), "bglu": row(p["b_glu"][i]), "d": row(p["s5_d"][i]), "wo": bf(p["w_out"][i]),
    }


def kernel(x, c, ctx, c_ctx, w_mod, b_mod, g_norm1, g_norm2, w_in, g_mla_q, g_mla_kv, w_mla_uq, w_mla_ukv,
           g_mla_qn, g_mla_kn, g_na_qn, g_na_kn, na_rpb, s5_a_re, s5_a_im, s5_log_dt, s5_b_re, s5_b_im,
           s5_c_re, s5_c_im, s5_d, w_glu, b_glu, w_br_mla, w_br_na, w_br_s5, w_out, w_ffn_gate, w_ffn_up,
           w_ffn_down, w_router, w_exp_gate, w_exp_up, w_exp_down):
    p = dict(w_in=w_in, g_norm1=g_norm1, g_norm2=g_norm2, g_mla_q=g_mla_q, g_mla_kv=g_mla_kv, w_mla_uq=w_mla_uq,
             w_mla_ukv=w_mla_ukv, g_mla_qn=g_mla_qn, g_mla_kn=g_mla_kn, g_na_qn=g_na_qn, g_na_kn=g_na_kn,
             s5_d=s5_d, w_glu=w_glu, b_glu=b_glu, w_br_mla=w_br_mla, w_br_na=w_br_na, w_br_s5=w_br_s5, w_out=w_out)
    assert x.shape[0] == 1 and x.shape[2] == D_MODEL
    n_lat, n_ctx = x.shape[1], ctx.shape[1]
    depth = w_mod.shape[0]
    xs = jnp.concatenate([x[0], ctx[0]], axis=0)

    cvec = jnp.zeros((8, D_MODEL), F32).at[0].set(c[0]).at[1].set(c_ctx)
    mods_all = _mod_call(cvec, w_mod, b_mod)[:, :2, None, :]
    rope = _rope_tables(n_lat, n_ctx)
    experts = (w_exp_gate.astype(BF16), w_exp_up.astype(BF16), w_exp_down.astype(BF16))

    for i in range(depth):
        lw = _layer_weights(i, p)
        mods = mods_all[i]
        q, k, v, nq, nk, nv, u, gate, qmax, kmax = _pre_call(xs, mods, n_lat, lw, rope)

        o_a = _mla_attend(q, k, v, qmax, kmax, n_lat)
        bias = _na_bias(na_rpb[i], n_lat // GRID_W)
        o_b = _na_call(nq, nk, nv, bias, n_lat, n_ctx)
        o_a_c, o_b_c = _ctx_call(q, k, v, nq, nk, nv, n_lat, n_ctx)
        attn = (o_a, o_a_c, o_b, o_b_c)

        ops = _s5prep_call(s5_a_re[i], s5_a_im[i], s5_log_dt[i], s5_b_re[i], s5_b_im[i], s5_c_re[i], s5_c_im[i])
        ys = _s5_mix(u, ops, n_lat, n_ctx)

        moe = i % 2 == 1
        j = i // 2
        if moe:
            lw["wrouter"] = jnp.pad(w_router[j], ((0, 0), (0, LANES - N_EXPERTS)))
            xs, h2, route, counts = _post_call(xs, mods, *attn, ys, u, gate, lw, n_lat, True)
            xs = _moe_call(xs, h2, route, counts, mods, *experts, j, n_lat)
        else:
            xs, h2 = _post_call(xs, mods, *attn, ys, u, gate, lw, n_lat, False)
            xs = _ffn_call(xs, h2, mods, w_ffn_gate[j].astype(BF16), w_ffn_up[j].astype(BF16),
                           w_ffn_down[j].astype(BF16), n_lat)
    return xs[:n_lat][None]
```
